```python
import math
import jax, jax.numpy as jnp
from jax import lax
import numpy as np

D_MODEL = 1024
BATCH = 4
SEQ = 4096
DEPTH = 4
DEC_BATCH = 128
DEC_SEQ = 8
PAST_LEN = 2048
PAGE_SIZE = 128

HEAD_DIM = 64
MIX_HEADS = D_MODEL // HEAD_DIM
X_HEADS = 4
TOK_HEADS = MIX_HEADS - X_HEADS
MIX_WIDTH = MIX_HEADS * HEAD_DIM
TOK_WIDTH = TOK_HEADS * HEAD_DIM
X_WIDTH = X_HEADS * HEAD_DIM
N_MEM = 256
NSA_KV_HEADS = 3
NSA_GROUP = TOK_HEADS // NSA_KV_HEADS
CMP_BLOCK = 32
CMP_STRIDE = 16
CMP_HID = 128
SLC_BLOCK = 64
SLC_TOPN = 8
WINDOW = 512
FORCE_BONUS = 1e4
N_BUCKETS = 32
MAX_DIST = 1024
Q_BLOCK = 128
FFN_HIDDEN = ((-(-8 * D_MODEL // 3) + 255) // 256) * 256
N_NSA_LAYERS = (DEPTH + 1) // 2
N_SB_LAYERS = DEPTH // 2
NSA_IN = TOK_WIDTH + 6 * NSA_KV_HEADS * HEAD_DIM + X_WIDTH + 3 * TOK_HEADS
SB_IN = 3 * TOK_WIDTH + X_WIDTH
RMS_EPS = 1e-6

kernel_name = "nsa_stickbreak_memory_hybrid_step"


def rmsnorm(x, g):
    xf = x.astype(jnp.float32)
    y = xf * lax.rsqrt(jnp.mean(xf * xf, axis=-1, keepdims=True) + RMS_EPS)
    return (y * g.astype(jnp.float32)).astype(x.dtype)


def rel_bucket(dist):
    exact = N_BUCKETS // 2
    d = jnp.maximum(dist, 0)
    far = exact + (jnp.log(jnp.maximum(d, 1).astype(jnp.float32) / exact)
                   / math.log(MAX_DIST / exact) * (N_BUCKETS - exact)).astype(jnp.int32)
    return jnp.where(d < exact, d, jnp.minimum(far, N_BUCKETS - 1))


def masked_softmax(s, mask):
    s = jnp.where(mask, s, -jnp.inf)
    m = jnp.max(s, axis=-1, keepdims=True)
    m = jnp.where(jnp.isfinite(m), m, 0.0)
    e = jnp.exp(s - m)
    return e / jnp.maximum(jnp.sum(e, axis=-1, keepdims=True), 1e-30)


def swiglu(x, w_in, w_out):
    a, u = jnp.split(x @ w_in, 2, axis=-1)
    return (jax.nn.silu(a) * u) @ w_out


def nsa_project(xn, w_in, qk_norm):
    b, t = xn.shape[:2]
    kvw = NSA_KV_HEADS * HEAD_DIM
    h = xn @ w_in
    q = rmsnorm(h[..., :TOK_WIDTH].reshape(b, t, TOK_HEADS, HEAD_DIM), qk_norm[0])
    kv = h[..., TOK_WIDTH:TOK_WIDTH + 6 * kvw].reshape(b, t, 6, NSA_KV_HEADS, HEAD_DIM)
    o = TOK_WIDTH + 6 * kvw
    xq = h[..., o:o + X_WIDTH].reshape(b, t, X_HEADS, HEAD_DIM)
    gates = jax.nn.sigmoid(h[..., o + X_WIDTH:]).reshape(b, t, 3, TOK_HEADS)
    rows = jnp.stack([kv[:, :, 0], kv[:, :, 1], rmsnorm(kv[:, :, 2], qk_norm[2]), kv[:, :, 3]], 2)
    rows_w = jnp.stack([rmsnorm(kv[:, :, 4], qk_norm[3]), kv[:, :, 5]], 2)
    return q, rows, rows_w, xq, gates


def compress_rows(rows, pe, w1, w2):
    b, L, g, hd = rows.shape
    n = (L - CMP_BLOCK) // CMP_STRIDE + 1
    idx = np.arange(n)[:, None] * CMP_STRIDE + np.arange(CMP_BLOCK)[None, :]
    blk = rows[:, idx] + pe[:, None, :]
    blk = jnp.swapaxes(blk, 2, 3).reshape(b, n, g, CMP_BLOCK * hd)
    return jax.nn.silu(blk @ w1) @ w2


def nsa_context(rows, pe, w1, w2, kn_cmp):
    b, L = rows.shape[:2]
    ck = rmsnorm(compress_rows(rows[:, :, 0], pe[0], w1[0], w2[0]), kn_cmp)
    cv = compress_rows(rows[:, :, 1], pe[1], w1[1], w2[1])
    n_cmp = ck.shape[1]
    c_end = (np.arange(n_cmp) * CMP_STRIDE + CMP_BLOCK - 1).astype(np.int32)
    n_slc = -(-L // SLC_BLOCK)
    pad = n_slc * SLC_BLOCK - L

    def blocks(r):
        r = jnp.pad(r, ((0, 0), (0, pad), (0, 0), (0, 0)))
        return r.reshape(b, n_slc, SLC_BLOCK, NSA_KV_HEADS, HEAD_DIM).transpose(0, 3, 1, 2, 4)

    sk = blocks(rows[:, :, 2])
    sv = blocks(rows[:, :, 3])
    cs = np.arange(n_cmp) * CMP_STRIDE
    ss = np.arange(n_slc) * SLC_BLOCK
    overlap = ((cs[:, None] < ss[None, :] + SLC_BLOCK) & (cs[:, None] + CMP_BLOCK > ss[None, :]))
    return ck, cv, c_end, jnp.asarray(overlap.astype(np.float32)), sk, sv


def nsa_attend(q, q_pos, gates, ck, cv, c_end, overlap, sk, sv, wk, wv, w_pos, rel_bias):
    b, tq = q.shape[:2]
    G, R = NSA_KV_HEADS, NSA_GROUP
    f32 = jnp.float32
    scale = HEAD_DIM ** -0.5
    qg = q.reshape(b, tq, G, R, HEAD_DIM)

    def head_bias(dist):
        bias = rel_bias[rel_bucket(dist)].reshape(dist.shape + (G, R))
        return jnp.moveaxis(bias, (-2, -1), (0, 1)).astype(f32)

    dist_c = q_pos[:, None] - c_end[None, :]
    s_c = jnp.einsum('bqgrd,bngd->bgrqn', qg, ck, preferred_element_type=f32) * scale + head_bias(dist_c)
    p_c = masked_softmax(s_c, dist_c >= 0)
    o_c = jnp.einsum('bgrqn,bngd->bqgrd', p_c.astype(cv.dtype), cv)

    n_slc = sk.shape[2]
    topn = min(SLC_TOPN, n_slc)
    imp = jnp.einsum('bgrqn,nj->bgqj', p_c, overlap)
    blk = jnp.arange(n_slc)
    cur = q_pos // SLC_BLOCK
    valid = blk[None, :] * SLC_BLOCK <= q_pos[:, None]
    forced = (blk[None, :] == 0) | (blk[None, :] == cur[:, None]) | (blk[None, :] == cur[:, None] - 1)
    score = jnp.where(valid, imp + jnp.where(forced, FORCE_BONUS, 0.0), -jnp.inf)
    _, sel = lax.top_k(score, topn)
    bi = jnp.arange(b)[:, None, None, None]
    gi = jnp.arange(G)[None, :, None, None]
    kg = sk[bi, gi, sel]
    vg = sv[bi, gi, sel]
    pos = sel[..., None] * SLC_BLOCK + jnp.arange(SLC_BLOCK)
    dist_s = q_pos[:, None, None] - pos
    tab = jnp.transpose(rel_bias.reshape(N_BUCKETS, G, R), (1, 0, 2))
    bias_s = tab[gi[..., None], rel_bucket(dist_s)]
    nsel = topn * SLC_BLOCK
    bias_s = jnp.moveaxis(bias_s, -1, 2).reshape(b, G, R, tq, nsel).astype(f32)
    s_s = jnp.einsum('bqgrd,bgqnkd->bgrqnk', qg, kg, preferred_element_type=f32).reshape(b, G, R, tq, nsel) * scale + bias_s
    mask_s = (dist_s >= 0).reshape(b, G, tq, nsel)[:, :, None]
    p_s = masked_softmax(s_s, mask_s)
    o_s = jnp.einsum('bgrqm,bgqmd->bqgrd', p_s.astype(vg.dtype), vg.reshape(b, G, tq, nsel, HEAD_DIM))

    dist_w = q_pos[:, None] - w_pos[None, :]
    mask_w = (dist_w >= 0) & (dist_w < WINDOW) & (w_pos[None, :] >= 0)
    s_w = jnp.einsum('bqgrd,bkgd->bgrqk', qg, wk, preferred_element_type=f32) * scale + head_bias(dist_w)
    p_w = masked_softmax(s_w, mask_w)
    o_w = jnp.einsum('bgrqk,bkgd->bqgrd', p_w.astype(wv.dtype), wv)

    o = jnp.stack([o_c, o_s, o_w], 2).reshape(b, tq, 3, TOK_HEADS, HEAD_DIM)
    return jnp.einsum('bqch,bqchd->bqhd', gates, o).reshape(b, tq, TOK_WIDTH)


def nsa_prompt(xn, w_in, qk_norm, pe, w1, w2, rel_bias):
    b, t = xn.shape[:2]
    q, rows, rows_w, xq, gates = nsa_project(xn, w_in, qk_norm)
    ck, cv, c_end, overlap, sk, sv = nsa_context(rows, pe, w1, w2, qk_norm[1])
    wpad = jnp.pad(rows_w, ((0, 0), (WINDOW, 0), (0, 0), (0, 0), (0, 0)))

    def block(i):
        q0 = i * Q_BLOCK
        q_pos = q0 + jnp.arange(Q_BLOCK, dtype=jnp.int32)
        w_pos = q0 - WINDOW + jnp.arange(WINDOW + Q_BLOCK, dtype=jnp.int32)
        wb = lax.dynamic_slice_in_dim(wpad, q0, WINDOW + Q_BLOCK, axis=1)
        return nsa_attend(lax.dynamic_slice_in_dim(q, q0, Q_BLOCK, 1), q_pos,
                          lax.dynamic_slice_in_dim(gates, q0, Q_BLOCK, 1),
                          ck, cv, c_end, overlap, sk, sv, wb[:, :, 0], wb[:, :, 1], w_pos, rel_bias)

    out = lax.map(block, jnp.arange(t // Q_BLOCK, dtype=jnp.int32))
    out = jnp.swapaxes(out, 0, 1).reshape(b, t, TOK_WIDTH)
    return out, xq, rows, rows_w[:, t - min(WINDOW, t):]


def nsa_sample(xn, pool, j, page_table, win_buf, w_in, qk_norm, pe, w1, w2, rel_bias):
    b, t = xn.shape[:2]
    q, rows, rows_w, xq, gates = nsa_project(xn, w_in, qk_norm)
    past = pool[j, page_table].reshape((b, PAST_LEN) + pool.shape[3:])
    ck, cv, c_end, overlap, sk, sv = nsa_context(jnp.concatenate([past, rows], 1), pe, w1, w2, qk_norm[1])
    w_all = jnp.concatenate([win_buf, rows_w], 1)
    nb = win_buf.shape[1]
    w_pos = PAST_LEN - nb + jnp.arange(nb + t, dtype=jnp.int32)
    q_pos = PAST_LEN + jnp.arange(t, dtype=jnp.int32)
    out = nsa_attend(q, q_pos, gates, ck, cv, c_end, overlap, sk, sv,
                     w_all[:, :, 0], w_all[:, :, 1], w_pos, rel_bias)
    return out, xq, rows, w_all[:, t:]


def stick_breaking(q, k, v, q_off):
    tq = q.shape[1]
    scale = HEAD_DIM ** -0.5
    outs = []
    for s0 in range(0, tq, Q_BLOCK):
        s1 = min(s0 + Q_BLOCK, tq)
        kend = q_off + s1
        z = jnp.einsum('bqhd,bkhd->bhqk', q[:, s0:s1], k[:, :kend], preferred_element_type=jnp.float32) * scale
        t_pos = q_off + np.arange(s0, s1)
        strict = np.arange(kend)[None, :] < t_pos[:, None]
        log_stay = jnp.where(strict, jax.nn.log_sigmoid(-z), 0.0)
        after = lax.cumsum(log_stay, axis=3, reverse=True) - log_stay
        a = jnp.where(strict, jnp.exp(jax.nn.log_sigmoid(z) + after), 0.0)
        outs.append(jnp.einsum('bhqk,bkhd->bqhd', a.astype(v.dtype), v[:, :kend]))
    return jnp.concatenate(outs, 1)


def sb_project(xn, w_in):
    b, t = xn.shape[:2]
    h = xn @ w_in
    qkv = h[..., :3 * TOK_WIDTH].reshape(b, t, 3, TOK_HEADS, HEAD_DIM)
    xq = h[..., 3 * TOK_WIDTH:].reshape(b, t, X_HEADS, HEAD_DIM)
    return qkv[:, :, 0], qkv[:, :, 1:], xq


def sb_prompt(xn, w_in):
    b, t = xn.shape[:2]
    q, rows, xq = sb_project(xn, w_in)
    out = stick_breaking(q, rows[:, :, 0], rows[:, :, 1], 0).reshape(b, t, TOK_WIDTH)
    return out, xq, rows


def sb_sample(xn, pool, j, page_table, w_in):
    b, t = xn.shape[:2]
    q, rows, xq = sb_project(xn, w_in)
    past = pool[j, page_table].reshape((b, PAST_LEN) + pool.shape[3:])
    full = jnp.concatenate([past, rows], 1)
    out = stick_breaking(q, full[:, :, 0], full[:, :, 1], PAST_LEN).reshape(b, t, TOK_WIDTH)
    return out, xq, rows


def memory_kv(mem, g, w_kv, kn):
    b, m = mem.shape[:2]
    kv = (rmsnorm(mem, g) @ w_kv).reshape(b, m, 2, X_HEADS, HEAD_DIM)
    return jnp.stack([rmsnorm(kv[:, :, 0], kn), kv[:, :, 1]], 2)


def memory_attend(xq, qn, mkv):
    b, t = xq.shape[:2]
    s = jnp.einsum('bqhd,bmhd->bhqm', rmsnorm(xq, qn), mkv[:, :, 0],
                   preferred_element_type=jnp.float32) * (HEAD_DIM ** -0.5)
    p = jax.nn.softmax(s, axis=-1)
    return jnp.einsum('bhqm,bmhd->bqhd', p.astype(mkv.dtype), mkv[:, :, 1]).reshape(b, t, X_WIDTH)


def setup_inputs(seed: int = 0) -> dict:
    key = jax.random.key(seed)
    ks = jax.random.split(key, 24)
    n_pages = PAST_LEN // PAGE_SIZE
    n_pool = (DEC_BATCH * n_pages * 5 + 3) // 4
    win_buf = min(WINDOW, PAST_LEN)

    def nrm(k, shape, scale=1.0):
        return jax.random.normal(k, shape, jnp.float32) * scale

    def gain(k, shape):
        return 1.0 + 0.05 * jax.random.normal(k, shape, jnp.float32)

    page_table = jax.random.permutation(ks[7], n_pool)[:DEC_BATCH * n_pages]
    return {
        "x_prompt": nrm(ks[0], (BATCH, SEQ, D_MODEL)),
        "x_sample": nrm(ks[1], (DEC_BATCH, DEC_SEQ, D_MODEL)),
        "mem_prompt": nrm(ks[2], (BATCH, N_MEM, D_MODEL)),
        "cache_nsa_kv": nrm(ks[3], (N_NSA_LAYERS, n_pool, PAGE_SIZE, 4, NSA_KV_HEADS, HEAD_DIM)),
        "cache_nsa_win": nrm(ks[4], (N_NSA_LAYERS, DEC_BATCH, win_buf, 2, NSA_KV_HEADS, HEAD_DIM)),
        "cache_sb_kv": nrm(ks[5], (N_SB_LAYERS, n_pool, PAGE_SIZE, 2, TOK_HEADS, HEAD_DIM)),
        "cache_mem_kv": nrm(ks[6], (DEPTH, DEC_BATCH, N_MEM, 2, X_HEADS, HEAD_DIM)),
        "page_table": page_table.reshape(DEC_BATCH, n_pages).astype(jnp.int32),
        "rel_bias": nrm(ks[8], (N_BUCKETS, TOK_HEADS), 0.5),
        "norm_mix": gain(ks[9], (DEPTH, D_MODEL)),
        "norm_ffn": gain(ks[10], (DEPTH, D_MODEL)),
        "norm_mem": gain(ks[11], (DEPTH, D_MODEL)),
        "w_in_nsa": nrm(ks[12], (N_NSA_LAYERS, D_MODEL, NSA_IN), D_MODEL ** -0.5),
        "w_in_sb": nrm(ks[13], (N_SB_LAYERS, D_MODEL, SB_IN), D_MODEL ** -0.5),
        "w_mem_kv": nrm(ks[14], (DEPTH, D_MODEL, 2 * X_WIDTH), D_MODEL ** -0.5),
        "w_out": nrm(ks[15], (DEPTH, MIX_WIDTH, D_MODEL), MIX_WIDTH ** -0.5),
        "nsa_qk_norm": gain(ks[16], (N_NSA_LAYERS, 4, HEAD_DIM)),
        "x_qk_norm": gain(ks[17], (DEPTH, 2, HEAD_DIM)),
        "cmp_pe": nrm(ks[18], (N_NSA_LAYERS, 2, CMP_BLOCK, HEAD_DIM), 0.1),
        "cmp_w1": nrm(ks[19], (N_NSA_LAYERS, 2, CMP_BLOCK * HEAD_DIM, CMP_HID), (CMP_BLOCK * HEAD_DIM) ** -0.5),
        "cmp_w2": nrm(ks[20], (N_NSA_LAYERS, 2, CMP_HID, HEAD_DIM), CMP_HID ** -0.5),
        "w_ffn_in": nrm(ks[21], (DEPTH, D_MODEL, 2 * FFN_HIDDEN), D_MODEL ** -0.5),
        "w_ffn_out": nrm(ks[22], (DEPTH, FFN_HIDDEN, D_MODEL), FFN_HIDDEN ** -0.5),
    }


def reference(x_prompt, x_sample, mem_prompt, cache_nsa_kv, cache_nsa_win, cache_sb_kv, cache_mem_kv,
              page_table, rel_bias, norm_mix, norm_ffn, norm_mem, w_in_nsa, w_in_sb, w_mem_kv, w_out,
              nsa_qk_norm, x_qk_norm, cmp_pe, cmp_w1, cmp_w2, w_ffn_in, w_ffn_out):
    xp, xs = x_prompt, x_sample
    nsa_p, nsa_s, win_p, win_s, sb_p, sb_s, mem_p = [], [], [], [], [], [], []
    for l in range(DEPTH):
        j = l // 2
        hp = rmsnorm(xp, norm_mix[l])
        hs = rmsnorm(xs, norm_mix[l])
        if l % 2 == 0:
            tp, qp, rows_p, wst_p = nsa_prompt(hp, w_in_nsa[j], nsa_qk_norm[j], cmp_pe[j], cmp_w1[j],
                                               cmp_w2[j], rel_bias)
            ts, qs, rows_s, wst_s = nsa_sample(hs, cache_nsa_kv, j, page_table, cache_nsa_win[j], w_in_nsa[j],
                                               nsa_qk_norm[j], cmp_pe[j], cmp_w1[j], cmp_w2[j], rel_bias)
            nsa_p.append(rows_p)
            nsa_s.append(rows_s)
            win_p.append(wst_p)
            win_s.append(wst_s)
        else:
            tp, qp, rows_p = sb_prompt(hp, w_in_sb[j])
            ts, qs, rows_s = sb_sample(hs, cache_sb_kv, j, page_table, w_in_sb[j])
            sb_p.append(rows_p)
            sb_s.append(rows_s)
        mkv_p = memory_kv(mem_prompt, norm_mem[l], w_mem_kv[l], x_qk_norm[l, 1])
        mem_p.append(mkv_p)
        xp = xp + jnp.concatenate([tp, memory_attend(qp, x_qk_norm[l, 0], mkv_p)], -1) @ w_out[l]
        xs = xs + jnp.concatenate([ts, memory_attend(qs, x_qk_norm[l, 0], cache_mem_kv[l])], -1) @ w_out[l]
        xp = xp + swiglu(rmsnorm(xp, norm_ffn[l]), w_ffn_in[l], w_ffn_out[l])
        xs = xs + swiglu(rmsnorm(xs, norm_ffn[l]), w_ffn_in[l], w_ffn_out[l])
    return (xp, xs, jnp.stack(nsa_p), jnp.stack(nsa_s), jnp.stack(win_p), jnp.stack(win_s),
            jnp.stack(sb_p), jnp.stack(sb_s), jnp.stack(mem_p))
```

```python
import functools
import math

import numpy as np
import jax
import jax.numpy as jnp
from jax import lax
from jax.experimental import pallas as pl
from jax.experimental.pallas import tpu as pltpu

F32 = jnp.float32
BF16 = jnp.bfloat16

D_MODEL = 1024
HEAD_DIM = 64
TOK_HEADS = 12
X_HEADS = 4
TOK_WIDTH = TOK_HEADS * HEAD_DIM
X_WIDTH = X_HEADS * HEAD_DIM
N_MEM = 256
NSA_G = 3
NSA_R = 4
CMP_BLOCK = 32
CMP_STRIDE = 16
CMP_HID = 128
SLC_BLOCK = 64
SLC_TOPN = 8
WINDOW = 512
FORCE_BONUS = 1e4
N_BUCKETS = 32
MAX_DIST = 1024
FFN_HIDDEN = 2816
RMS_EPS = 1e-6
PAGE = 128
SCALE = HEAD_DIM ** -0.5

LANES = 128
HALF = LANES // 2
VMEM_LIMIT = 56 * 1024 * 1024

EXP_ZERO = -104.0
NEG = -1e30


def _cp(sem, vmem=VMEM_LIMIT):
    return pltpu.CompilerParams(dimension_semantics=sem, vmem_limit_bytes=vmem)


def _dot(a, b):
    return jnp.dot(a, b, preferred_element_type=F32)


def _dot_t(a, b):
    return lax.dot_general(a, b, (((1,), (1,)), ((), ())), preferred_element_type=F32)


def _split_dot(x, w):
    hi = x.astype(BF16)
    lo = (x - hi.astype(F32)).astype(BF16)
    return _dot(hi, w) + _dot(lo, w)


def _rms_rows(x, g):
    ms = jnp.mean(x * x, axis=-1, keepdims=True)
    return x * lax.rsqrt(ms + RMS_EPS) * g


def _head_scale(h, gain, flag, s128):
    x2 = h * h
    parts = []
    for k in range(h.shape[1] // LANES):
        parts.append(_split_dot(x2[:, LANES * k:LANES * (k + 1)], s128))
    ssq = parts[0] if len(parts) == 1 else jnp.concatenate(parts, axis=1)
    r = lax.rsqrt(ssq * (1.0 / HEAD_DIM) + RMS_EPS)
    return h * jnp.where(flag > 0.5, r * gain, 1.0)


def _sigmoid(x):
    return 1.0 / (1.0 + jnp.exp(-x))


def _softplus(z):
    return jnp.maximum(z, 0.0) + jnp.log1p(jnp.exp(-jnp.abs(z)))


def _group_ones():
    i = np.arange(LANES)
    return jnp.asarray((i[:, None] // HALF == i[None, :] // HALF).astype(np.float32), BF16)


def _inproj_body(x_ref, g_ref, w_ref, gf_ref, s_ref, *out_refs, chunks):
    xn = _rms_rows(x_ref[...], g_ref[...]).astype(BF16)
    for c0, c1, norm, outs in chunks:
        h = _dot(xn, w_ref[:, c0:c1])
        if norm:
            h = _head_scale(h, gf_ref[0:1, c0:c1], gf_ref[1:2, c0:c1], s_ref[...])
        for oi, o0, kind in outs:
            o_ref = out_refs[oi]
            if kind == "sigmoid":
                o_ref[:, o0:o0 + (c1 - c0)] = _sigmoid(h)
            else:
                o_ref[:, o0:o0 + (c1 - c0)] = h.astype(o_ref.dtype)


def _inproj(x, g, w, gf, chunks, out_defs, tm):
    m = x.shape[0]
    n = w.shape[1]
    assert m % tm == 0
    out_shape = [jax.ShapeDtypeStruct((m, wd), dt) for wd, dt in out_defs]
    out_specs = [pl.BlockSpec((tm, wd), lambda i: (i, 0)) for wd, _ in out_defs]
    return pl.pallas_call(
        functools.partial(_inproj_body, chunks=chunks),
        grid=(m // tm,),
        in_specs=[
            pl.BlockSpec((tm, D_MODEL), lambda i: (i, 0)),
            pl.BlockSpec((1, D_MODEL), lambda i: (0, 0)),
            pl.BlockSpec((D_MODEL, n), lambda i: (0, 0)),
            pl.BlockSpec((2, n), lambda i: (0, 0)),
            pl.BlockSpec((LANES, LANES), lambda i: (0, 0)),
        ],
        out_specs=out_specs,
        out_shape=out_shape,
        compiler_params=_cp(("parallel",)),
        name="inproj",
    )(x, g, w, gf, _group_ones())


NSA_N = 2304
NSA_CHUNKS = (
    (0, 384, True, ((0, 0, "cast"),)),
    (384, 768, True, ((0, 384, "cast"),)),
    (768, 1152, False, ((1, 0, "cast"), (3, 0, "cast"))),
    (1152, 1536, True, ((1, 384, "cast"), (3, 384, "cast"))),
    (1536, 1920, True, ((2, 0, "cast"), (3, 768, "cast"))),
    (1920, 2176, True, ((4, 0, "cast"),)),
    (2176, 2304, False, ((5, 0, "sigmoid"),)),
)
SB_N = 2560
SB_CHUNKS = (
    (0, 384, False, ((0, 0, "cast"),)),
    (384, 768, False, ((0, 384, "cast"),)),
    (768, 1152, False, ((1, 0, "cast"), (2, 0, "cast"))),
    (1152, 1536, False, ((1, 384, "cast"), (2, 384, "cast"))),
    (1536, 1920, False, ((1, 768, "cast"), (2, 768, "cast"))),
    (1920, 2304, False, ((1, 1152, "cast"), (2, 1152, "cast"))),
    (2304, 2560, True, ((3, 0, "cast"),)),
)
NSA_CHUNKS_S = (
    (0, 384, True, ((0, 0, "cast"),)),
    (384, 768, True, ((0, 384, "cast"),)),
    (768, 1152, False, ((1, 0, "cast"),)),
    (1152, 1536, True, ((1, 384, "cast"),)),
    (1536, 1920, True, ((2, 0, "cast"),)),
    (1920, 2176, True, ((3, 0, "cast"),)),
    (2176, 2304, False, ((4, 0, "sigmoid"),)),
)
SB_CHUNKS_S = (
    (0, 384, False, ((0, 0, "cast"),)),
    (384, 768, False, ((0, 384, "cast"),)),
    (768, 1152, False, ((1, 0, "cast"),)),
    (1152, 1536, False, ((1, 384, "cast"),)),
    (1536, 1920, False, ((1, 768, "cast"),)),
    (1920, 2304, False, ((1, 1152, "cast"),)),
    (2304, 2560, True, ((2, 0, "cast"),)),
)
MEM_CHUNKS = (
    (0, 256, True, ((0, 0, "cast"), (1, 0, "cast"))),
    (256, 512, False, ((0, 256, "cast"), (1, 256, "cast"))),
)


def _nsa_in_weights(w_in, qk_norm, xq_gain):
    o = TOK_WIDTH + 6 * NSA_G * HEAD_DIM
    w = jnp.concatenate([w_in[:, :o + X_WIDTH],
                         jnp.pad(w_in[:, o + X_WIDTH:], ((0, 0), (0, LANES - 3 * TOK_HEADS)))], axis=1)
    kvw = NSA_G * HEAD_DIM
    one = jnp.ones((kvw,), F32)
    zero = jnp.zeros((kvw,), F32)
    gain = jnp.concatenate([jnp.tile(qk_norm[0], TOK_HEADS), one, one, jnp.tile(qk_norm[2], NSA_G), one,
                            jnp.tile(qk_norm[3], NSA_G), one, jnp.tile(xq_gain, X_HEADS), jnp.ones((LANES,), F32)])
    flag = jnp.concatenate([jnp.ones((TOK_WIDTH,), F32), zero, zero, one, zero, one, zero,
                            jnp.ones((X_WIDTH,), F32), jnp.zeros((LANES,), F32)])
    return w.astype(BF16), jnp.stack([gain, flag])


def _sb_in_weights(w_in, xq_gain):
    gain = jnp.concatenate([jnp.ones((3 * TOK_WIDTH,), F32), jnp.tile(xq_gain, X_HEADS)])
    flag = jnp.concatenate([jnp.zeros((3 * TOK_WIDTH,), F32), jnp.ones((X_WIDTH,), F32)])
    return w_in.astype(BF16), jnp.stack([gain, flag])


def _mem_in_weights(w_kv, k_gain):
    gain = jnp.concatenate([jnp.tile(k_gain, X_HEADS), jnp.ones((X_WIDTH,), F32)])
    flag = jnp.concatenate([jnp.ones((X_WIDTH,), F32), jnp.zeros((X_WIDTH,), F32)])
    return w_kv.astype(BF16), jnp.stack([gain, flag])


def _ffn_body(x_ref, tok_ref, mem_ref, wot_ref, wom_ref, g_ref, wa_ref, wu_ref, wo_ref, o_ref, acc_ref, xn_ref):
    j = pl.program_id(1)

    @pl.when(j == 0)
    def _():
        xm = (x_ref[...] + _dot(tok_ref[...].astype(BF16), wot_ref[...])
              + _dot(mem_ref[...].astype(BF16), wom_ref[...]))
        acc_ref[...] = xm
        xn_ref[...] = _rms_rows(xm, g_ref[...]).astype(BF16)

    xn = xn_ref[...]
    a = _dot(xn, wa_ref[...])
    u = _dot(xn, wu_ref[...])
    hsw = (a * _sigmoid(a) * u).astype(BF16)
    acc_ref[...] += _dot(hsw, wo_ref[...])

    @pl.when(j == pl.num_programs(1) - 1)
    def _():
        o_ref[...] = acc_ref[...]


def _out_ffn(x, tok, mem, wo_tok, wo_mem, g, w_in, w_out, tm, th):
    m = x.shape[0]
    nh = FFN_HIDDEN // th
    return pl.pallas_call(
        _ffn_body,
        grid=(m // tm, nh),
        in_specs=[
            pl.BlockSpec((tm, D_MODEL), lambda i, j: (i, 0)),
            pl.BlockSpec((tm, TOK_WIDTH), lambda i, j: (i, 0)),
            pl.BlockSpec((tm, X_WIDTH), lambda i, j: (i, 0)),
            pl.BlockSpec((TOK_WIDTH, D_MODEL), lambda i, j: (0, 0)),
            pl.BlockSpec((X_WIDTH, D_MODEL), lambda i, j: (0, 0)),
            pl.BlockSpec((1, D_MODEL), lambda i, j: (0, 0)),
            pl.BlockSpec((D_MODEL, th), lambda i, j: (0, j)),
            pl.BlockSpec((D_MODEL, th), lambda i, j: (0, j + nh)),
            pl.BlockSpec((th, D_MODEL), lambda i, j: (j, 0)),
        ],
        out_specs=pl.BlockSpec((tm, D_MODEL), lambda i, j: (i, 0)),
        out_shape=jax.ShapeDtypeStruct((m, D_MODEL), F32),
        scratch_shapes=[pltpu.VMEM((tm, D_MODEL), F32), pltpu.VMEM((tm, D_MODEL), BF16)],
        compiler_params=_cp(("parallel", "arbitrary")),
        name="out_ffn",
    )(x, tok, mem, wo_tok, wo_mem, g, w_in, w_in, w_out)


def _mem_heads(q, kv, lane):
    outs = []
    for c in range(X_WIDTH // LANES):
        q2 = q[:, LANES * c:LANES * (c + 1)]
        k2 = kv[:, LANES * c:LANES * (c + 1)]
        v2 = kv[:, X_WIDTH + LANES * c:X_WIDTH + LANES * (c + 1)]
        halves = []
        for half in range(2):
            sel = (lane < HALF) if half == 0 else (lane >= HALF)
            s = _dot_t(jnp.where(sel, q2, jnp.zeros_like(q2)), k2) * SCALE
            e = jnp.exp(s - jnp.max(s, axis=-1, keepdims=True))
            p = e / jnp.sum(e, axis=-1, keepdims=True)
            halves.append(_dot(p.astype(BF16), v2))
        outs.append(jnp.where(lane < HALF, halves[0], halves[1]))
    return jnp.concatenate(outs, axis=1)


def _memattn_p_body(q_ref, kv_ref, o_ref):
    lane = lax.broadcasted_iota(jnp.int32, (q_ref.shape[0], LANES), 1)
    o_ref[...] = _mem_heads(q_ref[...], kv_ref[...], lane).astype(o_ref.dtype)


def _memattn_prompt(xq, mkv, batch, tq):
    m = xq.shape[0]
    nt = m // batch // tq
    return pl.pallas_call(
        _memattn_p_body,
        grid=(batch, nt),
        in_specs=[pl.BlockSpec((tq, X_WIDTH), lambda b, i: (b * nt + i, 0)),
                  pl.BlockSpec((N_MEM, 2 * X_WIDTH), lambda b, i: (b, 0))],
        out_specs=pl.BlockSpec((tq, X_WIDTH), lambda b, i: (b * nt + i, 0)),
        out_shape=jax.ShapeDtypeStruct((m, X_WIDTH), BF16),
        compiler_params=_cp(("parallel", "parallel")),
        name="memattn_prompt",
    )(xq, mkv)


def _memattn_s_body(q_ref, kv_ref, o_ref, *, nb, t):
    lane = lax.broadcasted_iota(jnp.int32, (t, LANES), 1)
    for bi in range(nb):
        q = q_ref[bi * t:(bi + 1) * t, :].astype(BF16)
        o_ref[bi * t:(bi + 1) * t, :] = _mem_heads(q, kv_ref[bi].astype(BF16), lane)


def _memattn_sample(xq, mkv, layer, t, nb):
    m = xq.shape[0]
    batch = m // t
    return pl.pallas_call(
        functools.partial(_memattn_s_body, nb=nb, t=t),
        grid=(batch // nb,),
        in_specs=[pl.BlockSpec((nb * t, X_WIDTH), lambda i: (i, 0)),
                  pl.BlockSpec((None, nb, N_MEM, 2 * X_WIDTH), lambda i: (layer, i, 0, 0))],
        out_specs=pl.BlockSpec((nb * t, X_WIDTH), lambda i: (i, 0)),
        out_shape=jax.ShapeDtypeStruct((m, X_WIDTH), F32),
        compiler_params=_cp(("parallel",)),
        name="memattn_sample",
    )(xq, mkv)


def _sb_block(z, strict, carry, u_tri):
    sp = _softplus(z)
    ls = -sp if strict is None else jnp.where(strict, -sp, 0.0)
    after = _split_dot(ls, u_tri) + carry
    a = jnp.exp(z - sp + after)
    if strict is not None:
        a = jnp.where(strict, a, 0.0)
    return a, carry + jnp.sum(ls, axis=1, keepdims=True)


def _upper_tri(n):
    r = lax.broadcasted_iota(jnp.int32, (n, n), 0)
    c = lax.broadcasted_iota(jnp.int32, (n, n), 1)
    return jnp.where(r > c, 1.0, 0.0).astype(BF16)


def _sb_prompt_body(q_ref, k_ref, v_ref, o_ref, *, tq):
    i = pl.program_id(1)
    q0 = i * tq
    row = lax.broadcasted_iota(jnp.int32, (tq, tq), 0)
    col = lax.broadcasted_iota(jnp.int32, (tq, tq), 1)
    u_tri = _upper_tri(tq)
    lane = lax.broadcasted_iota(jnp.int32, (tq, LANES), 1)
    for pair in range(TOK_WIDTH // LANES):
        lanes = slice(LANES * pair, LANES * (pair + 1))
        q2 = q_ref[:, lanes]
        zero = jnp.zeros_like(q2)
        qh = (jnp.where(lane < HALF, q2, zero), jnp.where(lane >= HALF, q2, zero))

        def cond(c):
            kb, c0, c1, _, _ = c
            return jnp.logical_and(kb >= 0, jnp.maximum(jnp.max(c0), jnp.max(c1)) > EXP_ZERO)

        def body(c):
            kb, c0, c1, a0, a1 = c
            ks = pl.multiple_of(kb * tq, tq)
            kblk = k_ref[pl.ds(ks, tq), lanes]
            vblk = v_ref[pl.ds(ks, tq), lanes]
            strict = (ks + col) < (q0 + row)
            w0, c0 = _sb_block(_dot_t(qh[0], kblk) * SCALE, strict, c0, u_tri)
            w1, c1 = _sb_block(_dot_t(qh[1], kblk) * SCALE, strict, c1, u_tri)
            a0 = a0 + _dot(w0.astype(BF16), vblk)
            a1 = a1 + _dot(w1.astype(BF16), vblk)
            return kb - 1, c0, c1, a0, a1

        zc = jnp.zeros((tq, 1), F32)
        za = jnp.zeros((tq, LANES), F32)
        _, _, _, a0, a1 = lax.while_loop(cond, body, (i, zc, zc, za, za))
        o_ref[:, lanes] = jnp.where(lane < HALF, a0, a1).astype(o_ref.dtype)


def _sb_prompt(q, kvb, batch, tq):
    m = q.shape[0]
    t = m // batch
    nt = t // tq
    return pl.pallas_call(
        functools.partial(_sb_prompt_body, tq=tq),
        grid=(batch, nt),
        in_specs=[pl.BlockSpec((tq, TOK_WIDTH), lambda b, i: (b * nt + i, 0)),
                  pl.BlockSpec((t, TOK_WIDTH), lambda b, i: (b, 0)),
                  pl.BlockSpec((t, TOK_WIDTH), lambda b, i: (b, 1))],
        out_specs=pl.BlockSpec((tq, TOK_WIDTH), lambda b, i: (b * nt + i, 0)),
        out_shape=jax.ShapeDtypeStruct((m, TOK_WIDTH), BF16),
        compiler_params=_cp(("parallel", "parallel")),
        name="sb_prompt",
    )(q, kvb, kvb)


def _sb_sample_body(pt_ref, q_ref, new_ref, *rest, t, n_pages):
    page_refs = rest[:n_pages]
    o_ref = rest[n_pages]
    qbd_ref, acc_ref, carry_ref, blk_ref = rest[n_pages + 1:]
    del pt_ref
    rows = TOK_HEADS * t
    lane = lax.broadcasted_iota(jnp.int32, (rows, TOK_WIDTH), 1)
    rowi = lax.broadcasted_iota(jnp.int32, (rows, TOK_WIDTH), 0)
    own = (lane // HEAD_DIM) == (rowi // t)
    q = q_ref[0] * SCALE
    qbd_ref[...] = jnp.where(own, jnp.concatenate([q] * TOK_HEADS, axis=0), 0.0).astype(BF16)
    u_tri = _upper_tri(PAGE)

    blk_ref[...] = jnp.zeros_like(blk_ref)
    blk_ref[0:t, :] = new_ref[0]
    kcol = lax.broadcasted_iota(jnp.int32, (rows, PAGE), 1)
    trow = lax.broadcasted_iota(jnp.int32, (rows, PAGE), 0) % t
    strict = kcol < trow
    kv = blk_ref[...]
    z = _dot_t(qbd_ref[...], kv[:, :TOK_WIDTH].astype(BF16))
    w, carry = _sb_block(z, strict, jnp.zeros((rows, 1), F32), u_tri)
    acc_ref[...] = _dot(w.astype(BF16), kv[:, TOK_WIDTH:].astype(BF16))
    carry_ref[...] = carry

    for p in range(n_pages - 1, -1, -1):
        @pl.when(jnp.max(carry_ref[...]) > EXP_ZERO)
        def _(p=p):
            kv = page_refs[p][...]
            z = _dot_t(qbd_ref[...], kv[:, :TOK_WIDTH].astype(BF16))
            w, carry = _sb_block(z, None, carry_ref[...], u_tri)
            acc_ref[...] += _dot(w.astype(BF16), kv[:, TOK_WIDTH:].astype(BF16))
            carry_ref[...] = carry

    acc = jnp.where(own, acc_ref[...], 0.0)
    out = acc[0:t]
    for h in range(1, TOK_HEADS):
        out = out + acc[h * t:(h + 1) * t]
    o_ref[0] = out


def _sb_sample(q, new_rows, pool, layer, page_table):
    batch, t, _ = q.shape
    n_pages = page_table.shape[1]
    rows = TOK_HEADS * t
    page_specs = [
        pl.BlockSpec((None, None, PAGE, 2 * TOK_WIDTH), lambda b, pt, p=p: (layer, pt[b, p], 0, 0))
        for p in range(n_pages)
    ]
    grid_spec = pltpu.PrefetchScalarGridSpec(
        num_scalar_prefetch=1,
        grid=(batch,),
        in_specs=[pl.BlockSpec((1, t, TOK_WIDTH), lambda b, pt: (b, 0, 0)),
                  pl.BlockSpec((1, t, 2 * TOK_WIDTH), lambda b, pt: (b, 0, 0))] + page_specs,
        out_specs=pl.BlockSpec((1, t, TOK_WIDTH), lambda b, pt: (b, 0, 0)),
        scratch_shapes=[pltpu.VMEM((rows, TOK_WIDTH), BF16), pltpu.VMEM((rows, TOK_WIDTH), F32),
                        pltpu.VMEM((rows, 1), F32), pltpu.VMEM((PAGE, 2 * TOK_WIDTH), F32)],
    )
    return pl.pallas_call(
        functools.partial(_sb_sample_body, t=t, n_pages=n_pages),
        grid_spec=grid_spec,
        out_shape=jax.ShapeDtypeStruct((batch, t, TOK_WIDTH), F32),
        compiler_params=_cp(("arbitrary",)),
        name="sb_sample",
    )(page_table, q, new_rows, *([pool] * n_pages))


def _masked_softmax(s, mask):
    s = jnp.where(mask, s, -jnp.inf)
    m = jnp.max(s, axis=-1, keepdims=True)
    m = jnp.where(m == -jnp.inf, 0.0, m)
    e = jnp.exp(s - m)
    return e / jnp.maximum(jnp.sum(e, axis=-1, keepdims=True), 1e-30)


def _compress(load_chunks, nch, wc_ref, w1_ref, pe_ref, w2p_ref, gf_ref, s128):
    acc = [jnp.zeros((nch, 4 * CMP_HID), F32) for _ in range(3)]
    for p in range(CMP_STRIDE):
        for c in range(3):
            acc[c] = acc[c] + _dot(load_chunks(p, c).astype(BF16), wc_ref[c, p])
    bias = [_dot(pe_ref[tt], w1_ref[tt])[0:1] for tt in range(2)]
    outs = []
    for c, (ta, tb) in enumerate(((0, 0), (0, 1), (1, 1))):
        hid = []
        for half, tt in ((0, ta), (1, tb)):
            first = acc[c][:, 2 * CMP_HID * half:2 * CMP_HID * half + CMP_HID]
            second = acc[c][:, 2 * CMP_HID * half + CMP_HID:2 * CMP_HID * (half + 1)]
            pre = first + pltpu.roll(second, nch - 1, 0) + bias[tt]
            hid.append((pre * _sigmoid(pre)).astype(BF16))
        outs.append(_dot(hid[0], w2p_ref[ta, 0]) + _dot(hid[1], w2p_ref[tb, 1]))
    ckv = jnp.concatenate(outs, axis=1)
    return _head_scale(ckv, gf_ref[0:1, :], gf_ref[1:2, :], s128)


def _select_topn(score, topn):
    lane = lax.broadcasted_iota(jnp.int32, score.shape, 1)
    sel = jnp.zeros(score.shape, F32)
    cur = score
    for _ in range(topn):
        mx = jnp.max(cur, axis=1, keepdims=True)
        is_max = jnp.logical_and(cur == mx, mx > -jnp.inf)
        idx = jnp.min(jnp.where(is_max, lane, LANES), axis=1, keepdims=True)
        pick = lane == idx
        sel = jnp.where(pick, 1.0, sel)
        cur = jnp.where(pick, -jnp.inf, cur)
    return sel


def _stack_heads(q, pq_ref, g, t):
    return jnp.concatenate([_dot(q, pq_ref[NSA_R * g + r])[0:t] for r in range(NSA_R)], axis=0).astype(BF16)


def _place_heads(o, g, t, gate_of, lane):
    vh = (g + 1) % 2
    tiles = []
    for rp in range(2):
        outs = []
        for rr in range(2):
            r = 2 * rp + rr
            o_r = o[r * t:(r + 1) * t]
            if vh != rr:
                o_r = pltpu.roll(o_r, HALF, 1)
            outs.append(o_r * gate_of(NSA_R * g + r))
        tiles.append(jnp.where(lane < HALF, outs[0], outs[1]))
    return tiles


def _nsa_compress_body(r0_ref, r1_ref, r2_ref, wc_ref, w1_ref, pe_ref, w2p_ref, gf_ref, s_ref, o_ref, *, nch):
    tiles = (r0_ref, r1_ref, r2_ref)

    def load(p, c):
        return tiles[c][pl.ds(p, nch, stride=CMP_STRIDE), :]
    o_ref[...] = _compress(load, nch, wc_ref, w1_ref, pe_ref, w2p_ref, gf_ref, s_ref[...])


def _cmp_const_specs():
    return [pl.BlockSpec((3, CMP_STRIDE, LANES, 4 * CMP_HID), lambda *a: (0, 0, 0, 0)),
            pl.BlockSpec((2, CMP_BLOCK * HEAD_DIM, CMP_HID), lambda *a: (0, 0, 0)),
            pl.BlockSpec((2, 8, CMP_BLOCK * HEAD_DIM), lambda *a: (0, 0, 0)),
            pl.BlockSpec((2, 2, CMP_HID, LANES), lambda *a: (0, 0, 0, 0)),
            pl.BlockSpec((2, 3 * LANES), lambda *a: (0, 0)),
            pl.BlockSpec((LANES, LANES), lambda *a: (0, 0))]


def _nsa_compress(rows, cw, batch):
    m = rows.shape[0]
    t = m // batch
    nch = t // CMP_STRIDE
    return pl.pallas_call(
        functools.partial(_nsa_compress_body, nch=nch),
        grid=(batch,),
        in_specs=[pl.BlockSpec((t, LANES), lambda b, c=c: (b, c)) for c in range(3)] + _cmp_const_specs(),
        out_specs=pl.BlockSpec((nch, 3 * LANES), lambda b: (b, 0)),
        out_shape=jax.ShapeDtypeStruct((batch * nch, 3 * LANES), F32),
        compiler_params=_cp(("parallel",)),
        name="nsa_compress",
    )(rows, rows, rows, *cw, _group_ones())


def _nsa_cmp_body(q_ref, ckv_ref, bias_ref, ov_ref, pq_ref, g_ref, sel_ref, part_ref, *, tq, ncp, topn):
    i = pl.program_id(1)
    q0 = i * tq
    q = q_ref[...]
    ckv = ckv_ref[...].astype(BF16)
    rows4 = NSA_R * tq
    tpos = q0 + lax.broadcasted_iota(jnp.int32, (rows4, ncp), 0) % tq
    ncol = lax.broadcasted_iota(jnp.int32, (rows4, ncp), 1)
    mask_c = (CMP_STRIDE * ncol + CMP_BLOCK - 1) <= tpos
    lane = lax.broadcasted_iota(jnp.int32, (tq, LANES), 1)
    tq_pos = q0 + lax.broadcasted_iota(jnp.int32, (tq, LANES), 0)
    valid = lane * SLC_BLOCK <= tq_pos
    curb = tq_pos // SLC_BLOCK
    forced = (lane == 0) | (lane == curb) | (lane == curb - 1)
    gates = g_ref[...]
    for g in range(NSA_G):
        qg = _stack_heads(q, pq_ref, g, tq)
        s = _dot_t(qg, ckv[:, LANES * (g // 2):LANES * (g // 2 + 1)]) * SCALE + bias_ref[0, g]
        pb = _masked_softmax(s, mask_c).astype(BF16)
        vp = (3 + g) // 2
        o = _dot(pb, ckv[:, LANES * vp:LANES * (vp + 1)])
        imp4 = _dot(pb, ov_ref[...])
        imp = imp4[0:tq] + imp4[tq:2 * tq] + imp4[2 * tq:3 * tq] + imp4[3 * tq:4 * tq]
        score = jnp.where(valid, imp + jnp.where(forced, FORCE_BONUS, 0.0), -jnp.inf)
        sel_ref[:, LANES * g:LANES * (g + 1)] = _select_topn(score, topn).astype(sel_ref.dtype)
        tiles = _place_heads(o, g, tq, lambda h: gates[:, h:h + 1], lane)
        for rp in range(2):
            k = 2 * g + rp
            part_ref[:, LANES * k:LANES * (k + 1)] = tiles[rp]


def _nsa_cmp(qn, ckv, biasc, ov, pq, gates, batch, tq):
    m = qn.shape[0]
    t = m // batch
    nt = t // tq
    ncp = t // CMP_STRIDE
    topn = min(SLC_TOPN, -(-t // SLC_BLOCK))
    return pl.pallas_call(
        functools.partial(_nsa_cmp_body, tq=tq, ncp=ncp, topn=topn),
        grid=(batch, nt),
        in_specs=[pl.BlockSpec((tq, TOK_WIDTH), lambda b, i: (b * nt + i, 0)),
                  pl.BlockSpec((ncp, 3 * LANES), lambda b, i: (b, 0)),
                  pl.BlockSpec((1, NSA_G, NSA_R * tq, ncp), lambda b, i: (i, 0, 0, 0)),
                  pl.BlockSpec((ncp, LANES), lambda b, i: (0, 0)),
                  pl.BlockSpec((TOK_HEADS, TOK_WIDTH, LANES), lambda b, i: (0, 0, 0)),
                  pl.BlockSpec((tq, LANES), lambda b, i: (b * nt + i, 0))],
        out_specs=[pl.BlockSpec((tq, NSA_G * LANES), lambda b, i: (b * nt + i, 0)),
                   pl.BlockSpec((tq, TOK_WIDTH), lambda b, i: (b * nt + i, 0))],
        out_shape=[jax.ShapeDtypeStruct((m, NSA_G * LANES), BF16),
                   jax.ShapeDtypeStruct((m, TOK_WIDTH), F32)],
        compiler_params=_cp(("parallel", "parallel")),
        name="nsa_cmp_select",
    )(qn, ckv, biasc, ov, pq, gates)


N_NEAR = 8


def _nsa_slc_body(q_ref, kvb_ref, sel_ref, part_ref, g_ref, bt_ref, pq_ref, o_ref,
                  qg_s, m_s, l_s, acc_s, *, tq):
    i = pl.program_id(1)
    rows4 = NSA_R * tq
    q = q_ref[...]
    for g in range(NSA_G):
        qg_s[g] = _stack_heads(q, pq_ref, g, tq)
    rel = (lax.broadcasted_iota(jnp.int32, (rows4, LANES), 0) % tq
           - lax.broadcasted_iota(jnp.int32, (rows4, LANES), 1))
    jrow = lax.broadcasted_iota(jnp.int32, (LANES, LANES), 0)
    khalf = lax.broadcasted_iota(jnp.int32, (LANES, LANES), 1) // SLC_BLOCK
    lane = lax.broadcasted_iota(jnp.int32, (tq, LANES), 1)
    gates = g_ref[...]
    tok = [part_ref[:, LANES * k:LANES * (k + 1)] for k in range(2 * NSA_G)]

    for branch in range(2):
        koff = 3 * LANES * branch
        for g in range(NSA_G):
            m_s[g] = jnp.full((rows4, 1), NEG, F32)
            l_s[g] = jnp.zeros((rows4, 1), F32)
            acc_s[g] = jnp.zeros((rows4, LANES), F32)

        def body(kt, carry, branch=branch, koff=koff):
            ks = pl.multiple_of(kt * LANES, LANES)
            d = i - kt
            dist = d * LANES + rel
            didx = jnp.minimum(d, N_NEAR)
            if branch == 0:
                e_mat = jnp.where(jrow == 2 * kt + khalf, 1.0, 0.0).astype(BF16)
                vis = dist >= 0
            else:
                vis = jnp.logical_and(dist >= 0, dist < WINDOW)
            for g in range(NSA_G):
                kl = LANES * ((6 + g) // 2) + koff
                vl = LANES * ((9 + g) // 2) + koff
                kblk = kvb_ref[pl.ds(ks, LANES), kl:kl + LANES]
                vblk = kvb_ref[pl.ds(ks, LANES), vl:vl + LANES]
                s = _dot_t(qg_s[g], kblk) * SCALE + bt_ref[didx, g]
                if branch == 0:
                    selx = _dot(sel_ref[:, LANES * g:LANES * (g + 1)], e_mat)
                    mask = jnp.logical_and(vis, jnp.concatenate([selx] * NSA_R, axis=0) > 0.5)
                else:
                    mask = vis
                m_old = m_s[g]
                m_new = jnp.maximum(m_old, jnp.max(jnp.where(mask, s, NEG), axis=1, keepdims=True))
                p = jnp.where(mask, jnp.exp(s - m_new), 0.0)
                alpha = jnp.exp(m_old - m_new)
                l_s[g] = alpha * l_s[g] + jnp.sum(p, axis=1, keepdims=True)
                acc_s[g] = alpha * acc_s[g] + _dot(p.astype(BF16), vblk)
                m_s[g] = m_new
            return carry

        lo = 0 if branch == 0 else jnp.maximum(i - WINDOW // LANES, 0)
        lax.fori_loop(lo, i + 1, body, 0)
        for g in range(NSA_G):
            o = acc_s[g] / jnp.maximum(l_s[g], 1e-30)
            base = TOK_HEADS * (branch + 1)
            tiles = _place_heads(o, g, tq, lambda h: gates[:, base + h:base + h + 1], lane)
            for rp in range(2):
                tok[2 * g + rp] = tok[2 * g + rp] + tiles[rp]
    for k in range(2 * NSA_G):
        o_ref[:, LANES * k:LANES * (k + 1)] = tok[k].astype(o_ref.dtype)


def _nsa_slc_win(qn, kvb, sel, part, gates, bt, pq, batch):
    tq = LANES
    m = qn.shape[0]
    t = m // batch
    nt = t // tq
    rows4 = NSA_R * tq
    return pl.pallas_call(
        functools.partial(_nsa_slc_body, tq=tq),
        grid=(batch, nt),
        in_specs=[pl.BlockSpec((tq, TOK_WIDTH), lambda b, i: (b * nt + i, 0)),
                  pl.BlockSpec((t, 9 * LANES), lambda b, i: (b, 0)),
                  pl.BlockSpec((tq, NSA_G * LANES), lambda b, i: (b * nt + i, 0)),
                  pl.BlockSpec((tq, TOK_WIDTH), lambda b, i: (b * nt + i, 0)),
                  pl.BlockSpec((tq, LANES), lambda b, i: (b * nt + i, 0)),
                  pl.BlockSpec((N_NEAR + 1, NSA_G, rows4, LANES), lambda b, i: (0, 0, 0, 0)),
                  pl.BlockSpec((TOK_HEADS, TOK_WIDTH, LANES), lambda b, i: (0, 0, 0))],
        out_specs=pl.BlockSpec((tq, TOK_WIDTH), lambda b, i: (b * nt + i, 0)),
        out_shape=jax.ShapeDtypeStruct((m, TOK_WIDTH), BF16),
        scratch_shapes=[pltpu.VMEM((NSA_G, rows4, LANES), BF16), pltpu.VMEM((NSA_G, rows4, 1), F32),
                        pltpu.VMEM((NSA_G, rows4, 1), F32), pltpu.VMEM((NSA_G, rows4, LANES), F32)],
        compiler_params=_cp(("parallel", "parallel")),
        name="nsa_slc_win",
    )(qn, kvb, sel, part, gates, bt, pq)


def _nsa_sample_body(pt_ref, q_ref, new_ref, neww_ref, g_ref, win_ref, *rest, t, n_pages):
    page_refs = rest[:n_pages]
    (wc_ref, w1_ref, pe_ref, w2p_ref, gf_ref, s_ref, pq_ref, ov_ref, e_ref, bc_ref, bs_ref, bw_ref,
     tok_ref, wout_ref, s_buf, w_buf) = rest[n_pages:]
    del pt_ref
    past = n_pages * PAGE
    slen = past + PAGE
    nch = past // CMP_STRIDE
    n_cmp = nch - 1
    nwin = win_ref.shape[0]
    wlen = nwin + PAGE
    rows4 = NSA_R * t

    ntile = s_buf.shape[0]
    for k in range(ntile):
        lanes = slice(LANES * k, LANES * (k + 1))
        for p in range(n_pages):
            s_buf[k, p * PAGE:(p + 1) * PAGE, :] = page_refs[p][:, lanes]
        s_buf[k, past:slen, :] = jnp.zeros((PAGE, LANES), F32)
        s_buf[k, past:past + t, :] = new_ref[0, :, lanes]
    w_buf[0:nwin, :] = win_ref[...]
    w_buf[nwin:wlen, :] = jnp.zeros((PAGE, w_buf.shape[1]), F32)
    w_buf[nwin:nwin + t, :] = neww_ref[0]
    wout_ref[0:nwin - t, :] = win_ref[t:nwin, :]
    wout_ref[nwin - t:nwin, :] = neww_ref[0]

    def load(p, c):
        return s_buf[c, pl.ds(p, nch, stride=CMP_STRIDE), :]
    ckv = _compress(load, nch, wc_ref, w1_ref, pe_ref, w2p_ref, gf_ref, s_ref[...]).astype(BF16)

    q = jnp.concatenate([q_ref[0], jnp.zeros((8, TOK_WIDTH), F32)], axis=0).astype(BF16)
    slc_tiles = [s_buf[k].astype(BF16) for k in (3, 4, 5)]
    win_tiles = [w_buf[:, LANES * k:LANES * (k + 1)].astype(BF16) for k in (0, 1, 2)]
    gates = g_ref[0]
    lane = lax.broadcasted_iota(jnp.int32, (t, LANES), 1)
    tpos = past + lax.broadcasted_iota(jnp.int32, (t, LANES), 0)
    valid = lane * SLC_BLOCK <= tpos
    curb = tpos // SLC_BLOCK
    forced = (lane == 0) | (lane == curb) | (lane == curb - 1)
    mask_c = lax.broadcasted_iota(jnp.int32, (rows4, nch), 1) < n_cmp
    trow_s = lax.broadcasted_iota(jnp.int32, (rows4, slen), 0) % t
    kcol = lax.broadcasted_iota(jnp.int32, (rows4, slen), 1)
    vis_s = kcol <= past + trow_s
    trow_w = lax.broadcasted_iota(jnp.int32, (rows4, wlen), 0) % t
    wcol = lax.broadcasted_iota(jnp.int32, (rows4, wlen), 1)
    vis_w = jnp.logical_and(wcol > trow_w, wcol <= nwin + trow_w)

    for g in range(NSA_G):
        qg = _stack_heads(q, pq_ref, g, t)
        s = _dot_t(qg, ckv[:, LANES * (g // 2):LANES * (g // 2 + 1)]) * SCALE + bc_ref[g]
        pb = _masked_softmax(s, mask_c).astype(BF16)
        vp = (3 + g) // 2
        o_c = _dot(pb, ckv[:, LANES * vp:LANES * (vp + 1)])
        imp4 = _dot(pb, ov_ref[...])
        imp = imp4[0:t] + imp4[t:2 * t] + imp4[2 * t:3 * t] + imp4[3 * t:4 * t]
        score = jnp.where(valid, imp + jnp.where(forced, FORCE_BONUS, 0.0), -jnp.inf)
        sel = _select_topn(score, SLC_TOPN)
        s = _dot_t(qg, slc_tiles[0 if g < 2 else 1]) * SCALE + bs_ref[g]
        selx = _dot(sel.astype(BF16), e_ref[...])
        mask = jnp.logical_and(vis_s, jnp.concatenate([selx] * NSA_R, axis=0) > 0.5)
        o_s = _dot(_masked_softmax(s, mask).astype(BF16), slc_tiles[1 if g == 0 else 2])
        s = _dot_t(qg, win_tiles[0 if g < 2 else 1]) * SCALE + bw_ref[g]
        o_w = _dot(_masked_softmax(s, vis_w).astype(BF16), win_tiles[1 if g == 0 else 2])

        def gate_col(c, g=g):
            return jnp.concatenate([gates[:, TOK_HEADS * c + NSA_R * g + r:TOK_HEADS * c + NSA_R * g + r + 1]
                                    for r in range(NSA_R)], axis=0)
        o = gate_col(0) * o_c + gate_col(1) * o_s + gate_col(2) * o_w
        tiles = _place_heads(o, g, t, lambda h: 1.0, lane)
        for rp in range(2):
            k = 2 * g + rp
            tok_ref[0, :, LANES * k:LANES * (k + 1)] = tiles[rp]


def _nsa_sample(q, new_rows, new_w, gates, win_cache, pool, layer, page_table, cw, pq, ov, e_mat, bc, bs, bw):
    batch, t, _ = q.shape
    n_pages = page_table.shape[1]
    past = n_pages * PAGE
    nwin = win_cache.shape[2]
    nch = past // CMP_STRIDE
    rows4 = NSA_R * t
    page_specs = [
        pl.BlockSpec((None, None, PAGE, TOK_WIDTH), lambda b, pt, p=p: (layer, pt[b, p], 0, 0))
        for p in range(n_pages)
    ]
    const3 = lambda b, pt: (0, 0, 0)
    const2 = lambda b, pt: (0, 0)
    grid_spec = pltpu.PrefetchScalarGridSpec(
        num_scalar_prefetch=1,
        grid=(batch,),
        in_specs=[pl.BlockSpec((1, t, TOK_WIDTH), lambda b, pt: (b, 0, 0)),
                  pl.BlockSpec((1, t, TOK_WIDTH), lambda b, pt: (b, 0, 0)),
                  pl.BlockSpec((1, t, 3 * LANES), lambda b, pt: (b, 0, 0)),
                  pl.BlockSpec((1, t, LANES), lambda b, pt: (b, 0, 0)),
                  pl.BlockSpec((None, None, nwin, 3 * LANES), lambda b, pt: (layer, b, 0, 0))]
        + page_specs + _cmp_const_specs()
        + [pl.BlockSpec((TOK_HEADS, TOK_WIDTH, LANES), const3),
           pl.BlockSpec((nch, LANES), const2),
           pl.BlockSpec((LANES, past + PAGE), const2),
           pl.BlockSpec((NSA_G, rows4, nch), const3),
           pl.BlockSpec((NSA_G, rows4, past + PAGE), const3),
           pl.BlockSpec((NSA_G, rows4, nwin + PAGE), const3)],
        out_specs=[pl.BlockSpec((1, t, TOK_WIDTH), lambda b, pt: (b, 0, 0)),
                   pl.BlockSpec((None, nwin, 3 * LANES), lambda b, pt: (b, 0, 0))],
        scratch_shapes=[pltpu.VMEM((TOK_WIDTH // LANES, past + PAGE, LANES), F32),
                        pltpu.VMEM((nwin + PAGE, 3 * LANES), F32)],
    )
    return pl.pallas_call(
        functools.partial(_nsa_sample_body, t=t, n_pages=n_pages),
        grid_spec=grid_spec,
        out_shape=[jax.ShapeDtypeStruct((batch, t, TOK_WIDTH), F32),
                   jax.ShapeDtypeStruct((batch, nwin, 3 * LANES), F32)],
        compiler_params=_cp(("arbitrary",)),
        name="nsa_sample",
    )(page_table, q, new_rows, new_w, gates, win_cache, *([pool] * n_pages), *cw, _group_ones(),
      pq, ov, e_mat, bc, bs, bw)


def _rel_bucket(dist):
    exact = N_BUCKETS // 2
    d = jnp.maximum(dist, 0)
    far = exact + (jnp.log(jnp.maximum(d, 1).astype(F32) / exact)
                   / math.log(MAX_DIST / exact) * (N_BUCKETS - exact)).astype(jnp.int32)
    return jnp.where(d < exact, d, jnp.minimum(far, N_BUCKETS - 1))


def _bias_rows(rel_bias, dist):
    t, n = dist.shape
    b = rel_bias[_rel_bucket(jnp.asarray(dist, jnp.int32))]
    return jnp.transpose(b, (2, 0, 1)).reshape(NSA_G, NSA_R * t, n).astype(F32)


def _head_placement():
    pq = np.zeros((TOK_HEADS, TOK_WIDTH, LANES), np.float32)
    d = np.arange(HEAD_DIM)
    for h in range(TOK_HEADS):
        pq[h, HEAD_DIM * h + d, HALF * ((h // NSA_R) % 2) + d] = 1.0
    return jnp.asarray(pq, BF16)


def _overlap(nrows, n_cmp, n_slc):
    cs = np.arange(nrows)[:, None] * CMP_STRIDE
    ss = np.arange(LANES)[None, :] * SLC_BLOCK
    ov = (cs < ss + SLC_BLOCK) & (cs + CMP_BLOCK > ss)
    ov &= (np.arange(nrows)[:, None] < n_cmp) & (np.arange(LANES)[None, :] < n_slc)
    return jnp.asarray(ov.astype(np.float32), BF16)


def _cmp_weights(pe, w1, w2, kn_cmp):
    w1b = w1.astype(BF16)
    w1p = w1b.reshape(2, 2, CMP_STRIDE, HEAD_DIM, CMP_HID)
    cat = jnp.concatenate([w1p[:, 0], w1p[:, 1]], axis=-1)
    z = jnp.zeros_like(cat[0])

    def pair(ta, tb):
        top = jnp.concatenate([cat[ta], z], axis=-1)
        bot = jnp.concatenate([z, cat[tb]], axis=-1)
        return jnp.concatenate([top, bot], axis=1)
    wc = jnp.stack([pair(0, 0), pair(0, 1), pair(1, 1)])
    pe8 = jnp.pad(pe.reshape(2, 1, CMP_BLOCK * HEAD_DIM), ((0, 0), (0, 7), (0, 0))).astype(BF16)
    w2b = w2.astype(BF16)
    zz = jnp.zeros_like(w2b)
    w2p = jnp.stack([jnp.concatenate([w2b, zz], axis=-1), jnp.concatenate([zz, w2b], axis=-1)], axis=1)
    kvw = NSA_G * HEAD_DIM
    gain = jnp.concatenate([jnp.tile(kn_cmp, NSA_G), jnp.ones((kvw,), F32)])
    flag = jnp.concatenate([jnp.ones((kvw,), F32), jnp.zeros((kvw,), F32)])
    return wc, w1b, pe8, w2p, jnp.stack([gain, flag])


FAR = 1 << 20


def _prompt_tables(rel_bias, t, tq):
    ncp = t // CMP_STRIDE
    nt = t // tq
    n_cmp = (t - CMP_BLOCK) // CMP_STRIDE + 1
    n_slc = -(-t // SLC_BLOCK)
    dist_c = np.arange(t)[:, None] - (np.arange(ncp)[None, :] * CMP_STRIDE + CMP_BLOCK - 1)
    bc = rel_bias[_rel_bucket(jnp.asarray(dist_c, jnp.int32))]
    bc = jnp.transpose(bc.reshape(nt, tq, ncp, TOK_HEADS), (0, 3, 1, 2)).reshape(nt, NSA_G, NSA_R * tq, ncp)
    ij = np.arange(LANES)[:, None] - np.arange(LANES)[None, :]
    bt = jnp.stack([_bias_rows(rel_bias, (ij + LANES * d) if d < N_NEAR else np.full_like(ij, FAR))
                    for d in range(N_NEAR + 1)])
    return bc.astype(F32), bt, _overlap(ncp, n_cmp, n_slc)


def _sample_tables(rel_bias, past, t, nwin):
    nch = past // CMP_STRIDE
    n_cmp = (past + t - CMP_BLOCK) // CMP_STRIDE + 1
    n_slc = -(-(past + t) // SLC_BLOCK)
    qpos = past + np.arange(t)[:, None]
    bc = _bias_rows(rel_bias, qpos - (np.arange(nch)[None, :] * CMP_STRIDE + CMP_BLOCK - 1))
    bs = _bias_rows(rel_bias, qpos - np.arange(past + PAGE)[None, :])
    bw = _bias_rows(rel_bias, qpos - (past - nwin + np.arange(nwin + PAGE)[None, :]))
    e_mat = (np.arange(LANES)[:, None] == np.arange(past + PAGE)[None, :] // SLC_BLOCK)
    return bc, bs, bw, _overlap(nch, n_cmp, n_slc), jnp.asarray(e_mat.astype(np.float32), BF16)


def _nsa_prompt_mix(qn, rows, kvb, gates, cw, tables, pq, batch):
    bc, bt, ov = tables
    ckv = _nsa_compress(rows, cw, batch)
    sel, part = _nsa_cmp(qn, ckv, bc, ov, pq, gates, batch, LANES)
    return _nsa_slc_win(qn, kvb, sel, part, gates, bt, pq, batch)


TM = 512
TH = 1408
MEM_NB = 8


def kernel(x_prompt, x_sample, mem_prompt, cache_nsa_kv, cache_nsa_win, cache_sb_kv, cache_mem_kv, page_table,
           rel_bias, norm_mix, norm_ffn, norm_mem, w_in_nsa, w_in_sb, w_mem_kv, w_out, nsa_qk_norm, x_qk_norm,
           cmp_pe, cmp_w1, cmp_w2, w_ffn_in, w_ffn_out):
    batch, seq, _ = x_prompt.shape
    dbatch, dseq, _ = x_sample.shape
    depth = norm_mix.shape[0]
    n_pages = page_table.shape[1]
    past = n_pages * PAGE
    nwin = cache_nsa_win.shape[2]
    kvw = NSA_G * HEAD_DIM

    xp = x_prompt.reshape(batch * seq, D_MODEL)
    xs = x_sample.reshape(dbatch * dseq, D_MODEL)
    mem = mem_prompt.reshape(batch * N_MEM, D_MODEL)
    nsa_pool = cache_nsa_kv.reshape(cache_nsa_kv.shape[0], -1, PAGE, 4 * kvw)
    sb_pool = cache_sb_kv.reshape(cache_sb_kv.shape[0], -1, PAGE, 2 * TOK_WIDTH)
    win_cache = cache_nsa_win.reshape(cache_nsa_win.shape[0], dbatch, nwin, 2 * kvw)
    mem_cache = cache_mem_kv.reshape(depth, dbatch, N_MEM, 2 * X_WIDTH)

    pq = _head_placement()
    ptab = _prompt_tables(rel_bias, seq, LANES)
    bc_s, bs_s, bw_s, ov_s, e_s = _sample_tables(rel_bias, past, dseq, nwin)
    wo = w_out.astype(BF16)
    wfi = w_ffn_in.astype(BF16)
    wfo = w_ffn_out.astype(BF16)

    nsa_p, nsa_s, win_p, win_s, sb_p, sb_s, mem_p = [], [], [], [], [], [], []
    for l in range(depth):
        j = l // 2
        gmix = norm_mix[l][None]
        wm, gfm = _mem_in_weights(w_mem_kv[l], x_qk_norm[l, 1])
        mkv, mkvb = _inproj(mem, norm_mem[l][None], wm, gfm, MEM_CHUNKS,
                            [(2 * X_WIDTH, F32), (2 * X_WIDTH, BF16)], TM)
        mem_p.append(mkv.reshape(batch, N_MEM, 2, X_HEADS, HEAD_DIM))
        if l % 2 == 0:
            w, gf = _nsa_in_weights(w_in_nsa[j], nsa_qk_norm[j], x_qk_norm[l, 0])
            qn, rows, rows_w, kvb, xqn, gates = _inproj(
                xp, gmix, w, gf, NSA_CHUNKS,
                [(TOK_WIDTH, BF16), (4 * kvw, F32), (2 * kvw, F32), (6 * kvw, BF16), (X_WIDTH, BF16), (LANES, F32)], TM)
            cw = _cmp_weights(cmp_pe[j], cmp_w1[j], cmp_w2[j], nsa_qk_norm[j, 1])
            tok_p = _nsa_prompt_mix(qn, rows, kvb, gates, cw, ptab, pq, batch)
            qs, rows_s, rows_ws, xqs, gates_s = _inproj(
                xs, gmix, w, gf, NSA_CHUNKS_S,
                [(TOK_WIDTH, F32), (4 * kvw, F32), (2 * kvw, F32), (X_WIDTH, F32), (LANES, F32)], TM)
            tok_s, wout = _nsa_sample(
                qs.reshape(dbatch, dseq, TOK_WIDTH), rows_s.reshape(dbatch, dseq, 4 * kvw),
                rows_ws.reshape(dbatch, dseq, 2 * kvw), gates_s.reshape(dbatch, dseq, LANES),
                win_cache, nsa_pool, j, page_table, cw, pq, ov_s, e_s, bc_s, bs_s, bw_s)
            nsa_p.append(rows.reshape(batch, seq, 4, NSA_G, HEAD_DIM))
            nsa_s.append(rows_s.reshape(dbatch, dseq, 4, NSA_G, HEAD_DIM))
            win_p.append(rows_w.reshape(batch, seq, 2, NSA_G, HEAD_DIM)[:, seq - min(WINDOW, seq):])
            win_s.append(wout.reshape(dbatch, nwin, 2, NSA_G, HEAD_DIM))
        else:
            w, gf = _sb_in_weights(w_in_sb[j], x_qk_norm[l, 0])
            q, rows, kvb, xqn = _inproj(
                xp, gmix, w, gf, SB_CHUNKS,
                [(TOK_WIDTH, BF16), (2 * TOK_WIDTH, F32), (2 * TOK_WIDTH, BF16), (X_WIDTH, BF16)], TM)
            tok_p = _sb_prompt(q, kvb, batch, LANES)
            qs, rows_s, xqs = _inproj(
                xs, gmix, w, gf, SB_CHUNKS_S, [(TOK_WIDTH, F32), (2 * TOK_WIDTH, F32), (X_WIDTH, F32)], TM)
            tok_s = _sb_sample(qs.reshape(dbatch, dseq, TOK_WIDTH), rows_s.reshape(dbatch, dseq, 2 * TOK_WIDTH),
                               sb_pool, j, page_table)
            sb_p.append(rows.reshape(batch, seq, 2, TOK_HEADS, HEAD_DIM))
            sb_s.append(rows_s.reshape(dbatch, dseq, 2, TOK_HEADS, HEAD_DIM))
        memo_p = _memattn_prompt(xqn, mkvb, batch, TM)
        memo_s = _memattn_sample(xqs, mem_cache, l, dseq, MEM_NB)
        g_ffn = norm_ffn[l][None]
        xp = _out_ffn(xp, tok_p, memo_p, wo[l, :TOK_WIDTH], wo[l, TOK_WIDTH:], g_ffn, wfi[l], wfo[l], TM, TH)
        xs = _out_ffn(xs, tok_s.reshape(dbatch * dseq, TOK_WIDTH), memo_s, wo[l, :TOK_WIDTH], wo[l, TOK_WIDTH:],
                      g_ffn, wfi[l], wfo[l], TM, TH)
    return (xp.reshape(batch, seq, D_MODEL), xs.reshape(dbatch, dseq, D_MODEL), jnp.stack(nsa_p), jnp.stack(nsa_s),
            jnp.stack(win_p), jnp.stack(win_s), jnp.stack(sb_p), jnp.stack(sb_s), jnp.stack(mem_p))
```

```python
import functools
import math

import numpy as np
import jax
import jax.numpy as jnp
from jax import lax
from jax.experimental import pallas as pl
from jax.experimental.pallas import tpu as pltpu

F32 = jnp.float32
BF16 = jnp.bfloat16

D_MODEL = 1024
HEAD_DIM = 64
TOK_HEADS = 12
X_HEADS = 4
TOK_WIDTH = TOK_HEADS * HEAD_DIM
X_WIDTH = X_HEADS * HEAD_DIM
N_MEM = 256
NSA_G = 3
NSA_R = 4
CMP_BLOCK = 32
CMP_STRIDE = 16
CMP_HID = 128
SLC_BLOCK = 64
SLC_TOPN = 8
WINDOW = 512
FORCE_BONUS = 1e4
N_BUCKETS = 32
MAX_DIST = 1024
FFN_HIDDEN = 2816
RMS_EPS = 1e-6
PAGE = 128
SCALE = HEAD_DIM ** -0.5

LANES = 128
HALF = LANES // 2
VMEM_LIMIT = 56 * 1024 * 1024

EXP_ZERO = -104.0
NEG = -1e30


def _cp(sem, vmem=VMEM_LIMIT):
    return pltpu.CompilerParams(dimension_semantics=sem, vmem_limit_bytes=vmem)


def _dot(a, b):
    return jnp.dot(a, b, preferred_element_type=F32)


def _dot_t(a, b):
    return lax.dot_general(a, b, (((1,), (1,)), ((), ())), preferred_element_type=F32)


def _split_dot(x, w):
    hi = x.astype(BF16)
    lo = (x - hi.astype(F32)).astype(BF16)
    return _dot(hi, w) + _dot(lo, w)


def _rms_rows(x, g):
    ms = jnp.mean(x * x, axis=-1, keepdims=True)
    return x * lax.rsqrt(ms + RMS_EPS) * g


def _head_scale(h, gain, flag, s128):
    x2 = h * h
    parts = []
    for k in range(h.shape[1] // LANES):
        parts.append(_split_dot(x2[:, LANES * k:LANES * (k + 1)], s128))
    ssq = parts[0] if len(parts) == 1 else jnp.concatenate(parts, axis=1)
    r = lax.rsqrt(ssq * (1.0 / HEAD_DIM) + RMS_EPS)
    return h * jnp.where(flag > 0.5, r * gain, 1.0)


def _sigmoid(x):
    return 1.0 / (1.0 + jnp.exp(-x))


def _softplus(z):
    return jnp.maximum(z, 0.0) + jnp.log1p(jnp.exp(-jnp.abs(z)))


def _group_ones():
    i = np.arange(LANES)
    return jnp.asarray((i[:, None] // HALF == i[None, :] // HALF).astype(np.float32), BF16)


def _inproj_body(x_ref, g_ref, w_ref, gf_ref, s_ref, *out_refs, chunks):
    xn = _rms_rows(x_ref[...], g_ref[...]).astype(BF16)
    for c0, c1, norm, outs in chunks:
        h = _dot(xn, w_ref[:, c0:c1])
        if norm:
            h = _head_scale(h, gf_ref[0:1, c0:c1], gf_ref[1:2, c0:c1], s_ref[...])
        for oi, o0, kind in outs:
            o_ref = out_refs[oi]
            if kind == "sigmoid":
                o_ref[:, o0:o0 + (c1 - c0)] = _sigmoid(h)
            else:
                o_ref[:, o0:o0 + (c1 - c0)] = h.astype(o_ref.dtype)


def _inproj(x, g, w, gf, chunks, out_defs, tm):
    m = x.shape[0]
    n = w.shape[1]
    assert m % tm == 0
    out_shape = [jax.ShapeDtypeStruct((m, wd), dt) for wd, dt in out_defs]
    out_specs = [pl.BlockSpec((tm, wd), lambda i: (i, 0)) for wd, _ in out_defs]
    return pl.pallas_call(
        functools.partial(_inproj_body, chunks=chunks),
        grid=(m // tm,),
        in_specs=[
            pl.BlockSpec((tm, D_MODEL), lambda i: (i, 0)),
            pl.BlockSpec((1, D_MODEL), lambda i: (0, 0)),
            pl.BlockSpec((D_MODEL, n), lambda i: (0, 0)),
            pl.BlockSpec((2, n), lambda i: (0, 0)),
            pl.BlockSpec((LANES, LANES), lambda i: (0, 0)),
        ],
        out_specs=out_specs,
        out_shape=out_shape,
        compiler_params=_cp(("parallel",)),
        name="inproj",
    )(x, g, w, gf, _group_ones())


NSA_N = 2304
NSA_CHUNKS = (
    (0, 384, True, ((0, 0, "cast"),)),
    (384, 768, True, ((0, 384, "cast"),)),
    (768, 1152, False, ((1, 0, "cast"), (3, 0, "cast"))),
    (1152, 1536, True, ((1, 384, "cast"), (3, 384, "cast"))),
    (1536, 1920, True, ((2, 0, "cast"), (3, 768, "cast"))),
    (1920, 2176, True, ((4, 0, "cast"),)),
    (2176, 2304, False, ((5, 0, "sigmoid"),)),
)
SB_N = 2560
SB_CHUNKS = (
    (0, 384, False, ((0, 0, "cast"),)),
    (384, 768, False, ((0, 384, "cast"),)),
    (768, 1152, False, ((1, 0, "cast"), (2, 0, "cast"))),
    (1152, 1536, False, ((1, 384, "cast"), (2, 384, "cast"))),
    (1536, 1920, False, ((1, 768, "cast"), (2, 768, "cast"))),
    (1920, 2304, False, ((1, 1152, "cast"), (2, 1152, "cast"))),
    (2304, 2560, True, ((3, 0, "cast"),)),
)
NSA_CHUNKS_S = (
    (0, 384, True, ((0, 0, "cast"),)),
    (384, 768, True, ((0, 384, "cast"),)),
    (768, 1152, False, ((1, 0, "cast"),)),
    (1152, 1536, True, ((1, 384, "cast"),)),
    (1536, 1920, True, ((2, 0, "cast"),)),
    (1920, 2176, True, ((3, 0, "cast"),)),
    (2176, 2304, False, ((4, 0, "sigmoid"),)),
)
SB_CHUNKS_S = (
    (0, 384, False, ((0, 0, "cast"),)),
    (384, 768, False, ((0, 384, "cast"),)),
    (768, 1152, False, ((1, 0, "cast"),)),
    (1152, 1536, False, ((1, 384, "cast"),)),
    (1536, 1920, False, ((1, 768, "cast"),)),
    (1920, 2304, False, ((1, 1152, "cast"),)),
    (2304, 2560, True, ((2, 0, "cast"),)),
)
MEM_CHUNKS = (
    (0, 256, True, ((0, 0, "cast"), (1, 0, "cast"))),
    (256, 512, False, ((0, 256, "cast"), (1, 256, "cast"))),
)


def _nsa_in_weights(w_in, qk_norm, xq_gain):
    o = TOK_WIDTH + 6 * NSA_G * HEAD_DIM
    w = jnp.concatenate([w_in[:, :o + X_WIDTH],
                         jnp.pad(w_in[:, o + X_WIDTH:], ((0, 0), (0, LANES - 3 * TOK_HEADS)))], axis=1)
    kvw = NSA_G * HEAD_DIM
    one = jnp.ones((kvw,), F32)
    zero = jnp.zeros((kvw,), F32)
    gain = jnp.concatenate([jnp.tile(qk_norm[0], TOK_HEADS), one, one, jnp.tile(qk_norm[2], NSA_G), one,
                            jnp.tile(qk_norm[3], NSA_G), one, jnp.tile(xq_gain, X_HEADS), jnp.ones((LANES,), F32)])
    flag = jnp.concatenate([jnp.ones((TOK_WIDTH,), F32), zero, zero, one, zero, one, zero,
                            jnp.ones((X_WIDTH,), F32), jnp.zeros((LANES,), F32)])
    return w.astype(BF16), jnp.stack([gain, flag])


def _sb_in_weights(w_in, xq_gain):
    gain = jnp.concatenate([jnp.ones((3 * TOK_WIDTH,), F32), jnp.tile(xq_gain, X_HEADS)])
    flag = jnp.concatenate([jnp.zeros((3 * TOK_WIDTH,), F32), jnp.ones((X_WIDTH,), F32)])
    return w_in.astype(BF16), jnp.stack([gain, flag])


def _mem_in_weights(w_kv, k_gain):
    gain = jnp.concatenate([jnp.tile(k_gain, X_HEADS), jnp.ones((X_WIDTH,), F32)])
    flag = jnp.concatenate([jnp.ones((X_WIDTH,), F32), jnp.zeros((X_WIDTH,), F32)])
    return w_kv.astype(BF16), jnp.stack([gain, flag])


def _ffn_body(x_ref, tok_ref, mem_ref, wot_ref, wom_ref, g_ref, wa_ref, wu_ref, wo_ref, o_ref, acc_ref, xn_ref):
    j = pl.program_id(1)

    @pl.when(j == 0)
    def _():
        xm = (x_ref[...] + _dot(tok_ref[...].astype(BF16), wot_ref[...])
              + _dot(mem_ref[...].astype(BF16), wom_ref[...]))
        acc_ref[...] = xm
        xn_ref[...] = _rms_rows(xm, g_ref[...]).astype(BF16)

    xn = xn_ref[...]
    a = _dot(xn, wa_ref[...])
    u = _dot(xn, wu_ref[...])
    hsw = (a * _sigmoid(a) * u).astype(BF16)
    acc_ref[...] += _dot(hsw, wo_ref[...])

    @pl.when(j == pl.num_programs(1) - 1)
    def _():
        o_ref[...] = acc_ref[...]


def _out_ffn(x, tok, mem, wo_tok, wo_mem, g, w_in, w_out, tm, th):
    m = x.shape[0]
    nh = FFN_HIDDEN // th
    return pl.pallas_call(
        _ffn_body,
        grid=(m // tm, nh),
        in_specs=[
            pl.BlockSpec((tm, D_MODEL), lambda i, j: (i, 0)),
            pl.BlockSpec((tm, TOK_WIDTH), lambda i, j: (i, 0)),
            pl.BlockSpec((tm, X_WIDTH), lambda i, j: (i, 0)),
            pl.BlockSpec((TOK_WIDTH, D_MODEL), lambda i, j: (0, 0)),
            pl.BlockSpec((X_WIDTH, D_MODEL), lambda i, j: (0, 0)),
            pl.BlockSpec((1, D_MODEL), lambda i, j: (0, 0)),
            pl.BlockSpec((D_MODEL, th), lambda i, j: (0, j)),
            pl.BlockSpec((D_MODEL, th), lambda i, j: (0, j + nh)),
            pl.BlockSpec((th, D_MODEL), lambda i, j: (j, 0)),
        ],
        out_specs=pl.BlockSpec((tm, D_MODEL), lambda i, j: (i, 0)),
        out_shape=jax.ShapeDtypeStruct((m, D_MODEL), F32),
        scratch_shapes=[pltpu.VMEM((tm, D_MODEL), F32), pltpu.VMEM((tm, D_MODEL), BF16)],
        compiler_params=_cp(("parallel", "arbitrary")),
        name="out_ffn",
    )(x, tok, mem, wo_tok, wo_mem, g, w_in, w_in, w_out)


def _mem_heads(q, kv, lane, feature_major=False):
    outs = []
    for c in range(X_WIDTH // LANES):
        q2 = q[:, LANES * c:LANES * (c + 1)]
        if feature_major:
            k2 = kv[LANES * c:LANES * (c + 1), :]
            v2 = kv[X_WIDTH + LANES * c:X_WIDTH + LANES * (c + 1), :]
        else:
            k2 = kv[:, LANES * c:LANES * (c + 1)]
            v2 = kv[:, X_WIDTH + LANES * c:X_WIDTH + LANES * (c + 1)]
        halves = []
        for half in range(2):
            sel = (lane < HALF) if half == 0 else (lane >= HALF)
            qm = jnp.where(sel, q2, jnp.zeros_like(q2))
            s = (_dot(qm, k2) if feature_major else _dot_t(qm, k2)) * SCALE
            e = jnp.exp(s - jnp.max(s, axis=-1, keepdims=True))
            p = (e / jnp.sum(e, axis=-1, keepdims=True)).astype(BF16)
            halves.append(_dot_t(p, v2) if feature_major else _dot(p, v2))
        outs.append(jnp.where(lane < HALF, halves[0], halves[1]))
    return jnp.concatenate(outs, axis=1)


def _memattn_p_body(q_ref, kv_ref, o_ref):
    lane = lax.broadcasted_iota(jnp.int32, (q_ref.shape[0], LANES), 1)
    o_ref[...] = _mem_heads(q_ref[...], kv_ref[...], lane).astype(o_ref.dtype)


def _memattn_prompt(xq, mkv, batch, tq):
    m = xq.shape[0]
    nt = m // batch // tq
    return pl.pallas_call(
        _memattn_p_body,
        grid=(batch, nt),
        in_specs=[pl.BlockSpec((tq, X_WIDTH), lambda b, i: (b * nt + i, 0)),
                  pl.BlockSpec((N_MEM, 2 * X_WIDTH), lambda b, i: (b, 0))],
        out_specs=pl.BlockSpec((tq, X_WIDTH), lambda b, i: (b * nt + i, 0)),
        out_shape=jax.ShapeDtypeStruct((m, X_WIDTH), BF16),
        compiler_params=_cp(("parallel", "parallel")),
        name="memattn_prompt",
    )(xq, mkv)


def _memattn_s_body(q_ref, kv_ref, o_ref, *, nb, t):
    lane = lax.broadcasted_iota(jnp.int32, (t, LANES), 1)
    for bi in range(nb):
        q = q_ref[bi * t:(bi + 1) * t, :].astype(BF16)
        o_ref[bi * t:(bi + 1) * t, :] = _mem_heads(q, kv_ref[bi].astype(BF16), lane, feature_major=True)


def _memattn_sample(xq, mkv, layer, t, nb):
    m = xq.shape[0]
    batch = m // t
    return pl.pallas_call(
        functools.partial(_memattn_s_body, nb=nb, t=t),
        grid=(batch // nb,),
        in_specs=[pl.BlockSpec((nb * t, X_WIDTH), lambda i: (i, 0)),
                  pl.BlockSpec((None, nb, 2 * X_WIDTH, N_MEM), lambda i: (layer, i, 0, 0))],
        out_specs=pl.BlockSpec((nb * t, X_WIDTH), lambda i: (i, 0)),
        out_shape=jax.ShapeDtypeStruct((m, X_WIDTH), F32),
        compiler_params=_cp(("parallel",)),
        name="memattn_sample",
    )(xq, mkv)


def _sb_block(z, strict, carry, u_tri):
    sp = _softplus(z)
    ls = -sp if strict is None else jnp.where(strict, -sp, 0.0)
    after = _split_dot(ls, u_tri) + carry
    a = jnp.exp(z - sp + after)
    if strict is not None:
        a = jnp.where(strict, a, 0.0)
    return a, carry + jnp.sum(ls, axis=1, keepdims=True)


def _upper_tri(n):
    r = lax.broadcasted_iota(jnp.int32, (n, n), 0)
    c = lax.broadcasted_iota(jnp.int32, (n, n), 1)
    return jnp.where(r > c, 1.0, 0.0).astype(BF16)


def _sb_prompt_body(q_ref, k_ref, v_ref, o_ref, qh_s, c_s, a_s, *, tq):
    i = pl.program_id(1)
    npair = TOK_WIDTH // LANES
    u_tri = _upper_tri(tq)
    lane = lax.broadcasted_iota(jnp.int32, (tq, LANES), 1)
    for pair in range(npair):
        q2 = q_ref[:, LANES * pair:LANES * (pair + 1)] * SCALE
        zero = jnp.zeros_like(q2)
        qh_s[2 * pair] = jnp.where(lane < HALF, q2, zero)
        qh_s[2 * pair + 1] = jnp.where(lane >= HALF, q2, zero)
    c_s[...] = jnp.zeros_like(c_s)
    a_s[...] = jnp.zeros_like(a_s)

    def key_block(kb, strict):
        ks = pl.multiple_of(kb * tq, tq)
        heads = range(TOK_HEADS)
        kblk = [k_ref[pl.ds(ks, tq), LANES * p:LANES * (p + 1)] for p in range(npair)]
        vblk = [v_ref[pl.ds(ks, tq), LANES * p:LANES * (p + 1)] for p in range(npair)]
        z = [_dot_t(qh_s[h], kblk[h // 2]) for h in heads]
        sp = [_softplus(z[h]) for h in heads]
        ls = [-sp[h] if strict is None else jnp.where(strict, -sp[h], 0.0) for h in heads]
        after = [_split_dot(ls[h], u_tri) for h in heads]
        old = [c_s[h] for h in heads]
        w = [jnp.exp(z[h] - sp[h] + after[h] + old[h]) for h in heads]
        if strict is not None:
            w = [jnp.where(strict, w[h], 0.0) for h in heads]
        pv = [_dot(w[h].astype(BF16), vblk[h // 2]) for h in heads]
        top = None
        for h in heads:
            carry = old[h] + jnp.sum(ls[h], axis=1, keepdims=True)
            c_s[h] = carry
            top = carry if top is None else jnp.maximum(top, carry)
        for p in range(npair):
            a_s[p] += jnp.where(lane < HALF, pv[2 * p], pv[2 * p + 1])
        return jnp.max(top)

    row = lax.broadcasted_iota(jnp.int32, (tq, tq), 0)
    col = lax.broadcasted_iota(jnp.int32, (tq, tq), 1)
    top0 = key_block(i, col < row)

    def cond(c):
        kb, top = c
        return jnp.logical_and(kb >= 0, top > EXP_ZERO)

    def body(c):
        kb, _ = c
        return kb - 1, key_block(kb, None)

    lax.while_loop(cond, body, (i - 1, top0))
    for pair in range(npair):
        o_ref[:, LANES * pair:LANES * (pair + 1)] = a_s[pair].astype(o_ref.dtype)


def _sb_prompt(q, kvb, batch, tq):
    assert tq == LANES
    m = q.shape[0]
    t = m // batch
    nt = t // tq
    return pl.pallas_call(
        functools.partial(_sb_prompt_body, tq=tq),
        grid=(batch, nt),
        in_specs=[pl.BlockSpec((tq, TOK_WIDTH), lambda b, i: (b * nt + i, 0)),
                  pl.BlockSpec((t, TOK_WIDTH), lambda b, i: (b, 0)),
                  pl.BlockSpec((t, TOK_WIDTH), lambda b, i: (b, 1))],
        out_specs=pl.BlockSpec((tq, TOK_WIDTH), lambda b, i: (b * nt + i, 0)),
        out_shape=jax.ShapeDtypeStruct((m, TOK_WIDTH), BF16),
        scratch_shapes=[pltpu.VMEM((TOK_HEADS, tq, LANES), BF16), pltpu.VMEM((TOK_HEADS, tq, LANES), F32),
                        pltpu.VMEM((TOK_WIDTH // LANES, tq, LANES), F32)],
        compiler_params=_cp(("parallel", "parallel")),
        name="sb_prompt",
    )(q, kvb, kvb)


def _sb_sample_body(pt_ref, q_ref, new_ref, *rest, t, n_pages):
    page_refs = rest[:n_pages]
    o_ref = rest[n_pages]
    qbd_ref, acc_ref, carry_ref, blk_ref = rest[n_pages + 1:]
    del pt_ref
    rows = TOK_HEADS * t
    lane = lax.broadcasted_iota(jnp.int32, (rows, TOK_WIDTH), 1)
    rowi = lax.broadcasted_iota(jnp.int32, (rows, TOK_WIDTH), 0)
    own = (lane // HEAD_DIM) == (rowi // t)
    q = q_ref[0] * SCALE
    qbd_ref[...] = jnp.where(own, jnp.concatenate([q] * TOK_HEADS, axis=0), 0.0).astype(BF16)
    u_tri = _upper_tri(PAGE)

    blk_ref[...] = jnp.zeros_like(blk_ref)
    blk_ref[0:t, :] = new_ref[0]
    kcol = lax.broadcasted_iota(jnp.int32, (rows, PAGE), 1)
    trow = lax.broadcasted_iota(jnp.int32, (rows, PAGE), 0) % t
    strict = kcol < trow
    kv = blk_ref[...]
    z = _dot_t(qbd_ref[...], kv[:, :TOK_WIDTH].astype(BF16))
    w, carry = _sb_block(z, strict, jnp.zeros((rows, 1), F32), u_tri)
    acc_ref[...] = _dot(w.astype(BF16), kv[:, TOK_WIDTH:].astype(BF16))
    carry_ref[...] = carry

    for p in range(n_pages - 1, -1, -1):
        @pl.when(jnp.max(carry_ref[...]) > EXP_ZERO)
        def _(p=p):
            kt = page_refs[p][0:TOK_WIDTH, :].astype(BF16)
            vt = page_refs[p][TOK_WIDTH:2 * TOK_WIDTH, :].astype(BF16)
            w, carry = _sb_block(_dot(qbd_ref[...], kt), None, carry_ref[...], u_tri)
            acc_ref[...] += _dot_t(w.astype(BF16), vt)
            carry_ref[...] = carry

    acc = jnp.where(own, acc_ref[...], 0.0)
    out = acc[0:t]
    for h in range(1, TOK_HEADS):
        out = out + acc[h * t:(h + 1) * t]
    o_ref[0] = out


def _sb_sample(q, new_rows, pool, layer, page_table):
    batch, t, _ = q.shape
    n_pages = page_table.shape[1]
    rows = TOK_HEADS * t
    page_specs = [
        pl.BlockSpec((None, None, 2 * TOK_WIDTH, PAGE), lambda b, pt, p=p: (layer, pt[b, p], 0, 0))
        for p in range(n_pages)
    ]
    grid_spec = pltpu.PrefetchScalarGridSpec(
        num_scalar_prefetch=1,
        grid=(batch,),
        in_specs=[pl.BlockSpec((1, t, TOK_WIDTH), lambda b, pt: (b, 0, 0)),
                  pl.BlockSpec((1, t, 2 * TOK_WIDTH), lambda b, pt: (b, 0, 0))] + page_specs,
        out_specs=pl.BlockSpec((1, t, TOK_WIDTH), lambda b, pt: (b, 0, 0)),
        scratch_shapes=[pltpu.VMEM((rows, TOK_WIDTH), BF16), pltpu.VMEM((rows, TOK_WIDTH), F32),
                        pltpu.VMEM((rows, 1), F32), pltpu.VMEM((PAGE, 2 * TOK_WIDTH), F32)],
    )
    return pl.pallas_call(
        functools.partial(_sb_sample_body, t=t, n_pages=n_pages),
        grid_spec=grid_spec,
        out_shape=jax.ShapeDtypeStruct((batch, t, TOK_WIDTH), F32),
        compiler_params=_cp(("arbitrary",)),
        name="sb_sample",
    )(page_table, q, new_rows, *([pool] * n_pages))


def _masked_softmax(s, mask):
    s = jnp.where(mask, s, -jnp.inf)
    m = jnp.max(s, axis=-1, keepdims=True)
    m = jnp.where(m == -jnp.inf, 0.0, m)
    e = jnp.exp(s - m)
    return e / jnp.maximum(jnp.sum(e, axis=-1, keepdims=True), 1e-30)


def _compress(load_chunks, nch, wc_ref, w1_ref, pe_ref, w2p_ref, gf_ref, s128):
    acc = [jnp.zeros((nch, 4 * CMP_HID), F32) for _ in range(3)]
    for p in range(CMP_STRIDE):
        for c in range(3):
            acc[c] = acc[c] + _dot(load_chunks(p, c).astype(BF16), wc_ref[c, p])
    bias = [_dot(pe_ref[tt], w1_ref[tt])[0:1] for tt in range(2)]
    outs = []
    for c, (ta, tb) in enumerate(((0, 0), (0, 1), (1, 1))):
        hid = []
        for half, tt in ((0, ta), (1, tb)):
            first = acc[c][:, 2 * CMP_HID * half:2 * CMP_HID * half + CMP_HID]
            second = acc[c][:, 2 * CMP_HID * half + CMP_HID:2 * CMP_HID * (half + 1)]
            pre = first + pltpu.roll(second, nch - 1, 0) + bias[tt]
            hid.append((pre * _sigmoid(pre)).astype(BF16))
        outs.append(_dot(hid[0], w2p_ref[ta, 0]) + _dot(hid[1], w2p_ref[tb, 1]))
    ckv = jnp.concatenate(outs, axis=1)
    return _head_scale(ckv, gf_ref[0:1, :], gf_ref[1:2, :], s128)


def _select_topn(score, topn):
    lane = lax.broadcasted_iota(jnp.int32, score.shape, 1)
    sel = jnp.zeros(score.shape, F32)
    cur = score
    for _ in range(topn):
        mx = jnp.max(cur, axis=1, keepdims=True)
        is_max = jnp.logical_and(cur == mx, mx > -jnp.inf)
        idx = jnp.min(jnp.where(is_max, lane, LANES), axis=1, keepdims=True)
        pick = lane == idx
        sel = jnp.where(pick, 1.0, sel)
        cur = jnp.where(pick, -jnp.inf, cur)
    return sel


def _stack_heads(q, pq_ref, g, t):
    return jnp.concatenate([_dot(q, pq_ref[NSA_R * g + r])[0:t] for r in range(NSA_R)], axis=0).astype(BF16)


def _place_heads(o, g, t, gate_of, lane):
    vh = (g + 1) % 2
    tiles = []
    for rp in range(2):
        outs = []
        for rr in range(2):
            r = 2 * rp + rr
            o_r = o[r * t:(r + 1) * t]
            if vh != rr:
                o_r = pltpu.roll(o_r, HALF, 1)
            outs.append(o_r * gate_of(NSA_R * g + r))
        tiles.append(jnp.where(lane < HALF, outs[0], outs[1]))
    return tiles


def _nsa_compress_body(r0_ref, r1_ref, r2_ref, wc_ref, w1_ref, pe_ref, w2p_ref, gf_ref, s_ref, o_ref, *, nch):
    tiles = (r0_ref, r1_ref, r2_ref)

    def load(p, c):
        return tiles[c][pl.ds(p, nch, stride=CMP_STRIDE), :]
    o_ref[...] = _compress(load, nch, wc_ref, w1_ref, pe_ref, w2p_ref, gf_ref, s_ref[...])


def _cmp_const_specs():
    return [pl.BlockSpec((3, CMP_STRIDE, LANES, 4 * CMP_HID), lambda *a: (0, 0, 0, 0)),
            pl.BlockSpec((2, CMP_BLOCK * HEAD_DIM, CMP_HID), lambda *a: (0, 0, 0)),
            pl.BlockSpec((2, 8, CMP_BLOCK * HEAD_DIM), lambda *a: (0, 0, 0)),
            pl.BlockSpec((2, 2, CMP_HID, LANES), lambda *a: (0, 0, 0, 0)),
            pl.BlockSpec((2, 3 * LANES), lambda *a: (0, 0)),
            pl.BlockSpec((LANES, LANES), lambda *a: (0, 0))]


def _nsa_compress(rows, cw, batch):
    m = rows.shape[0]
    t = m // batch
    nch = t // CMP_STRIDE
    return pl.pallas_call(
        functools.partial(_nsa_compress_body, nch=nch),
        grid=(batch,),
        in_specs=[pl.BlockSpec((t, LANES), lambda b, c=c: (b, c)) for c in range(3)] + _cmp_const_specs(),
        out_specs=pl.BlockSpec((nch, 3 * LANES), lambda b: (b, 0)),
        out_shape=jax.ShapeDtypeStruct((batch * nch, 3 * LANES), F32),
        compiler_params=_cp(("parallel",)),
        name="nsa_compress",
    )(rows, rows, rows, *cw, _group_ones())


def _nsa_cmp_body(q_ref, ckv_ref, bias_ref, ov_ref, pq_ref, g_ref, sel_ref, part_ref, *, tq, ncp, topn):
    i = pl.program_id(1)
    q0 = i * tq
    q = q_ref[...]
    ckv = ckv_ref[...].astype(BF16)
    rows4 = NSA_R * tq
    tpos = q0 + lax.broadcasted_iota(jnp.int32, (rows4, ncp), 0) % tq
    ncol = lax.broadcasted_iota(jnp.int32, (rows4, ncp), 1)
    mask_c = (CMP_STRIDE * ncol + CMP_BLOCK - 1) <= tpos
    lane = lax.broadcasted_iota(jnp.int32, (tq, LANES), 1)
    tq_pos = q0 + lax.broadcasted_iota(jnp.int32, (tq, LANES), 0)
    valid = lane * SLC_BLOCK <= tq_pos
    curb = tq_pos // SLC_BLOCK
    forced = (lane == 0) | (lane == curb) | (lane == curb - 1)
    gates = g_ref[...]
    for g in range(NSA_G):
        qg = _stack_heads(q, pq_ref, g, tq)
        s = _dot_t(qg, ckv[:, LANES * (g // 2):LANES * (g // 2 + 1)]) * SCALE + bias_ref[0, g]
        pb = _masked_softmax(s, mask_c).astype(BF16)
        vp = (3 + g) // 2
        o = _dot(pb, ckv[:, LANES * vp:LANES * (vp + 1)])
        imp4 = _dot(pb, ov_ref[...])
        imp = imp4[0:tq] + imp4[tq:2 * tq] + imp4[2 * tq:3 * tq] + imp4[3 * tq:4 * tq]
        score = jnp.where(valid, imp + jnp.where(forced, FORCE_BONUS, 0.0), -jnp.inf)
        sel_ref[:, LANES * g:LANES * (g + 1)] = _select_topn(score, topn).astype(sel_ref.dtype)
        tiles = _place_heads(o, g, tq, lambda h: gates[:, h:h + 1], lane)
        for rp in range(2):
            k = 2 * g + rp
            part_ref[:, LANES * k:LANES * (k + 1)] = tiles[rp]


def _nsa_cmp(qn, ckv, biasc, ov, pq, gates, batch, tq):
    m = qn.shape[0]
    t = m // batch
    nt = t // tq
    ncp = t // CMP_STRIDE
    topn = min(SLC_TOPN, -(-t // SLC_BLOCK))
    return pl.pallas_call(
        functools.partial(_nsa_cmp_body, tq=tq, ncp=ncp, topn=topn),
        grid=(batch, nt),
        in_specs=[pl.BlockSpec((tq, TOK_WIDTH), lambda b, i: (b * nt + i, 0)),
                  pl.BlockSpec((ncp, 3 * LANES), lambda b, i: (b, 0)),
                  pl.BlockSpec((1, NSA_G, NSA_R * tq, ncp), lambda b, i: (i, 0, 0, 0)),
                  pl.BlockSpec((ncp, LANES), lambda b, i: (0, 0)),
                  pl.BlockSpec((TOK_HEADS, TOK_WIDTH, LANES), lambda b, i: (0, 0, 0)),
                  pl.BlockSpec((tq, LANES), lambda b, i: (b * nt + i, 0))],
        out_specs=[pl.BlockSpec((tq, NSA_G * LANES), lambda b, i: (b * nt + i, 0)),
                   pl.BlockSpec((tq, TOK_WIDTH), lambda b, i: (b * nt + i, 0))],
        out_shape=[jax.ShapeDtypeStruct((m, NSA_G * LANES), BF16),
                   jax.ShapeDtypeStruct((m, TOK_WIDTH), F32)],
        compiler_params=_cp(("parallel", "parallel")),
        name="nsa_cmp_select",
    )(qn, ckv, biasc, ov, pq, gates)


N_NEAR = 8


def _nsa_slc_body(q_ref, kvb_ref, sel_ref, part_ref, g_ref, bt_ref, pq_ref, o_ref,
                  qg_s, m_s, l_s, acc_s, *, tq):
    i = pl.program_id(1)
    rows4 = NSA_R * tq
    q = q_ref[...]
    for g in range(NSA_G):
        qg_s[g] = _stack_heads(q, pq_ref, g, tq) * SCALE
    rowm = lax.broadcasted_iota(jnp.int32, (rows4, LANES), 0) % tq
    col = lax.broadcasted_iota(jnp.int32, (rows4, LANES), 1)
    causal = jnp.where(col <= rowm, 0.0, NEG)
    oldest = jnp.where(col > rowm, 0.0, NEG)
    jrow = lax.broadcasted_iota(jnp.int32, (LANES, LANES), 0)
    khalf = lax.broadcasted_iota(jnp.int32, (LANES, LANES), 1) // SLC_BLOCK
    lane = lax.broadcasted_iota(jnp.int32, (tq, LANES), 1)
    gates = g_ref[...]
    tok = [part_ref[:, LANES * k:LANES * (k + 1)] for k in range(2 * NSA_G)]

    def tile_step(kt, branch, extra):
        koff = 3 * LANES * branch
        ks = pl.multiple_of(kt * LANES, LANES)
        didx = jnp.minimum(i - kt, N_NEAR)
        groups = range(NSA_G)
        kblk = [kvb_ref[pl.ds(ks, LANES), pl.ds(LANES * ((6 + g) // 2) + koff, LANES)] for g in groups]
        vblk = [kvb_ref[pl.ds(ks, LANES), pl.ds(LANES * ((9 + g) // 2) + koff, LANES)] for g in groups]
        s = [_dot_t(qg_s[g], kblk[g]) + bt_ref[didx, g] for g in groups]
        if branch == 0:
            e_mat = jnp.where(jrow == 2 * kt + khalf, 1.0, 0.0).astype(BF16)
            selx = [_dot(sel_ref[:, LANES * g:LANES * (g + 1)], e_mat) for g in groups]
            s = [s[g] + jnp.concatenate([(selx[g] - 1.0) * (-NEG)] * NSA_R, axis=0) for g in groups]
        if extra is not None:
            s = [s[g] + extra for g in groups]
        m_old = [m_s[g] for g in groups]
        m_new = [jnp.maximum(m_old[g], jnp.max(s[g], axis=1, keepdims=True)) for g in groups]
        p = [jnp.exp(s[g] - m_new[g]) for g in groups]
        alpha = [jnp.exp(m_old[g] - m_new[g]) for g in groups]
        pv = [_dot(p[g].astype(BF16), vblk[g]) for g in groups]
        for g in groups:
            l_s[g] = alpha[g] * l_s[g] + jnp.sum(p[g], axis=1, keepdims=True)
            acc_s[g] = alpha[g] * acc_s[g] + pv[g]
            m_s[g] = m_new[g]

    for branch in range(2):
        for g in range(NSA_G):
            m_s[g] = jnp.full((rows4, LANES), NEG, F32)
            l_s[g] = jnp.zeros((rows4, LANES), F32)
            acc_s[g] = jnp.zeros((rows4, LANES), F32)

        def body(kt, carry, branch=branch):
            tile_step(kt, branch, None)
            return carry

        if branch == 0:
            lax.fori_loop(0, i, body, 0)
        else:
            nback = WINDOW // LANES

            @pl.when(i >= nback)
            def _():
                tile_step(i - nback, 1, oldest)
            lax.fori_loop(jnp.maximum(i - nback + 1, 0), i, body, 0)
        tile_step(i, branch, causal)
        for g in range(NSA_G):
            o = acc_s[g] / jnp.maximum(l_s[g], 1e-30)
            base = TOK_HEADS * (branch + 1)
            tiles = _place_heads(o, g, tq, lambda h: gates[:, base + h:base + h + 1], lane)
            for rp in range(2):
                tok[2 * g + rp] = tok[2 * g + rp] + tiles[rp]
    for k in range(2 * NSA_G):
        o_ref[:, LANES * k:LANES * (k + 1)] = tok[k].astype(o_ref.dtype)


def _nsa_slc_win(qn, kvb, sel, part, gates, bt, pq, batch):
    tq = LANES
    m = qn.shape[0]
    t = m // batch
    nt = t // tq
    rows4 = NSA_R * tq
    return pl.pallas_call(
        functools.partial(_nsa_slc_body, tq=tq),
        grid=(batch, nt),
        in_specs=[pl.BlockSpec((tq, TOK_WIDTH), lambda b, i: (b * nt + i, 0)),
                  pl.BlockSpec((t, 9 * LANES), lambda b, i: (b, 0)),
                  pl.BlockSpec((tq, NSA_G * LANES), lambda b, i: (b * nt + i, 0)),
                  pl.BlockSpec((tq, TOK_WIDTH), lambda b, i: (b * nt + i, 0)),
                  pl.BlockSpec((tq, LANES), lambda b, i: (b * nt + i, 0)),
                  pl.BlockSpec((N_NEAR + 1, NSA_G, rows4, LANES), lambda b, i: (0, 0, 0, 0)),
                  pl.BlockSpec((TOK_HEADS, TOK_WIDTH, LANES), lambda b, i: (0, 0, 0))],
        out_specs=pl.BlockSpec((tq, TOK_WIDTH), lambda b, i: (b * nt + i, 0)),
        out_shape=jax.ShapeDtypeStruct((m, TOK_WIDTH), BF16),
        scratch_shapes=[pltpu.VMEM((NSA_G, rows4, LANES), BF16), pltpu.VMEM((NSA_G, rows4, LANES), F32),
                        pltpu.VMEM((NSA_G, rows4, LANES), F32), pltpu.VMEM((NSA_G, rows4, LANES), F32)],
        compiler_params=_cp(("parallel", "parallel")),
        name="nsa_slc_win",
    )(qn, kvb, sel, part, gates, bt, pq)


def _nsa_sample_body(pt_ref, q_ref, new_ref, neww_ref, g_ref, win_ref, *rest, t, n_pages):
    page_refs = rest[:n_pages]
    (wc_ref, w1_ref, pe_ref, w2p_ref, gf_ref, s_ref, pq_ref, ov_ref, e_ref, bc_ref, bs_ref, bw_ref,
     tok_ref, wout_ref, x_buf, n_buf, nw_buf) = rest[n_pages:]
    del pt_ref
    past = n_pages * PAGE
    slen = past + PAGE
    nch = past // CMP_STRIDE
    n_cmp = nch - 1
    nwin = win_ref.shape[1]
    wlen = nwin + PAGE
    rows4 = NSA_R * t

    for c in range(3):
        for p in range(n_pages):
            x_buf[c, p * PAGE:(p + 1) * PAGE, :] = page_refs[p][LANES * c:LANES * (c + 1), :].T
    n_buf[...] = jnp.zeros_like(n_buf)
    n_buf[0:t, :] = new_ref[0]
    nw_buf[...] = jnp.zeros_like(nw_buf)
    nw_buf[0:t, :] = neww_ref[0]
    for c in range(3):
        full = jnp.concatenate([win_ref[LANES * c:LANES * (c + 1), :], nw_buf[:, LANES * c:LANES * (c + 1)].T], axis=1)
        wout_ref[LANES * c:LANES * (c + 1), :] = pltpu.roll(full, wlen - t, 1)[:, 0:nwin]

    def load(p, c):
        return x_buf[c, pl.ds(p, nch, stride=CMP_STRIDE), :]
    ckv = _compress(load, nch, wc_ref, w1_ref, pe_ref, w2p_ref, gf_ref, s_ref[...]).astype(BF16)

    q = jnp.concatenate([q_ref[0], jnp.zeros((8, TOK_WIDTH), F32)], axis=0).astype(BF16)
    new_tiles = [n_buf[:, LANES * k:LANES * (k + 1)].astype(BF16) for k in (3, 4, 5)]
    neww_tiles = [nw_buf[:, LANES * k:LANES * (k + 1)].astype(BF16) for k in (0, 1, 2)]
    win_tiles = [win_ref[LANES * k:LANES * (k + 1), :].astype(BF16) for k in (0, 1, 2)]
    tile_cache = {}

    def page_tile(k, p):
        if (k, p) not in tile_cache:
            tile_cache[(k, p)] = page_refs[p][LANES * k:LANES * (k + 1), :].astype(BF16)
        return tile_cache[(k, p)]
    gates = g_ref[0]
    lane = lax.broadcasted_iota(jnp.int32, (t, LANES), 1)
    tpos = past + lax.broadcasted_iota(jnp.int32, (t, LANES), 0)
    valid = lane * SLC_BLOCK <= tpos
    curb = tpos // SLC_BLOCK
    forced = (lane == 0) | (lane == curb) | (lane == curb - 1)
    mask_c = lax.broadcasted_iota(jnp.int32, (rows4, nch), 1) < n_cmp
    trow_s = lax.broadcasted_iota(jnp.int32, (rows4, slen), 0) % t
    kcol = lax.broadcasted_iota(jnp.int32, (rows4, slen), 1)
    vis_s = kcol <= past + trow_s
    trow_w = lax.broadcasted_iota(jnp.int32, (rows4, wlen), 0) % t
    wcol = lax.broadcasted_iota(jnp.int32, (rows4, wlen), 1)
    vis_w = jnp.logical_and(wcol > trow_w, wcol <= nwin + trow_w)

    for g in range(NSA_G):
        qg = _stack_heads(q, pq_ref, g, t)
        s = _dot_t(qg, ckv[:, LANES * (g // 2):LANES * (g // 2 + 1)]) * SCALE + bc_ref[g]
        pb = _masked_softmax(s, mask_c).astype(BF16)
        vp = (3 + g) // 2
        o_c = _dot(pb, ckv[:, LANES * vp:LANES * (vp + 1)])
        imp4 = _dot(pb, ov_ref[...])
        imp = imp4[0:t] + imp4[t:2 * t] + imp4[2 * t:3 * t] + imp4[3 * t:4 * t]
        score = jnp.where(valid, imp + jnp.where(forced, FORCE_BONUS, 0.0), -jnp.inf)
        sel = _select_topn(score, SLC_TOPN)
        ki = 0 if g < 2 else 1
        vi = 1 if g == 0 else 2
        s = jnp.concatenate([_dot(qg, page_tile(3 + ki, p)) for p in range(n_pages)]
                            + [_dot_t(qg, new_tiles[ki])], axis=1) * SCALE + bs_ref[g]
        selx = _dot(sel.astype(BF16), e_ref[...])
        mask = jnp.logical_and(vis_s, jnp.concatenate([selx] * NSA_R, axis=0) > 0.5)
        ps = _masked_softmax(s, mask).astype(BF16)
        o_s = _dot(ps[:, past:slen], new_tiles[vi])
        for p in range(n_pages):
            o_s = o_s + _dot_t(ps[:, p * PAGE:(p + 1) * PAGE], page_tile(3 + vi, p))
        s = jnp.concatenate([_dot(qg, win_tiles[ki]), _dot_t(qg, neww_tiles[ki])], axis=1) * SCALE + bw_ref[g]
        pw = _masked_softmax(s, vis_w).astype(BF16)
        o_w = _dot_t(pw[:, 0:nwin], win_tiles[vi]) + _dot(pw[:, nwin:wlen], neww_tiles[vi])

        def gate_col(c, g=g):
            return jnp.concatenate([gates[:, TOK_HEADS * c + NSA_R * g + r:TOK_HEADS * c + NSA_R * g + r + 1]
                                    for r in range(NSA_R)], axis=0)
        o = gate_col(0) * o_c + gate_col(1) * o_s + gate_col(2) * o_w
        tiles = _place_heads(o, g, t, lambda h: 1.0, lane)
        for rp in range(2):
            k = 2 * g + rp
            tok_ref[0, :, LANES * k:LANES * (k + 1)] = tiles[rp]


def _nsa_sample(q, new_rows, new_w, gates, win_cache, pool, layer, page_table, cw, pq, ov, e_mat, bc, bs, bw):
    batch, t, _ = q.shape
    n_pages = page_table.shape[1]
    past = n_pages * PAGE
    nwin = win_cache.shape[3]
    nch = past // CMP_STRIDE
    rows4 = NSA_R * t
    page_specs = [
        pl.BlockSpec((None, None, TOK_WIDTH, PAGE), lambda b, pt, p=p: (layer, pt[b, p], 0, 0))
        for p in range(n_pages)
    ]
    const3 = lambda b, pt: (0, 0, 0)
    const2 = lambda b, pt: (0, 0)
    grid_spec = pltpu.PrefetchScalarGridSpec(
        num_scalar_prefetch=1,
        grid=(batch,),
        in_specs=[pl.BlockSpec((1, t, TOK_WIDTH), lambda b, pt: (b, 0, 0)),
                  pl.BlockSpec((1, t, TOK_WIDTH), lambda b, pt: (b, 0, 0)),
                  pl.BlockSpec((1, t, 3 * LANES), lambda b, pt: (b, 0, 0)),
                  pl.BlockSpec((1, t, LANES), lambda b, pt: (b, 0, 0)),
                  pl.BlockSpec((None, None, 3 * LANES, nwin), lambda b, pt: (layer, b, 0, 0))]
        + page_specs + _cmp_const_specs()
        + [pl.BlockSpec((TOK_HEADS, TOK_WIDTH, LANES), const3),
           pl.BlockSpec((nch, LANES), const2),
           pl.BlockSpec((LANES, past + PAGE), const2),
           pl.BlockSpec((NSA_G, rows4, nch), const3),
           pl.BlockSpec((NSA_G, rows4, past + PAGE), const3),
           pl.BlockSpec((NSA_G, rows4, nwin + PAGE), const3)],
        out_specs=[pl.BlockSpec((1, t, TOK_WIDTH), lambda b, pt: (b, 0, 0)),
                   pl.BlockSpec((None, 3 * LANES, nwin), lambda b, pt: (b, 0, 0))],
        scratch_shapes=[pltpu.VMEM((3, past, LANES), F32), pltpu.VMEM((PAGE, TOK_WIDTH), F32),
                        pltpu.VMEM((PAGE, 3 * LANES), F32)],
    )
    return pl.pallas_call(
        functools.partial(_nsa_sample_body, t=t, n_pages=n_pages),
        grid_spec=grid_spec,
        out_shape=[jax.ShapeDtypeStruct((batch, t, TOK_WIDTH), F32),
                   jax.ShapeDtypeStruct((batch, 3 * LANES, nwin), F32)],
        compiler_params=_cp(("arbitrary",)),
        name="nsa_sample",
    )(page_table, q, new_rows, new_w, gates, win_cache, *([pool] * n_pages), *cw, _group_ones(),
      pq, ov, e_mat, bc, bs, bw)


def _rel_bucket(dist):
    exact = N_BUCKETS // 2
    d = jnp.maximum(dist, 0)
    far = exact + (jnp.log(jnp.maximum(d, 1).astype(F32) / exact)
                   / math.log(MAX_DIST / exact) * (N_BUCKETS - exact)).astype(jnp.int32)
    return jnp.where(d < exact, d, jnp.minimum(far, N_BUCKETS - 1))


NEG_PAD = 1024


def _dist_table(rel_bias, n):
    tab = rel_bias[_rel_bucket(jnp.arange(n, dtype=jnp.int32))].T.astype(F32)
    return jnp.pad(tab, ((0, 0), (NEG_PAD, 0)))


def _skew(v, nrows, step):
    p = v.shape[-1]
    flat = jnp.tile(v, (1,) * (v.ndim - 1) + (nrows,))[..., :nrows * (p - step)]
    return flat.reshape(v.shape[:-1] + (nrows, p - step))


def _toeplitz(tab, c0, nrows, ncols):
    lo = c0 - ncols + 1 + NEG_PAD
    hi = c0 + nrows + NEG_PAD
    u = jnp.flip(tab[:, lo:hi], axis=1)
    v = jnp.roll(u, -(nrows - 1), axis=1)
    return _skew(v, nrows, 1)[..., :ncols]


def _group_rows(m):
    return m.reshape(NSA_G, NSA_R * m.shape[1], m.shape[2])


def _head_placement():
    pq = np.zeros((TOK_HEADS, TOK_WIDTH, LANES), np.float32)
    d = np.arange(HEAD_DIM)
    for h in range(TOK_HEADS):
        pq[h, HEAD_DIM * h + d, HALF * ((h // NSA_R) % 2) + d] = 1.0
    return jnp.asarray(pq, BF16)


def _overlap(nrows, n_cmp, n_slc):
    cs = np.arange(nrows)[:, None] * CMP_STRIDE
    ss = np.arange(LANES)[None, :] * SLC_BLOCK
    ov = (cs < ss + SLC_BLOCK) & (cs + CMP_BLOCK > ss)
    ov &= (np.arange(nrows)[:, None] < n_cmp) & (np.arange(LANES)[None, :] < n_slc)
    return jnp.asarray(ov.astype(np.float32), BF16)


def _cmp_weights(pe, w1, w2, kn_cmp):
    w1b = w1.astype(BF16)
    w1p = w1b.reshape(2, 2, CMP_STRIDE, HEAD_DIM, CMP_HID)
    cat = jnp.concatenate([w1p[:, 0], w1p[:, 1]], axis=-1)
    z = jnp.zeros_like(cat[0])

    def pair(ta, tb):
        top = jnp.concatenate([cat[ta], z], axis=-1)
        bot = jnp.concatenate([z, cat[tb]], axis=-1)
        return jnp.concatenate([top, bot], axis=1)
    wc = jnp.stack([pair(0, 0), pair(0, 1), pair(1, 1)])
    pe8 = jnp.pad(pe.reshape(2, 1, CMP_BLOCK * HEAD_DIM), ((0, 0), (0, 7), (0, 0))).astype(BF16)
    w2b = w2.astype(BF16)
    zz = jnp.zeros_like(w2b)
    w2p = jnp.stack([jnp.concatenate([w2b, zz], axis=-1), jnp.concatenate([zz, w2b], axis=-1)], axis=1)
    kvw = NSA_G * HEAD_DIM
    gain = jnp.concatenate([jnp.tile(kn_cmp, NSA_G), jnp.ones((kvw,), F32)])
    flag = jnp.concatenate([jnp.ones((kvw,), F32), jnp.zeros((kvw,), F32)])
    return wc, w1b, pe8, w2p, jnp.stack([gain, flag])


def _prompt_tables(rel_bias, t, tq):
    ncp = t // CMP_STRIDE
    nt = t // tq
    n_cmp = (t - CMP_BLOCK) // CMP_STRIDE + 1
    n_slc = -(-t // SLC_BLOCK)
    tab = _dist_table(rel_bias, max(t, MAX_DIST) + LANES)
    lo = NEG_PAD - (CMP_BLOCK - 1)
    bc = _skew(tab[:, lo:lo + t + CMP_STRIDE], ncp, CMP_STRIDE)
    bc = bc.reshape(NSA_G, NSA_R, ncp, nt, tq)
    bc = jnp.transpose(bc, (3, 0, 1, 4, 2)).reshape(nt, NSA_G, NSA_R * tq, ncp)
    near = [_group_rows(_toeplitz(tab, LANES * d, LANES, LANES)) for d in range(N_NEAR)]
    far = jnp.broadcast_to(tab[:, -1].reshape(NSA_G, NSA_R, 1, 1), (NSA_G, NSA_R, LANES, LANES))
    bt = jnp.stack(near + [far.reshape(NSA_G, NSA_R * LANES, LANES)])
    return bc, bt, _overlap(ncp, n_cmp, n_slc)


def _sample_tables(rel_bias, past, t, nwin):
    nch = past // CMP_STRIDE
    n_cmp = (past + t - CMP_BLOCK) // CMP_STRIDE + 1
    n_slc = -(-(past + t) // SLC_BLOCK)
    tab = _dist_table(rel_bias, past + LANES)
    dist_c = past + np.arange(t)[:, None] - (np.arange(nch)[None, :] * CMP_STRIDE + CMP_BLOCK - 1)
    bc = _group_rows(tab[:, np.maximum(dist_c, -NEG_PAD) + NEG_PAD])
    bs = _group_rows(_toeplitz(tab, past, t, past + PAGE))
    bw = _group_rows(_toeplitz(tab, nwin, t, nwin + PAGE))
    e_mat = (np.arange(LANES)[:, None] == np.arange(past + PAGE)[None, :] // SLC_BLOCK)
    return bc, bs, bw, _overlap(nch, n_cmp, n_slc), jnp.asarray(e_mat.astype(np.float32), BF16)


def _nsa_prompt_mix(qn, rows, kvb, gates, cw, tables, pq, batch):
    bc, bt, ov = tables
    ckv = _nsa_compress(rows, cw, batch)
    sel, part = _nsa_cmp(qn, ckv, bc, ov, pq, gates, batch, LANES)
    return _nsa_slc_win(qn, kvb, sel, part, gates, bt, pq, batch)


TM = 512
TH = 1408
MEM_NB = 8


def kernel(x_prompt, x_sample, mem_prompt, cache_nsa_kv, cache_nsa_win, cache_sb_kv, cache_mem_kv, page_table,
           rel_bias, norm_mix, norm_ffn, norm_mem, w_in_nsa, w_in_sb, w_mem_kv, w_out, nsa_qk_norm, x_qk_norm,
           cmp_pe, cmp_w1, cmp_w2, w_ffn_in, w_ffn_out):
    batch, seq, _ = x_prompt.shape
    dbatch, dseq, _ = x_sample.shape
    depth = norm_mix.shape[0]
    n_pages = page_table.shape[1]
    past = n_pages * PAGE
    nwin = cache_nsa_win.shape[2]
    kvw = NSA_G * HEAD_DIM

    xp = x_prompt.reshape(batch * seq, D_MODEL)
    xs = x_sample.reshape(dbatch * dseq, D_MODEL)
    mem = mem_prompt.reshape(batch * N_MEM, D_MODEL)
    def feature_major(c):
        ct = jnp.transpose(c, (0, 1, 3, 4, 5, 2))
        return ct.reshape(c.shape[0], c.shape[1], -1, c.shape[2])
    nsa_pool = feature_major(cache_nsa_kv)
    sb_pool = feature_major(cache_sb_kv)
    win_cache = feature_major(cache_nsa_win)
    mem_cache = feature_major(cache_mem_kv)

    pq = _head_placement()
    ptab = _prompt_tables(rel_bias, seq, LANES)
    bc_s, bs_s, bw_s, ov_s, e_s = _sample_tables(rel_bias, past, dseq, nwin)
    wo = w_out.astype(BF16)
    wfi = w_ffn_in.astype(BF16)
    wfo = w_ffn_out.astype(BF16)

    nsa_p, nsa_s, win_p, win_s, sb_p, sb_s, mem_p = [], [], [], [], [], [], []
    for l in range(depth):
        j = l // 2
        gmix = norm_mix[l][None]
        wm, gfm = _mem_in_weights(w_mem_kv[l], x_qk_norm[l, 1])
        mkv, mkvb = _inproj(mem, norm_mem[l][None], wm, gfm, MEM_CHUNKS,
                            [(2 * X_WIDTH, F32), (2 * X_WIDTH, BF16)], TM)
        mem_p.append(mkv.reshape(batch, N_MEM, 2, X_HEADS, HEAD_DIM))
        if l % 2 == 0:
            w, gf = _nsa_in_weights(w_in_nsa[j], nsa_qk_norm[j], x_qk_norm[l, 0])
            qn, rows, rows_w, kvb, xqn, gates = _inproj(
                xp, gmix, w, gf, NSA_CHUNKS,
                [(TOK_WIDTH, BF16), (4 * kvw, F32), (2 * kvw, F32), (6 * kvw, BF16), (X_WIDTH, BF16), (LANES, F32)], TM)
            cw = _cmp_weights(cmp_pe[j], cmp_w1[j], cmp_w2[j], nsa_qk_norm[j, 1])
            tok_p = _nsa_prompt_mix(qn, rows, kvb, gates, cw, ptab, pq, batch)
            qs, rows_s, rows_ws, xqs, gates_s = _inproj(
                xs, gmix, w, gf, NSA_CHUNKS_S,
                [(TOK_WIDTH, F32), (4 * kvw, F32), (2 * kvw, F32), (X_WIDTH, F32), (LANES, F32)], TM)
            tok_s, wout = _nsa_sample(
                qs.reshape(dbatch, dseq, TOK_WIDTH), rows_s.reshape(dbatch, dseq, 4 * kvw),
                rows_ws.reshape(dbatch, dseq, 2 * kvw), gates_s.reshape(dbatch, dseq, LANES),
                win_cache, nsa_pool, j, page_table, cw, pq, ov_s, e_s, bc_s, bs_s, bw_s)
            nsa_p.append(rows.reshape(batch, seq, 4, NSA_G, HEAD_DIM))
            nsa_s.append(rows_s.reshape(dbatch, dseq, 4, NSA_G, HEAD_DIM))
            win_p.append(rows_w.reshape(batch, seq, 2, NSA_G, HEAD_DIM)[:, seq - min(WINDOW, seq):])
            win_s.append(jnp.transpose(wout.reshape(dbatch, 2, NSA_G, HEAD_DIM, nwin), (0, 4, 1, 2, 3)))
        else:
            w, gf = _sb_in_weights(w_in_sb[j], x_qk_norm[l, 0])
            q, rows, kvb, xqn = _inproj(
                xp, gmix, w, gf, SB_CHUNKS,
                [(TOK_WIDTH, BF16), (2 * TOK_WIDTH, F32), (2 * TOK_WIDTH, BF16), (X_WIDTH, BF16)], TM)
            tok_p = _sb_prompt(q, kvb, batch, LANES)
            qs, rows_s, xqs = _inproj(
                xs, gmix, w, gf, SB_CHUNKS_S, [(TOK_WIDTH, F32), (2 * TOK_WIDTH, F32), (X_WIDTH, F32)], TM)
            tok_s = _sb_sample(qs.reshape(dbatch, dseq, TOK_WIDTH), rows_s.reshape(dbatch, dseq, 2 * TOK_WIDTH),
                               sb_pool, j, page_table)
            sb_p.append(rows.reshape(batch, seq, 2, TOK_HEADS, HEAD_DIM))
            sb_s.append(rows_s.reshape(dbatch, dseq, 2, TOK_HEADS, HEAD_DIM))
        memo_p = _memattn_prompt(xqn, mkvb, batch, TM)
        memo_s = _memattn_sample(xqs, mem_cache, l, dseq, MEM_NB)
        g_ffn = norm_ffn[l][None]
        xp = _out_ffn(xp, tok_p, memo_p, wo[l, :TOK_WIDTH], wo[l, TOK_WIDTH:], g_ffn, wfi[l], wfo[l], TM, TH)
        xs = _out_ffn(xs, tok_s.reshape(dbatch * dseq, TOK_WIDTH), memo_s, wo[l, :TOK_WIDTH], wo[l, TOK_WIDTH:],
                      g_ffn, wfi[l], wfo[l], TM, TH)
    return (xp.reshape(batch, seq, D_MODEL), xs.reshape(dbatch, dseq, D_MODEL), jnp.stack(nsa_p), jnp.stack(nsa_s),
            jnp.stack(win_p), jnp.stack(win_s), jnp.stack(sb_p), jnp.stack(sb_s), jnp.stack(mem_p))
```

```python
import functools
import math

import numpy as np
import jax
import jax.numpy as jnp
from jax import lax
from jax.experimental import pallas as pl
from jax.experimental.pallas import tpu as pltpu

F32 = jnp.float32
BF16 = jnp.bfloat16

D_MODEL = 1024
HEAD_DIM = 64
TOK_HEADS = 12
X_HEADS = 4
TOK_WIDTH = TOK_HEADS * HEAD_DIM
X_WIDTH = X_HEADS * HEAD_DIM
N_MEM = 256
NSA_G = 3
NSA_R = 4
CMP_BLOCK = 32
CMP_STRIDE = 16
CMP_HID = 128
SLC_BLOCK = 64
SLC_TOPN = 8
WINDOW = 512
FORCE_BONUS = 1e4
N_BUCKETS = 32
MAX_DIST = 1024
FFN_HIDDEN = 2816
RMS_EPS = 1e-6
PAGE = 128
SCALE = HEAD_DIM ** -0.5

LANES = 128
HALF = LANES // 2
VMEM_LIMIT = 56 * 1024 * 1024

EXP_ZERO = -104.0
NEG = -1e30


def _cp(sem, vmem=VMEM_LIMIT):
    return pltpu.CompilerParams(dimension_semantics=sem, vmem_limit_bytes=vmem)


def _dot(a, b):
    return jnp.dot(a, b, preferred_element_type=F32)


def _dot_t(a, b):
    return lax.dot_general(a, b, (((1,), (1,)), ((), ())), preferred_element_type=F32)


def _split_dot(x, w):
    hi = x.astype(BF16)
    lo = (x - hi.astype(F32)).astype(BF16)
    return _dot(hi, w) + _dot(lo, w)


def _rms_rows(x, g):
    ms = jnp.mean(x * x, axis=-1, keepdims=True)
    return x * lax.rsqrt(ms + RMS_EPS) * g


def _head_scale(h, gain, flag, s128):
    x2 = h * h
    parts = []
    for k in range(h.shape[1] // LANES):
        parts.append(_split_dot(x2[:, LANES * k:LANES * (k + 1)], s128))
    ssq = parts[0] if len(parts) == 1 else jnp.concatenate(parts, axis=1)
    r = lax.rsqrt(ssq * (1.0 / HEAD_DIM) + RMS_EPS)
    return h * jnp.where(flag > 0.5, r * gain, 1.0)


def _sigmoid(x):
    return 1.0 / (1.0 + jnp.exp(-x))


def _softplus(z):
    return jnp.maximum(z, 0.0) + jnp.log1p(jnp.exp(-jnp.abs(z)))


def _group_ones():
    i = np.arange(LANES)
    return jnp.asarray((i[:, None] // HALF == i[None, :] // HALF).astype(np.float32), BF16)


def _inproj_body(x_ref, g_ref, w_ref, gf_ref, s_ref, *out_refs, chunks):
    xn = _rms_rows(x_ref[...], g_ref[...]).astype(BF16)
    for c0, c1, norm, outs in chunks:
        h = _dot(xn, w_ref[:, c0:c1])
        if norm:
            h = _head_scale(h, gf_ref[0:1, c0:c1], gf_ref[1:2, c0:c1], s_ref[...])
        for oi, o0, kind in outs:
            o_ref = out_refs[oi]
            if kind == "sigmoid":
                o_ref[:, o0:o0 + (c1 - c0)] = _sigmoid(h)
            else:
                o_ref[:, o0:o0 + (c1 - c0)] = h.astype(o_ref.dtype)


def _inproj(x, g, w, gf, chunks, out_defs, tm):
    m = x.shape[0]
    n = w.shape[1]
    assert m % tm == 0
    out_shape = [jax.ShapeDtypeStruct((m, wd), dt) for wd, dt in out_defs]
    out_specs = [pl.BlockSpec((tm, wd), lambda i: (i, 0)) for wd, _ in out_defs]
    return pl.pallas_call(
        functools.partial(_inproj_body, chunks=chunks),
        grid=(m // tm,),
        in_specs=[
            pl.BlockSpec((tm, D_MODEL), lambda i: (i, 0)),
            pl.BlockSpec((1, D_MODEL), lambda i: (0, 0)),
            pl.BlockSpec((D_MODEL, n), lambda i: (0, 0)),
            pl.BlockSpec((2, n), lambda i: (0, 0)),
            pl.BlockSpec((LANES, LANES), lambda i: (0, 0)),
        ],
        out_specs=out_specs,
        out_shape=out_shape,
        compiler_params=_cp(("parallel",)),
        name="inproj",
    )(x, g, w, gf, _group_ones())


NSA_N = 2304
NSA_CHUNKS = (
    (0, 384, True, ((0, 0, "cast"),)),
    (384, 768, True, ((0, 384, "cast"),)),
    (768, 1152, False, ((1, 0, "cast"), (3, 0, "cast"))),
    (1152, 1536, True, ((1, 384, "cast"), (3, 384, "cast"))),
    (1536, 1920, True, ((2, 0, "cast"), (3, 768, "cast"))),
    (1920, 2176, True, ((4, 0, "cast"),)),
    (2176, 2304, False, ((5, 0, "sigmoid"),)),
)
SB_N = 2560
SB_CHUNKS = (
    (0, 384, False, ((0, 0, "cast"),)),
    (384, 768, False, ((0, 384, "cast"),)),
    (768, 1152, False, ((1, 0, "cast"), (2, 0, "cast"))),
    (1152, 1536, False, ((1, 384, "cast"), (2, 384, "cast"))),
    (1536, 1920, False, ((1, 768, "cast"), (2, 768, "cast"))),
    (1920, 2304, False, ((1, 1152, "cast"), (2, 1152, "cast"))),
    (2304, 2560, True, ((3, 0, "cast"),)),
)
NSA_CHUNKS_S = (
    (0, 384, True, ((0, 0, "cast"),)),
    (384, 768, True, ((0, 384, "cast"),)),
    (768, 1152, False, ((1, 0, "cast"),)),
    (1152, 1536, True, ((1, 384, "cast"),)),
    (1536, 1920, True, ((2, 0, "cast"),)),
    (1920, 2176, True, ((3, 0, "cast"),)),
    (2176, 2304, False, ((4, 0, "sigmoid"),)),
)
SB_CHUNKS_S = (
    (0, 384, False, ((0, 0, "cast"),)),
    (384, 768, False, ((0, 384, "cast"),)),
    (768, 1152, False, ((1, 0, "cast"),)),
    (1152, 1536, False, ((1, 384, "cast"),)),
    (1536, 1920, False, ((1, 768, "cast"),)),
    (1920, 2304, False, ((1, 1152, "cast"),)),
    (2304, 2560, True, ((2, 0, "cast"),)),
)
MEM_CHUNKS = (
    (0, 256, True, ((0, 0, "cast"), (1, 0, "cast"))),
    (256, 512, False, ((0, 256, "cast"), (1, 256, "cast"))),
)


def _nsa_in_weights(w_in, qk_norm, xq_gain):
    o = TOK_WIDTH + 6 * NSA_G * HEAD_DIM
    w = jnp.concatenate([w_in[:, :o + X_WIDTH],
                         jnp.pad(w_in[:, o + X_WIDTH:], ((0, 0), (0, LANES - 3 * TOK_HEADS)))], axis=1)
    kvw = NSA_G * HEAD_DIM
    one = jnp.ones((kvw,), F32)
    zero = jnp.zeros((kvw,), F32)
    gain = jnp.concatenate([jnp.tile(qk_norm[0], TOK_HEADS), one, one, jnp.tile(qk_norm[2], NSA_G), one,
                            jnp.tile(qk_norm[3], NSA_G), one, jnp.tile(xq_gain, X_HEADS), jnp.ones((LANES,), F32)])
    flag = jnp.concatenate([jnp.ones((TOK_WIDTH,), F32), zero, zero, one, zero, one, zero,
                            jnp.ones((X_WIDTH,), F32), jnp.zeros((LANES,), F32)])
    return w.astype(BF16), jnp.stack([gain, flag])


def _sb_in_weights(w_in, xq_gain):
    gain = jnp.concatenate([jnp.ones((3 * TOK_WIDTH,), F32), jnp.tile(xq_gain, X_HEADS)])
    flag = jnp.concatenate([jnp.zeros((3 * TOK_WIDTH,), F32), jnp.ones((X_WIDTH,), F32)])
    return w_in.astype(BF16), jnp.stack([gain, flag])


def _mem_in_weights(w_kv, k_gain):
    gain = jnp.concatenate([jnp.tile(k_gain, X_HEADS), jnp.ones((X_WIDTH,), F32)])
    flag = jnp.concatenate([jnp.ones((X_WIDTH,), F32), jnp.zeros((X_WIDTH,), F32)])
    return w_kv.astype(BF16), jnp.stack([gain, flag])


def _ffn_body(x_ref, tok_ref, mem_ref, wot_ref, wom_ref, g_ref, wa_ref, wu_ref, wo_ref, o_ref, acc_ref, xn_ref):
    j = pl.program_id(1)

    @pl.when(j == 0)
    def _():
        xm = (x_ref[...] + _dot(tok_ref[...].astype(BF16), wot_ref[...])
              + _dot(mem_ref[...].astype(BF16), wom_ref[...]))
        acc_ref[...] = xm
        xn_ref[...] = _rms_rows(xm, g_ref[...]).astype(BF16)

    xn = xn_ref[...]
    a = _dot(xn, wa_ref[...])
    u = _dot(xn, wu_ref[...])
    hsw = (a * _sigmoid(a) * u).astype(BF16)
    acc_ref[...] += _dot(hsw, wo_ref[...])

    @pl.when(j == pl.num_programs(1) - 1)
    def _():
        o_ref[...] = acc_ref[...]


def _out_ffn(x, tok, mem, wo_tok, wo_mem, g, w_in, w_out, tm, th):
    m = x.shape[0]
    nh = FFN_HIDDEN // th
    return pl.pallas_call(
        _ffn_body,
        grid=(m // tm, nh),
        in_specs=[
            pl.BlockSpec((tm, D_MODEL), lambda i, j: (i, 0)),
            pl.BlockSpec((tm, TOK_WIDTH), lambda i, j: (i, 0)),
            pl.BlockSpec((tm, X_WIDTH), lambda i, j: (i, 0)),
            pl.BlockSpec((TOK_WIDTH, D_MODEL), lambda i, j: (0, 0)),
            pl.BlockSpec((X_WIDTH, D_MODEL), lambda i, j: (0, 0)),
            pl.BlockSpec((1, D_MODEL), lambda i, j: (0, 0)),
            pl.BlockSpec((D_MODEL, th), lambda i, j: (0, j)),
            pl.BlockSpec((D_MODEL, th), lambda i, j: (0, j + nh)),
            pl.BlockSpec((th, D_MODEL), lambda i, j: (j, 0)),
        ],
        out_specs=pl.BlockSpec((tm, D_MODEL), lambda i, j: (i, 0)),
        out_shape=jax.ShapeDtypeStruct((m, D_MODEL), F32),
        scratch_shapes=[pltpu.VMEM((tm, D_MODEL), F32), pltpu.VMEM((tm, D_MODEL), BF16)],
        compiler_params=_cp(("parallel", "arbitrary")),
        name="out_ffn",
    )(x, tok, mem, wo_tok, wo_mem, g, w_in, w_in, w_out)


def _mem_heads(q, kv, lane, feature_major=False):
    outs = []
    for c in range(X_WIDTH // LANES):
        q2 = q[:, LANES * c:LANES * (c + 1)]
        if feature_major:
            k2 = kv[LANES * c:LANES * (c + 1), :]
            v2 = kv[X_WIDTH + LANES * c:X_WIDTH + LANES * (c + 1), :]
        else:
            k2 = kv[:, LANES * c:LANES * (c + 1)]
            v2 = kv[:, X_WIDTH + LANES * c:X_WIDTH + LANES * (c + 1)]
        halves = []
        for half in range(2):
            sel = (lane < HALF) if half == 0 else (lane >= HALF)
            qm = jnp.where(sel, q2, jnp.zeros_like(q2))
            s = (_dot(qm, k2) if feature_major else _dot_t(qm, k2)) * SCALE
            e = jnp.exp(s - jnp.max(s, axis=-1, keepdims=True))
            p = (e / jnp.sum(e, axis=-1, keepdims=True)).astype(BF16)
            halves.append(_dot_t(p, v2) if feature_major else _dot(p, v2))
        outs.append(jnp.where(lane < HALF, halves[0], halves[1]))
    return jnp.concatenate(outs, axis=1)


def _memattn_p_body(q_ref, kv_ref, o_ref):
    lane = lax.broadcasted_iota(jnp.int32, (q_ref.shape[0], LANES), 1)
    o_ref[...] = _mem_heads(q_ref[...], kv_ref[...], lane).astype(o_ref.dtype)


def _memattn_prompt(xq, mkv, batch, tq):
    m = xq.shape[0]
    nt = m // batch // tq
    return pl.pallas_call(
        _memattn_p_body,
        grid=(batch, nt),
        in_specs=[pl.BlockSpec((tq, X_WIDTH), lambda b, i: (b * nt + i, 0)),
                  pl.BlockSpec((N_MEM, 2 * X_WIDTH), lambda b, i: (b, 0))],
        out_specs=pl.BlockSpec((tq, X_WIDTH), lambda b, i: (b * nt + i, 0)),
        out_shape=jax.ShapeDtypeStruct((m, X_WIDTH), BF16),
        compiler_params=_cp(("parallel", "parallel")),
        name="memattn_prompt",
    )(xq, mkv)


def _memattn_s_body(q_ref, kv_ref, o_ref, *, nb, t):
    lane = lax.broadcasted_iota(jnp.int32, (t, LANES), 1)
    for bi in range(nb):
        q = q_ref[bi * t:(bi + 1) * t, :].astype(BF16)
        o_ref[bi * t:(bi + 1) * t, :] = _mem_heads(q, kv_ref[bi].astype(BF16), lane, feature_major=True)


def _memattn_sample(xq, mkv, layer, t, nb):
    m = xq.shape[0]
    batch = m // t
    return pl.pallas_call(
        functools.partial(_memattn_s_body, nb=nb, t=t),
        grid=(batch // nb,),
        in_specs=[pl.BlockSpec((nb * t, X_WIDTH), lambda i: (i, 0)),
                  pl.BlockSpec((None, nb, 2 * X_WIDTH, N_MEM), lambda i: (layer, i, 0, 0))],
        out_specs=pl.BlockSpec((nb * t, X_WIDTH), lambda i: (i, 0)),
        out_shape=jax.ShapeDtypeStruct((m, X_WIDTH), F32),
        compiler_params=_cp(("parallel",)),
        name="memattn_sample",
    )(xq, mkv)


def _sb_block(z, strict, carry, u_tri):
    sp = _softplus(z)
    ls = -sp if strict is None else jnp.where(strict, -sp, 0.0)
    after = _split_dot(ls, u_tri) + carry
    a = jnp.exp(z - sp + after)
    if strict is not None:
        a = jnp.where(strict, a, 0.0)
    return a, carry + jnp.sum(ls, axis=1, keepdims=True)


def _upper_tri(n):
    r = lax.broadcasted_iota(jnp.int32, (n, n), 0)
    c = lax.broadcasted_iota(jnp.int32, (n, n), 1)
    return jnp.where(r > c, 1.0, 0.0).astype(BF16)


def _sb_prompt_body(q_ref, k_ref, v_ref, o_ref, qh_s, c_s, a_s, *, tq):
    i = pl.program_id(1)
    npair = TOK_WIDTH // LANES
    u_tri = _upper_tri(tq)
    lane = lax.broadcasted_iota(jnp.int32, (tq, LANES), 1)
    for pair in range(npair):
        q2 = q_ref[:, LANES * pair:LANES * (pair + 1)] * SCALE
        zero = jnp.zeros_like(q2)
        qh_s[2 * pair] = jnp.where(lane < HALF, q2, zero)
        qh_s[2 * pair + 1] = jnp.where(lane >= HALF, q2, zero)
    c_s[...] = jnp.zeros_like(c_s)
    a_s[...] = jnp.zeros_like(a_s)

    def key_block(kb, strict):
        ks = pl.multiple_of(kb * tq, tq)
        heads = range(TOK_HEADS)
        kblk = [k_ref[pl.ds(ks, tq), LANES * p:LANES * (p + 1)] for p in range(npair)]
        vblk = [v_ref[pl.ds(ks, tq), LANES * p:LANES * (p + 1)] for p in range(npair)]
        z = [_dot_t(qh_s[h], kblk[h // 2]) for h in heads]
        sp = [_softplus(z[h]) for h in heads]
        ls = [-sp[h] if strict is None else jnp.where(strict, -sp[h], 0.0) for h in heads]
        after = [_split_dot(ls[h], u_tri) for h in heads]
        old = [c_s[h] for h in heads]
        w = [jnp.exp(z[h] - sp[h] + after[h] + old[h]) for h in heads]
        if strict is not None:
            w = [jnp.where(strict, w[h], 0.0) for h in heads]
        pv = [_dot(w[h].astype(BF16), vblk[h // 2]) for h in heads]
        top = None
        for h in heads:
            carry = old[h] + jnp.sum(ls[h], axis=1, keepdims=True)
            c_s[h] = carry
            top = carry if top is None else jnp.maximum(top, carry)
        for p in range(npair):
            a_s[p] += jnp.where(lane < HALF, pv[2 * p], pv[2 * p + 1])
        return jnp.max(top)

    row = lax.broadcasted_iota(jnp.int32, (tq, tq), 0)
    col = lax.broadcasted_iota(jnp.int32, (tq, tq), 1)
    top0 = key_block(i, col < row)

    def cond(c):
        kb, top = c
        return jnp.logical_and(kb >= 0, top > EXP_ZERO)

    def body(c):
        kb, _ = c
        return kb - 1, key_block(kb, None)

    lax.while_loop(cond, body, (i - 1, top0))
    for pair in range(npair):
        o_ref[:, LANES * pair:LANES * (pair + 1)] = a_s[pair].astype(o_ref.dtype)


def _sb_prompt(q, kvb, batch, tq):
    assert tq == LANES
    m = q.shape[0]
    t = m // batch
    nt = t // tq
    return pl.pallas_call(
        functools.partial(_sb_prompt_body, tq=tq),
        grid=(batch, nt),
        in_specs=[pl.BlockSpec((tq, TOK_WIDTH), lambda b, i: (b * nt + i, 0)),
                  pl.BlockSpec((t, TOK_WIDTH), lambda b, i: (b, 0)),
                  pl.BlockSpec((t, TOK_WIDTH), lambda b, i: (b, 1))],
        out_specs=pl.BlockSpec((tq, TOK_WIDTH), lambda b, i: (b * nt + i, 0)),
        out_shape=jax.ShapeDtypeStruct((m, TOK_WIDTH), BF16),
        scratch_shapes=[pltpu.VMEM((TOK_HEADS, tq, LANES), BF16), pltpu.VMEM((TOK_HEADS, tq, LANES), F32),
                        pltpu.VMEM((TOK_WIDTH // LANES, tq, LANES), F32)],
        compiler_params=_cp(("parallel", "parallel")),
        name="sb_prompt",
    )(q, kvb, kvb)


def _sb_sample_body(pt_ref, q_ref, new_ref, *rest, t, n_pages):
    page_refs = rest[:n_pages]
    o_ref = rest[n_pages]
    qbd_ref, acc_ref, carry_ref, blk_ref = rest[n_pages + 1:]
    del pt_ref
    rows = TOK_HEADS * t
    lane = lax.broadcasted_iota(jnp.int32, (rows, TOK_WIDTH), 1)
    rowi = lax.broadcasted_iota(jnp.int32, (rows, TOK_WIDTH), 0)
    own = (lane // HEAD_DIM) == (rowi // t)
    q = q_ref[0] * SCALE
    qbd_ref[...] = jnp.where(own, jnp.concatenate([q] * TOK_HEADS, axis=0), 0.0).astype(BF16)
    u_tri = _upper_tri(PAGE)

    blk_ref[...] = jnp.zeros_like(blk_ref)
    blk_ref[0:t, :] = new_ref[0]
    kcol = lax.broadcasted_iota(jnp.int32, (rows, PAGE), 1)
    trow = lax.broadcasted_iota(jnp.int32, (rows, PAGE), 0) % t
    strict = kcol < trow
    kv = blk_ref[...]
    z = _dot_t(qbd_ref[...], kv[:, :TOK_WIDTH].astype(BF16))
    w, carry = _sb_block(z, strict, jnp.zeros((rows, 1), F32), u_tri)
    acc_ref[...] = _dot(w.astype(BF16), kv[:, TOK_WIDTH:].astype(BF16))
    carry_ref[...] = carry

    for p in range(n_pages - 1, -1, -1):
        @pl.when(jnp.max(carry_ref[...]) > EXP_ZERO)
        def _(p=p):
            kt = page_refs[p][0:TOK_WIDTH, :].astype(BF16)
            vt = page_refs[p][TOK_WIDTH:2 * TOK_WIDTH, :].astype(BF16)
            w, carry = _sb_block(_dot(qbd_ref[...], kt), None, carry_ref[...], u_tri)
            acc_ref[...] += _dot_t(w.astype(BF16), vt)
            carry_ref[...] = carry

    acc = jnp.where(own, acc_ref[...], 0.0)
    out = acc[0:t]
    for h in range(1, TOK_HEADS):
        out = out + acc[h * t:(h + 1) * t]
    o_ref[0] = out


def _sb_sample(q, new_rows, pool, layer, page_table):
    batch, t, _ = q.shape
    n_pages = page_table.shape[1]
    rows = TOK_HEADS * t
    page_specs = [
        pl.BlockSpec((None, None, 2 * TOK_WIDTH, PAGE), lambda b, pt, p=p: (layer, pt[b, p], 0, 0))
        for p in range(n_pages)
    ]
    grid_spec = pltpu.PrefetchScalarGridSpec(
        num_scalar_prefetch=1,
        grid=(batch,),
        in_specs=[pl.BlockSpec((1, t, TOK_WIDTH), lambda b, pt: (b, 0, 0)),
                  pl.BlockSpec((1, t, 2 * TOK_WIDTH), lambda b, pt: (b, 0, 0))] + page_specs,
        out_specs=pl.BlockSpec((1, t, TOK_WIDTH), lambda b, pt: (b, 0, 0)),
        scratch_shapes=[pltpu.VMEM((rows, TOK_WIDTH), BF16), pltpu.VMEM((rows, TOK_WIDTH), F32),
                        pltpu.VMEM((rows, 1), F32), pltpu.VMEM((PAGE, 2 * TOK_WIDTH), F32)],
    )
    return pl.pallas_call(
        functools.partial(_sb_sample_body, t=t, n_pages=n_pages),
        grid_spec=grid_spec,
        out_shape=jax.ShapeDtypeStruct((batch, t, TOK_WIDTH), F32),
        compiler_params=_cp(("arbitrary",)),
        name="sb_sample",
    )(page_table, q, new_rows, *([pool] * n_pages))


def _masked_softmax(s, mask):
    s = jnp.where(mask, s, -jnp.inf)
    m = jnp.max(s, axis=-1, keepdims=True)
    m = jnp.where(m == -jnp.inf, 0.0, m)
    e = jnp.exp(s - m)
    return e / jnp.maximum(jnp.sum(e, axis=-1, keepdims=True), 1e-30)


def _compress(load_chunks, nch, wc_ref, w1_ref, pe_ref, w2p_ref, gf_ref, s128):
    acc = [jnp.zeros((nch, 4 * CMP_HID), F32) for _ in range(3)]
    for p in range(CMP_STRIDE):
        for c in range(3):
            acc[c] = acc[c] + _dot(load_chunks(p, c).astype(BF16), wc_ref[c, p])
    bias = [_dot(pe_ref[tt], w1_ref[tt])[0:1] for tt in range(2)]
    outs = []
    for c, (ta, tb) in enumerate(((0, 0), (0, 1), (1, 1))):
        hid = []
        for half, tt in ((0, ta), (1, tb)):
            first = acc[c][:, 2 * CMP_HID * half:2 * CMP_HID * half + CMP_HID]
            second = acc[c][:, 2 * CMP_HID * half + CMP_HID:2 * CMP_HID * (half + 1)]
            pre = first + pltpu.roll(second, nch - 1, 0) + bias[tt]
            hid.append((pre * _sigmoid(pre)).astype(BF16))
        outs.append(_dot(hid[0], w2p_ref[ta, 0]) + _dot(hid[1], w2p_ref[tb, 1]))
    ckv = jnp.concatenate(outs, axis=1)
    return _head_scale(ckv, gf_ref[0:1, :], gf_ref[1:2, :], s128)


def _select_topn(score, topn):
    lane = lax.broadcasted_iota(jnp.int32, score.shape, 1)
    sel = jnp.zeros(score.shape, F32)
    cur = score
    for _ in range(topn):
        mx = jnp.max(cur, axis=1, keepdims=True)
        is_max = jnp.logical_and(cur == mx, mx > -jnp.inf)
        idx = jnp.min(jnp.where(is_max, lane, LANES), axis=1, keepdims=True)
        pick = lane == idx
        sel = jnp.where(pick, 1.0, sel)
        cur = jnp.where(pick, -jnp.inf, cur)
    return sel


def _stack_heads(q, pq_ref, g, t):
    return jnp.concatenate([_dot(q, pq_ref[NSA_R * g + r])[0:t] for r in range(NSA_R)], axis=0).astype(BF16)


def _place_heads(o, g, t, gate_of, lane):
    vh = (g + 1) % 2
    tiles = []
    for rp in range(2):
        outs = []
        for rr in range(2):
            r = 2 * rp + rr
            o_r = o[r * t:(r + 1) * t]
            if vh != rr:
                o_r = pltpu.roll(o_r, HALF, 1)
            outs.append(o_r * gate_of(NSA_R * g + r))
        tiles.append(jnp.where(lane < HALF, outs[0], outs[1]))
    return tiles


def _nsa_compress_body(r0_ref, r1_ref, r2_ref, wc_ref, w1_ref, pe_ref, w2p_ref, gf_ref, s_ref, o_ref, *, nch):
    tiles = (r0_ref, r1_ref, r2_ref)

    def load(p, c):
        return tiles[c][pl.ds(p, nch, stride=CMP_STRIDE), :]
    o_ref[...] = _compress(load, nch, wc_ref, w1_ref, pe_ref, w2p_ref, gf_ref, s_ref[...])


def _cmp_const_specs():
    return [pl.BlockSpec((3, CMP_STRIDE, LANES, 4 * CMP_HID), lambda *a: (0, 0, 0, 0)),
            pl.BlockSpec((2, CMP_BLOCK * HEAD_DIM, CMP_HID), lambda *a: (0, 0, 0)),
            pl.BlockSpec((2, 8, CMP_BLOCK * HEAD_DIM), lambda *a: (0, 0, 0)),
            pl.BlockSpec((2, 2, CMP_HID, LANES), lambda *a: (0, 0, 0, 0)),
            pl.BlockSpec((2, 3 * LANES), lambda *a: (0, 0)),
            pl.BlockSpec((LANES, LANES), lambda *a: (0, 0))]


def _nsa_compress(rows, cw, batch):
    m = rows.shape[0]
    t = m // batch
    nch = t // CMP_STRIDE
    return pl.pallas_call(
        functools.partial(_nsa_compress_body, nch=nch),
        grid=(batch,),
        in_specs=[pl.BlockSpec((t, LANES), lambda b, c=c: (b, c)) for c in range(3)] + _cmp_const_specs(),
        out_specs=pl.BlockSpec((nch, 3 * LANES), lambda b: (b, 0)),
        out_shape=jax.ShapeDtypeStruct((batch * nch, 3 * LANES), F32),
        compiler_params=_cp(("parallel",)),
        name="nsa_compress",
    )(rows, rows, rows, *cw, _group_ones())


def _nsa_cmp_body(q_ref, ckv_ref, bias_ref, ov_ref, pq_ref, g_ref, sel_ref, part_ref, *, tq, ncp, topn):
    i = pl.program_id(1)
    q0 = i * tq
    q = q_ref[...]
    ckv = ckv_ref[...].astype(BF16)
    rows4 = NSA_R * tq
    tpos = q0 + lax.broadcasted_iota(jnp.int32, (rows4, ncp), 0) % tq
    ncol = lax.broadcasted_iota(jnp.int32, (rows4, ncp), 1)
    mask_c = (CMP_STRIDE * ncol + CMP_BLOCK - 1) <= tpos
    lane = lax.broadcasted_iota(jnp.int32, (tq, LANES), 1)
    lane3 = lax.broadcasted_iota(jnp.int32, (NSA_G * tq, LANES), 1)
    tq_pos = q0 + lax.broadcasted_iota(jnp.int32, (NSA_G * tq, LANES), 0) % tq
    valid = lane3 * SLC_BLOCK <= tq_pos
    curb = tq_pos // SLC_BLOCK
    forced = (lane3 == 0) | (lane3 == curb) | (lane3 == curb - 1)
    gates = g_ref[...]
    groups = range(NSA_G)
    qg = [_stack_heads(q, pq_ref, g, tq) * SCALE for g in groups]
    s = [_dot_t(qg[g], ckv[:, LANES * (g // 2):LANES * (g // 2 + 1)]) + bias_ref[0, g] for g in groups]
    pb = [_masked_softmax(s[g], mask_c).astype(BF16) for g in groups]
    o = [_dot(pb[g], ckv[:, LANES * ((3 + g) // 2):LANES * ((3 + g) // 2 + 1)]) for g in groups]
    imp4 = [_dot(pb[g], ov_ref[...]) for g in groups]
    imp = jnp.concatenate([imp4[g][0:tq] + imp4[g][tq:2 * tq] + imp4[g][2 * tq:3 * tq] + imp4[g][3 * tq:4 * tq]
                           for g in groups], axis=0)
    score = jnp.where(valid, imp + jnp.where(forced, FORCE_BONUS, 0.0), -jnp.inf)
    sel = _select_topn(score, topn).astype(sel_ref.dtype)
    for g in groups:
        sel_ref[:, LANES * g:LANES * (g + 1)] = sel[g * tq:(g + 1) * tq]
        tiles = _place_heads(o[g], g, tq, lambda h: gates[:, h:h + 1], lane)
        for rp in range(2):
            k = 2 * g + rp
            part_ref[:, LANES * k:LANES * (k + 1)] = tiles[rp]


def _nsa_cmp(qn, ckv, biasc, ov, pq, gates, batch, tq):
    m = qn.shape[0]
    t = m // batch
    nt = t // tq
    ncp = t // CMP_STRIDE
    topn = min(SLC_TOPN, -(-t // SLC_BLOCK))
    return pl.pallas_call(
        functools.partial(_nsa_cmp_body, tq=tq, ncp=ncp, topn=topn),
        grid=(batch, nt),
        in_specs=[pl.BlockSpec((tq, TOK_WIDTH), lambda b, i: (b * nt + i, 0)),
                  pl.BlockSpec((ncp, 3 * LANES), lambda b, i: (b, 0)),
                  pl.BlockSpec((1, NSA_G, NSA_R * tq, ncp), lambda b, i: (i, 0, 0, 0)),
                  pl.BlockSpec((ncp, LANES), lambda b, i: (0, 0)),
                  pl.BlockSpec((TOK_HEADS, TOK_WIDTH, LANES), lambda b, i: (0, 0, 0)),
                  pl.BlockSpec((tq, LANES), lambda b, i: (b * nt + i, 0))],
        out_specs=[pl.BlockSpec((tq, NSA_G * LANES), lambda b, i: (b * nt + i, 0)),
                   pl.BlockSpec((tq, TOK_WIDTH), lambda b, i: (b * nt + i, 0))],
        out_shape=[jax.ShapeDtypeStruct((m, NSA_G * LANES), BF16),
                   jax.ShapeDtypeStruct((m, TOK_WIDTH), F32)],
        compiler_params=_cp(("parallel", "parallel")),
        name="nsa_cmp_select",
    )(qn, ckv, biasc, ov, pq, gates)


N_NEAR = 8


def _nsa_slc_body(q_ref, kvb_ref, sel_ref, part_ref, g_ref, bt_ref, pq_ref, o_ref,
                  qg_s, m_s, l_s, acc_s, *, tq):
    i = pl.program_id(1)
    rows4 = NSA_R * tq
    q = q_ref[...]
    for g in range(NSA_G):
        qg_s[g] = _stack_heads(q, pq_ref, g, tq) * SCALE
    rowm = lax.broadcasted_iota(jnp.int32, (rows4, LANES), 0) % tq
    col = lax.broadcasted_iota(jnp.int32, (rows4, LANES), 1)
    causal = jnp.where(col <= rowm, 0.0, NEG)
    oldest = jnp.where(col > rowm, 0.0, NEG)
    jrow = lax.broadcasted_iota(jnp.int32, (LANES, LANES), 0)
    khalf = lax.broadcasted_iota(jnp.int32, (LANES, LANES), 1) // SLC_BLOCK
    lane = lax.broadcasted_iota(jnp.int32, (tq, LANES), 1)
    gates = g_ref[...]
    tok = [part_ref[:, LANES * k:LANES * (k + 1)] for k in range(2 * NSA_G)]

    def tile_step(kt, branch, extra):
        koff = 3 * LANES * branch
        ks = pl.multiple_of(kt * LANES, LANES)
        didx = jnp.minimum(i - kt, N_NEAR)
        groups = range(NSA_G)
        kblk = [kvb_ref[pl.ds(ks, LANES), pl.ds(LANES * ((6 + g) // 2) + koff, LANES)] for g in groups]
        vblk = [kvb_ref[pl.ds(ks, LANES), pl.ds(LANES * ((9 + g) // 2) + koff, LANES)] for g in groups]
        s = [_dot_t(qg_s[g], kblk[g]) + bt_ref[didx, g] for g in groups]
        if branch == 0:
            e_mat = jnp.where(jrow == 2 * kt + khalf, 1.0, 0.0).astype(BF16)
            selx = [_dot(sel_ref[:, LANES * g:LANES * (g + 1)], e_mat) for g in groups]
            s = [s[g] + jnp.concatenate([(selx[g] - 1.0) * (-NEG)] * NSA_R, axis=0) for g in groups]
        if extra is not None:
            s = [s[g] + extra for g in groups]
        m_old = [m_s[g] for g in groups]
        m_new = [jnp.maximum(m_old[g], jnp.max(s[g], axis=1, keepdims=True)) for g in groups]
        p = [jnp.exp(s[g] - m_new[g]) for g in groups]
        alpha = [jnp.exp(m_old[g] - m_new[g]) for g in groups]
        pv = [_dot(p[g].astype(BF16), vblk[g]) for g in groups]
        for g in groups:
            l_s[g] = alpha[g] * l_s[g] + jnp.sum(p[g], axis=1, keepdims=True)
            acc_s[g] = alpha[g] * acc_s[g] + pv[g]
            m_s[g] = m_new[g]

    for branch in range(2):
        for g in range(NSA_G):
            m_s[g] = jnp.full((rows4, LANES), NEG, F32)
            l_s[g] = jnp.zeros((rows4, LANES), F32)
            acc_s[g] = jnp.zeros((rows4, LANES), F32)

        def body(kt, carry, branch=branch):
            tile_step(kt, branch, None)
            return carry

        if branch == 0:
            lax.fori_loop(0, i, body, 0)
        else:
            nback = WINDOW // LANES

            @pl.when(i >= nback)
            def _():
                tile_step(i - nback, 1, oldest)
            lax.fori_loop(jnp.maximum(i - nback + 1, 0), i, body, 0)
        tile_step(i, branch, causal)
        for g in range(NSA_G):
            o = acc_s[g] / jnp.maximum(l_s[g], 1e-30)
            base = TOK_HEADS * (branch + 1)
            tiles = _place_heads(o, g, tq, lambda h: gates[:, base + h:base + h + 1], lane)
            for rp in range(2):
                tok[2 * g + rp] = tok[2 * g + rp] + tiles[rp]
    for k in range(2 * NSA_G):
        o_ref[:, LANES * k:LANES * (k + 1)] = tok[k].astype(o_ref.dtype)


def _nsa_slc_win(qn, kvb, sel, part, gates, bt, pq, batch):
    tq = LANES
    m = qn.shape[0]
    t = m // batch
    nt = t // tq
    rows4 = NSA_R * tq
    return pl.pallas_call(
        functools.partial(_nsa_slc_body, tq=tq),
        grid=(batch, nt),
        in_specs=[pl.BlockSpec((tq, TOK_WIDTH), lambda b, i: (b * nt + i, 0)),
                  pl.BlockSpec((t, 9 * LANES), lambda b, i: (b, 0)),
                  pl.BlockSpec((tq, NSA_G * LANES), lambda b, i: (b * nt + i, 0)),
                  pl.BlockSpec((tq, TOK_WIDTH), lambda b, i: (b * nt + i, 0)),
                  pl.BlockSpec((tq, LANES), lambda b, i: (b * nt + i, 0)),
                  pl.BlockSpec((N_NEAR + 1, NSA_G, rows4, LANES), lambda b, i: (0, 0, 0, 0)),
                  pl.BlockSpec((TOK_HEADS, TOK_WIDTH, LANES), lambda b, i: (0, 0, 0))],
        out_specs=pl.BlockSpec((tq, TOK_WIDTH), lambda b, i: (b * nt + i, 0)),
        out_shape=jax.ShapeDtypeStruct((m, TOK_WIDTH), BF16),
        scratch_shapes=[pltpu.VMEM((NSA_G, rows4, LANES), BF16), pltpu.VMEM((NSA_G, rows4, LANES), F32),
                        pltpu.VMEM((NSA_G, rows4, LANES), F32), pltpu.VMEM((NSA_G, rows4, LANES), F32)],
        compiler_params=_cp(("parallel", "parallel")),
        name="nsa_slc_win",
    )(qn, kvb, sel, part, gates, bt, pq)


def _nsa_sample_body(pt_ref, q_ref, new_ref, neww_ref, g_ref, win_ref, *rest, t, n_pages):
    page_refs = rest[:n_pages]
    (wc_ref, w1_ref, pe_ref, w2p_ref, gf_ref, s_ref, pq_ref, ov_ref, e_ref, bc_ref, bs_ref, bw_ref,
     tok_ref, wout_ref, x_buf, n_buf, nw_buf) = rest[n_pages:]
    del pt_ref
    past = n_pages * PAGE
    slen = past + PAGE
    nch = past // CMP_STRIDE
    n_cmp = nch - 1
    nwin = win_ref.shape[1]
    wlen = nwin + PAGE
    rows4 = NSA_R * t

    for c in range(3):
        for p in range(n_pages):
            x_buf[c, p * PAGE:(p + 1) * PAGE, :] = page_refs[p][LANES * c:LANES * (c + 1), :].T
    n_buf[...] = jnp.zeros_like(n_buf)
    n_buf[0:t, :] = new_ref[0]
    nw_buf[...] = jnp.zeros_like(nw_buf)
    nw_buf[0:t, :] = neww_ref[0]
    for c in range(3):
        full = jnp.concatenate([win_ref[LANES * c:LANES * (c + 1), :], nw_buf[:, LANES * c:LANES * (c + 1)].T], axis=1)
        wout_ref[LANES * c:LANES * (c + 1), :] = pltpu.roll(full, wlen - t, 1)[:, 0:nwin]

    def load(p, c):
        return x_buf[c, pl.ds(p, nch, stride=CMP_STRIDE), :]
    ckv = _compress(load, nch, wc_ref, w1_ref, pe_ref, w2p_ref, gf_ref, s_ref[...]).astype(BF16)

    q = jnp.concatenate([q_ref[0], jnp.zeros((8, TOK_WIDTH), F32)], axis=0).astype(BF16)
    new_tiles = [n_buf[:, LANES * k:LANES * (k + 1)].astype(BF16) for k in (3, 4, 5)]
    neww_tiles = [nw_buf[:, LANES * k:LANES * (k + 1)].astype(BF16) for k in (0, 1, 2)]
    win_tiles = [win_ref[LANES * k:LANES * (k + 1), :].astype(BF16) for k in (0, 1, 2)]
    tile_cache = {}

    def page_tile(k, p):
        if (k, p) not in tile_cache:
            tile_cache[(k, p)] = page_refs[p][LANES * k:LANES * (k + 1), :].astype(BF16)
        return tile_cache[(k, p)]
    gates = g_ref[0]
    lane = lax.broadcasted_iota(jnp.int32, (t, LANES), 1)
    lane3 = lax.broadcasted_iota(jnp.int32, (NSA_G * t, LANES), 1)
    tpos = past + lax.broadcasted_iota(jnp.int32, (NSA_G * t, LANES), 0) % t
    valid = lane3 * SLC_BLOCK <= tpos
    curb = tpos // SLC_BLOCK
    forced = (lane3 == 0) | (lane3 == curb) | (lane3 == curb - 1)
    mask_c = lax.broadcasted_iota(jnp.int32, (rows4, nch), 1) < n_cmp
    trow_s = lax.broadcasted_iota(jnp.int32, (rows4, slen), 0) % t
    kcol = lax.broadcasted_iota(jnp.int32, (rows4, slen), 1)
    vis_s = kcol <= past + trow_s
    trow_w = lax.broadcasted_iota(jnp.int32, (rows4, wlen), 0) % t
    wcol = lax.broadcasted_iota(jnp.int32, (rows4, wlen), 1)
    vis_w = jnp.logical_and(wcol > trow_w, wcol <= nwin + trow_w)

    groups = range(NSA_G)
    ki = (0, 0, 1)
    vi = (1, 2, 2)
    qg = [_stack_heads(q, pq_ref, g, t) * SCALE for g in groups]
    sc = [_dot_t(qg[g], ckv[:, LANES * (g // 2):LANES * (g // 2 + 1)]) + bc_ref[g] for g in groups]
    pc = [_masked_softmax(sc[g], mask_c).astype(BF16) for g in groups]
    o_c = [_dot(pc[g], ckv[:, LANES * ((3 + g) // 2):LANES * ((3 + g) // 2 + 1)]) for g in groups]
    imp4 = [_dot(pc[g], ov_ref[...]) for g in groups]
    imp = jnp.concatenate([imp4[g][0:t] + imp4[g][t:2 * t] + imp4[g][2 * t:3 * t] + imp4[g][3 * t:4 * t]
                           for g in groups], axis=0)
    score = jnp.where(valid, imp + jnp.where(forced, FORCE_BONUS, 0.0), -jnp.inf)
    sel = _select_topn(score, SLC_TOPN).astype(BF16)
    ss = [jnp.concatenate([_dot(qg[g], page_tile(3 + ki[g], p)) for p in range(n_pages)]
                          + [_dot_t(qg[g], new_tiles[ki[g]])], axis=1) + bs_ref[g] for g in groups]
    selx = _dot(sel, e_ref[...])
    ps = [_masked_softmax(ss[g], jnp.logical_and(
        vis_s, jnp.concatenate([selx[g * t:(g + 1) * t]] * NSA_R, axis=0) > 0.5)).astype(BF16) for g in groups]
    o_s = [_dot(ps[g][:, past:slen], new_tiles[vi[g]]) for g in groups]
    for p in range(n_pages):
        o_s = [o_s[g] + _dot_t(ps[g][:, p * PAGE:(p + 1) * PAGE], page_tile(3 + vi[g], p)) for g in groups]
    sw = [jnp.concatenate([_dot(qg[g], win_tiles[ki[g]]), _dot_t(qg[g], neww_tiles[ki[g]])], axis=1) + bw_ref[g]
          for g in groups]
    pw = [_masked_softmax(sw[g], vis_w).astype(BF16) for g in groups]
    o_w = [_dot_t(pw[g][:, 0:nwin], win_tiles[vi[g]]) + _dot(pw[g][:, nwin:wlen], neww_tiles[vi[g]]) for g in groups]

    def gate_col(c, g):
        return jnp.concatenate([gates[:, TOK_HEADS * c + NSA_R * g + r:TOK_HEADS * c + NSA_R * g + r + 1]
                                for r in range(NSA_R)], axis=0)
    for g in groups:
        o = gate_col(0, g) * o_c[g] + gate_col(1, g) * o_s[g] + gate_col(2, g) * o_w[g]
        tiles = _place_heads(o, g, t, lambda h: 1.0, lane)
        for rp in range(2):
            k = 2 * g + rp
            tok_ref[0, :, LANES * k:LANES * (k + 1)] = tiles[rp]


def _nsa_sample(q, new_rows, new_w, gates, win_cache, pool, layer, page_table, cw, pq, ov, e_mat, bc, bs, bw):
    batch, t, _ = q.shape
    n_pages = page_table.shape[1]
    past = n_pages * PAGE
    nwin = win_cache.shape[3]
    nch = past // CMP_STRIDE
    rows4 = NSA_R * t
    page_specs = [
        pl.BlockSpec((None, None, TOK_WIDTH, PAGE), lambda b, pt, p=p: (layer, pt[b, p], 0, 0))
        for p in range(n_pages)
    ]
    const3 = lambda b, pt: (0, 0, 0)
    const2 = lambda b, pt: (0, 0)
    grid_spec = pltpu.PrefetchScalarGridSpec(
        num_scalar_prefetch=1,
        grid=(batch,),
        in_specs=[pl.BlockSpec((1, t, TOK_WIDTH), lambda b, pt: (b, 0, 0)),
                  pl.BlockSpec((1, t, TOK_WIDTH), lambda b, pt: (b, 0, 0)),
                  pl.BlockSpec((1, t, 3 * LANES), lambda b, pt: (b, 0, 0)),
                  pl.BlockSpec((1, t, LANES), lambda b, pt: (b, 0, 0)),
                  pl.BlockSpec((None, None, 3 * LANES, nwin), lambda b, pt: (layer, b, 0, 0))]
        + page_specs + _cmp_const_specs()
        + [pl.BlockSpec((TOK_HEADS, TOK_WIDTH, LANES), const3),
           pl.BlockSpec((nch, LANES), const2),
           pl.BlockSpec((LANES, past + PAGE), const2),
           pl.BlockSpec((NSA_G, rows4, nch), const3),
           pl.BlockSpec((NSA_G, rows4, past + PAGE), const3),
           pl.BlockSpec((NSA_G, rows4, nwin + PAGE), const3)],
        out_specs=[pl.BlockSpec((1, t, TOK_WIDTH), lambda b, pt: (b, 0, 0)),
                   pl.BlockSpec((None, 3 * LANES, nwin), lambda b, pt: (b, 0, 0))],
        scratch_shapes=[pltpu.VMEM((3, past, LANES), F32), pltpu.VMEM((PAGE, TOK_WIDTH), F32),
                        pltpu.VMEM((PAGE, 3 * LANES), F32)],
    )
    return pl.pallas_call(
        functools.partial(_nsa_sample_body, t=t, n_pages=n_pages),
        grid_spec=grid_spec,
        out_shape=[jax.ShapeDtypeStruct((batch, t, TOK_WIDTH), F32),
                   jax.ShapeDtypeStruct((batch, 3 * LANES, nwin), F32)],
        compiler_params=_cp(("arbitrary",)),
        name="nsa_sample",
    )(page_table, q, new_rows, new_w, gates, win_cache, *([pool] * n_pages), *cw, _group_ones(),
      pq, ov, e_mat, bc, bs, bw)


def _rel_bucket(dist):
    exact = N_BUCKETS // 2
    d = jnp.maximum(dist, 0)
    far = exact + (jnp.log(jnp.maximum(d, 1).astype(F32) / exact)
                   / math.log(MAX_DIST / exact) * (N_BUCKETS - exact)).astype(jnp.int32)
    return jnp.where(d < exact, d, jnp.minimum(far, N_BUCKETS - 1))


NEG_PAD = 1024


def _dist_table(rel_bias, n):
    tab = rel_bias[_rel_bucket(jnp.arange(n, dtype=jnp.int32))].T.astype(F32)
    return jnp.pad(tab, ((0, 0), (NEG_PAD, 0)))


def _skew(v, nrows, step):
    p = v.shape[-1]
    flat = jnp.tile(v, (1,) * (v.ndim - 1) + (nrows,))[..., :nrows * (p - step)]
    return flat.reshape(v.shape[:-1] + (nrows, p - step))


def _toeplitz(tab, c0, nrows, ncols):
    lo = c0 - ncols + 1 + NEG_PAD
    hi = c0 + nrows + NEG_PAD
    u = jnp.flip(tab[:, lo:hi], axis=1)
    v = jnp.roll(u, -(nrows - 1), axis=1)
    return _skew(v, nrows, 1)[..., :ncols]


def _group_rows(m):
    return m.reshape(NSA_G, NSA_R * m.shape[1], m.shape[2])


def _head_placement():
    pq = np.zeros((TOK_HEADS, TOK_WIDTH, LANES), np.float32)
    d = np.arange(HEAD_DIM)
    for h in range(TOK_HEADS):
        pq[h, HEAD_DIM * h + d, HALF * ((h // NSA_R) % 2) + d] = 1.0
    return jnp.asarray(pq, BF16)


def _overlap(nrows, n_cmp, n_slc):
    cs = np.arange(nrows)[:, None] * CMP_STRIDE
    ss = np.arange(LANES)[None, :] * SLC_BLOCK
    ov = (cs < ss + SLC_BLOCK) & (cs + CMP_BLOCK > ss)
    ov &= (np.arange(nrows)[:, None] < n_cmp) & (np.arange(LANES)[None, :] < n_slc)
    return jnp.asarray(ov.astype(np.float32), BF16)


def _cmp_weights(pe, w1, w2, kn_cmp):
    w1b = w1.astype(BF16)
    w1p = w1b.reshape(2, 2, CMP_STRIDE, HEAD_DIM, CMP_HID)
    cat = jnp.concatenate([w1p[:, 0], w1p[:, 1]], axis=-1)
    z = jnp.zeros_like(cat[0])

    def pair(ta, tb):
        top = jnp.concatenate([cat[ta], z], axis=-1)
        bot = jnp.concatenate([z, cat[tb]], axis=-1)
        return jnp.concatenate([top, bot], axis=1)
    wc = jnp.stack([pair(0, 0), pair(0, 1), pair(1, 1)])
    pe8 = jnp.pad(pe.reshape(2, 1, CMP_BLOCK * HEAD_DIM), ((0, 0), (0, 7), (0, 0))).astype(BF16)
    w2b = w2.astype(BF16)
    zz = jnp.zeros_like(w2b)
    w2p = jnp.stack([jnp.concatenate([w2b, zz], axis=-1), jnp.concatenate([zz, w2b], axis=-1)], axis=1)
    kvw = NSA_G * HEAD_DIM
    gain = jnp.concatenate([jnp.tile(kn_cmp, NSA_G), jnp.ones((kvw,), F32)])
    flag = jnp.concatenate([jnp.ones((kvw,), F32), jnp.zeros((kvw,), F32)])
    return wc, w1b, pe8, w2p, jnp.stack([gain, flag])


def _prompt_tables(rel_bias, t, tq):
    ncp = t // CMP_STRIDE
    nt = t // tq
    n_cmp = (t - CMP_BLOCK) // CMP_STRIDE + 1
    n_slc = -(-t // SLC_BLOCK)
    tab = _dist_table(rel_bias, max(t, MAX_DIST) + LANES)
    per = LANES // CMP_STRIDE
    lo = [NEG_PAD - CMP_STRIDE * b - (CMP_BLOCK - 1) for b in range(per)]
    base = jnp.stack([tab[:, x:x + t + LANES] for x in lo], axis=1)
    bc = _skew(base, ncp // per, LANES)
    bc = bc.reshape(NSA_G, NSA_R, per, ncp // per, nt, tq)
    bc = jnp.transpose(bc, (4, 0, 1, 5, 3, 2)).reshape(nt, NSA_G, NSA_R * tq, ncp)
    near = [_group_rows(_toeplitz(tab, LANES * d, LANES, LANES)) for d in range(N_NEAR)]
    far = jnp.broadcast_to(tab[:, -1].reshape(NSA_G, NSA_R, 1, 1), (NSA_G, NSA_R, LANES, LANES))
    bt = jnp.stack(near + [far.reshape(NSA_G, NSA_R * LANES, LANES)])
    return bc, bt, _overlap(ncp, n_cmp, n_slc)


def _sample_tables(rel_bias, past, t, nwin):
    nch = past // CMP_STRIDE
    n_cmp = (past + t - CMP_BLOCK) // CMP_STRIDE + 1
    n_slc = -(-(past + t) // SLC_BLOCK)
    tab = _dist_table(rel_bias, past + LANES)
    dist_c = past + np.arange(t)[:, None] - (np.arange(nch)[None, :] * CMP_STRIDE + CMP_BLOCK - 1)
    bc = _group_rows(tab[:, np.maximum(dist_c, -NEG_PAD) + NEG_PAD])
    bs = _group_rows(_toeplitz(tab, past, t, past + PAGE))
    bw = _group_rows(_toeplitz(tab, nwin, t, nwin + PAGE))
    e_mat = (np.arange(LANES)[:, None] == np.arange(past + PAGE)[None, :] // SLC_BLOCK)
    return bc, bs, bw, _overlap(nch, n_cmp, n_slc), jnp.asarray(e_mat.astype(np.float32), BF16)


def _nsa_prompt_mix(qn, rows, kvb, gates, cw, tables, pq, batch):
    bc, bt, ov = tables
    ckv = _nsa_compress(rows, cw, batch)
    sel, part = _nsa_cmp(qn, ckv, bc, ov, pq, gates, batch, LANES)
    return _nsa_slc_win(qn, kvb, sel, part, gates, bt, pq, batch)


TM = 512
TH = 1408
MEM_NB = 8


def kernel(x_prompt, x_sample, mem_prompt, cache_nsa_kv, cache_nsa_win, cache_sb_kv, cache_mem_kv, page_table,
           rel_bias, norm_mix, norm_ffn, norm_mem, w_in_nsa, w_in_sb, w_mem_kv, w_out, nsa_qk_norm, x_qk_norm,
           cmp_pe, cmp_w1, cmp_w2, w_ffn_in, w_ffn_out):
    batch, seq, _ = x_prompt.shape
    dbatch, dseq, _ = x_sample.shape
    depth = norm_mix.shape[0]
    n_pages = page_table.shape[1]
    past = n_pages * PAGE
    nwin = cache_nsa_win.shape[2]
    kvw = NSA_G * HEAD_DIM

    xp = x_prompt.reshape(batch * seq, D_MODEL)
    xs = x_sample.reshape(dbatch * dseq, D_MODEL)
    mem = mem_prompt.reshape(batch * N_MEM, D_MODEL)
    def feature_major(c):
        ct = jnp.transpose(c, (0, 1, 3, 4, 5, 2))
        return ct.reshape(c.shape[0], c.shape[1], -1, c.shape[2])
    nsa_pool = feature_major(cache_nsa_kv)
    sb_pool = feature_major(cache_sb_kv)
    win_cache = feature_major(cache_nsa_win)
    mem_cache = feature_major(cache_mem_kv)

    pq = _head_placement()
    ptab = _prompt_tables(rel_bias, seq, LANES)
    bc_s, bs_s, bw_s, ov_s, e_s = _sample_tables(rel_bias, past, dseq, nwin)
    wo = w_out.astype(BF16)
    wfi = w_ffn_in.astype(BF16)
    wfo = w_ffn_out.astype(BF16)

    nsa_p, nsa_s, win_p, win_s, sb_p, sb_s, mem_p = [], [], [], [], [], [], []
    for l in range(depth):
        j = l // 2
        gmix = norm_mix[l][None]
        wm, gfm = _mem_in_weights(w_mem_kv[l], x_qk_norm[l, 1])
        mkv, mkvb = _inproj(mem, norm_mem[l][None], wm, gfm, MEM_CHUNKS,
                            [(2 * X_WIDTH, F32), (2 * X_WIDTH, BF16)], TM)
        mem_p.append(mkv.reshape(batch, N_MEM, 2, X_HEADS, HEAD_DIM))
        if l % 2 == 0:
            w, gf = _nsa_in_weights(w_in_nsa[j], nsa_qk_norm[j], x_qk_norm[l, 0])
            qn, rows, rows_w, kvb, xqn, gates = _inproj(
                xp, gmix, w, gf, NSA_CHUNKS,
                [(TOK_WIDTH, BF16), (4 * kvw, F32), (2 * kvw, F32), (6 * kvw, BF16), (X_WIDTH, BF16), (LANES, F32)], TM)
            cw = _cmp_weights(cmp_pe[j], cmp_w1[j], cmp_w2[j], nsa_qk_norm[j, 1])
            tok_p = _nsa_prompt_mix(qn, rows, kvb, gates, cw, ptab, pq, batch)
            qs, rows_s, rows_ws, xqs, gates_s = _inproj(
                xs, gmix, w, gf, NSA_CHUNKS_S,
                [(TOK_WIDTH, F32), (4 * kvw, F32), (2 * kvw, F32), (X_WIDTH, F32), (LANES, F32)], TM)
            tok_s, wout = _nsa_sample(
                qs.reshape(dbatch, dseq, TOK_WIDTH), rows_s.reshape(dbatch, dseq, 4 * kvw),
                rows_ws.reshape(dbatch, dseq, 2 * kvw), gates_s.reshape(dbatch, dseq, LANES),
                win_cache, nsa_pool, j, page_table, cw, pq, ov_s, e_s, bc_s, bs_s, bw_s)
            nsa_p.append(rows.reshape(batch, seq, 4, NSA_G, HEAD_DIM))
            nsa_s.append(rows_s.reshape(dbatch, dseq, 4, NSA_G, HEAD_DIM))
            win_p.append(rows_w.reshape(batch, seq, 2, NSA_G, HEAD_DIM)[:, seq - min(WINDOW, seq):])
            win_s.append(jnp.transpose(wout.reshape(dbatch, 2, NSA_G, HEAD_DIM, nwin), (0, 4, 1, 2, 3)))
        else:
            w, gf = _sb_in_weights(w_in_sb[j], x_qk_norm[l, 0])
            q, rows, kvb, xqn = _inproj(
                xp, gmix, w, gf, SB_CHUNKS,
                [(TOK_WIDTH, BF16), (2 * TOK_WIDTH, F32), (2 * TOK_WIDTH, BF16), (X_WIDTH, BF16)], TM)
            tok_p = _sb_prompt(q, kvb, batch, LANES)
            qs, rows_s, xqs = _inproj(
                xs, gmix, w, gf, SB_CHUNKS_S, [(TOK_WIDTH, F32), (2 * TOK_WIDTH, F32), (X_WIDTH, F32)], TM)
            tok_s = _sb_sample(qs.reshape(dbatch, dseq, TOK_WIDTH), rows_s.reshape(dbatch, dseq, 2 * TOK_WIDTH),
                               sb_pool, j, page_table)
            sb_p.append(rows.reshape(batch, seq, 2, TOK_HEADS, HEAD_DIM))
            sb_s.append(rows_s.reshape(dbatch, dseq, 2, TOK_HEADS, HEAD_DIM))
        memo_p = _memattn_prompt(xqn, mkvb, batch, TM)
        memo_s = _memattn_sample(xqs, mem_cache, l, dseq, MEM_NB)
        g_ffn = norm_ffn[l][None]
        xp = _out_ffn(xp, tok_p, memo_p, wo[l, :TOK_WIDTH], wo[l, TOK_WIDTH:], g_ffn, wfi[l], wfo[l], TM, TH)
        xs = _out_ffn(xs, tok_s.reshape(dbatch * dseq, TOK_WIDTH), memo_s, wo[l, :TOK_WIDTH], wo[l, TOK_WIDTH:],
                      g_ffn, wfi[l], wfo[l], TM, TH)
    return (xp.reshape(batch, seq, D_MODEL), xs.reshape(dbatch, dseq, D_MODEL), jnp.stack(nsa_p), jnp.stack(nsa_s),
            jnp.stack(win_p), jnp.stack(win_s), jnp.stack(sb_p), jnp.stack(sb_s), jnp.stack(mem_p))
```

```python
import functools
import math

import numpy as np
import jax
import jax.numpy as jnp
from jax import lax
from jax.experimental import pallas as pl
from jax.experimental.pallas import tpu as pltpu

F32 = jnp.float32
BF16 = jnp.bfloat16

D_MODEL = 1024
HEAD_DIM = 64
TOK_HEADS = 12
X_HEADS = 4
TOK_WIDTH = TOK_HEADS * HEAD_DIM
X_WIDTH = X_HEADS * HEAD_DIM
N_MEM = 256
NSA_G = 3
NSA_R = 4
CMP_BLOCK = 32
CMP_STRIDE = 16
CMP_HID = 128
SLC_BLOCK = 64
SLC_TOPN = 8
WINDOW = 512
FORCE_BONUS = 1e4
N_BUCKETS = 32
MAX_DIST = 1024
FFN_HIDDEN = 2816
RMS_EPS = 1e-6
PAGE = 128
SCALE = HEAD_DIM ** -0.5

LANES = 128
HALF = LANES // 2
VMEM_LIMIT = 56 * 1024 * 1024

EXP_ZERO = -104.0
NEG = -1e30


def _cp(sem, vmem=VMEM_LIMIT):
    return pltpu.CompilerParams(dimension_semantics=sem, vmem_limit_bytes=vmem)


def _dot(a, b):
    return jnp.dot(a, b, preferred_element_type=F32)


def _dot_t(a, b):
    return lax.dot_general(a, b, (((1,), (1,)), ((), ())), preferred_element_type=F32)


def _split_dot(x, w):
    hi = x.astype(BF16)
    lo = (x - hi.astype(F32)).astype(BF16)
    return _dot(hi, w) + _dot(lo, w)


def _rms_rows(x, g):
    ms = jnp.mean(x * x, axis=-1, keepdims=True)
    return x * lax.rsqrt(ms + RMS_EPS) * g


def _head_scale(h, gain, flag, s128):
    x2 = h * h
    parts = []
    for k in range(h.shape[1] // LANES):
        parts.append(_split_dot(x2[:, LANES * k:LANES * (k + 1)], s128))
    ssq = parts[0] if len(parts) == 1 else jnp.concatenate(parts, axis=1)
    r = lax.rsqrt(ssq * (1.0 / HEAD_DIM) + RMS_EPS)
    return h * jnp.where(flag > 0.5, r * gain, 1.0)


def _sigmoid(x):
    return 1.0 / (1.0 + jnp.exp(-x))


def _softplus(z):
    return jnp.maximum(z, 0.0) + jnp.log1p(jnp.exp(-jnp.abs(z)))


def _group_ones():
    i = np.arange(LANES)
    return jnp.asarray((i[:, None] // HALF == i[None, :] // HALF).astype(np.float32), BF16)


def _inproj_body(x_ref, g_ref, w_ref, gf_ref, s_ref, *rest, chunks, tchunks):
    if tchunks:
        wt_ref, gcol_ref = rest[:2]
        out_refs = rest[2:]
    else:
        out_refs = rest
    xn = _rms_rows(x_ref[...], g_ref[...]).astype(BF16)
    for c0, c1, norm, outs in chunks:
        h = _dot(xn, w_ref[:, c0:c1])
        if norm:
            h = _head_scale(h, gf_ref[0:1, c0:c1], gf_ref[1:2, c0:c1], s_ref[...])
        for oi, o0, kind in outs:
            o_ref = out_refs[oi]
            if kind == "sigmoid":
                o_ref[:, o0:o0 + (c1 - c0)] = _sigmoid(h)
            else:
                o_ref[:, o0:o0 + (c1 - c0)] = h.astype(o_ref.dtype)
    for r0, r1, norm, outs in tchunks:
        ht = _dot_t(wt_ref[r0:r1, :], xn)
        if norm:
            parts = []
            for k in range((r1 - r0) // HEAD_DIM):
                blk = ht[HEAD_DIM * k:HEAD_DIM * (k + 1)]
                ms = jnp.sum(blk * blk, axis=0, keepdims=True) * (1.0 / HEAD_DIM)
                parts.append(blk * lax.rsqrt(ms + RMS_EPS))
            ht = jnp.concatenate(parts, axis=0) * gcol_ref[r0:r1, :]
        for oi, o0, kind in outs:
            o_ref = out_refs[oi]
            if kind == "tiles":
                for j in range(ht.shape[1] // LANES):
                    o_ref[j, o0:o0 + (r1 - r0), :] = ht[:, LANES * j:LANES * (j + 1)].astype(o_ref.dtype)
            else:
                o_ref[o0:o0 + (r1 - r0), :] = ht.astype(o_ref.dtype)


def _inproj(x, g, w, gf, chunks, out_defs, tm, tchunks=(), rows_per_batch=None):
    m = x.shape[0]
    n = w.shape[1]
    assert m % tm == 0
    tb = rows_per_batch or m
    ntb = tb // tm
    nb = m // tb
    out_shape, out_specs = [], []
    for d in out_defs:
        if d[0] == "T":
            out_shape.append(jax.ShapeDtypeStruct((nb, d[1], tb), d[2]))
            out_specs.append(pl.BlockSpec((None, d[1], tm), lambda i: (i // ntb, 0, i % ntb)))
        elif d[0] == "tiles":
            out_shape.append(jax.ShapeDtypeStruct((nb, tb // LANES, d[1], LANES), d[2]))
            out_specs.append(pl.BlockSpec((None, tm // LANES, d[1], LANES), lambda i: (i // ntb, i % ntb, 0, 0)))
        else:
            out_shape.append(jax.ShapeDtypeStruct((m, d[0]), d[1]))
            out_specs.append(pl.BlockSpec((tm, d[0]), lambda i: (i, 0)))
    in_specs = [
        pl.BlockSpec((tm, D_MODEL), lambda i: (i, 0)),
        pl.BlockSpec((1, D_MODEL), lambda i: (0, 0)),
        pl.BlockSpec((D_MODEL, n), lambda i: (0, 0)),
        pl.BlockSpec((2, n), lambda i: (0, 0)),
        pl.BlockSpec((LANES, LANES), lambda i: (0, 0)),
    ]
    args = [x, g, w, gf, _group_ones()]
    if tchunks:
        in_specs += [pl.BlockSpec((n, D_MODEL), lambda i: (0, 0)), pl.BlockSpec((n, 1), lambda i: (0, 0))]
        args += [w.T, gf[0].reshape(n, 1)]
    return pl.pallas_call(
        functools.partial(_inproj_body, chunks=chunks, tchunks=tchunks),
        grid=(m // tm,),
        in_specs=in_specs,
        out_specs=out_specs,
        out_shape=out_shape,
        compiler_params=_cp(("parallel",)),
        name="inproj",
    )(*args)


NSA_N = 2304
NSA_CHUNKS = (
    (0, 384, True, ((0, 0, "cast"),)),
    (384, 768, True, ((0, 384, "cast"),)),
    (768, 1152, False, ((1, 0, "cast"), (3, 0, "cast"))),
    (1152, 1536, True, ((1, 384, "cast"), (3, 384, "cast"))),
    (1536, 1920, True, ((2, 0, "cast"), (3, 768, "cast"))),
    (1920, 2176, True, ((4, 0, "cast"),)),
    (2176, 2304, False, ((5, 0, "sigmoid"),)),
)
SB_N = 2560
SB_CHUNKS = (
    (0, 384, False, ((0, 0, "cast"),)),
    (384, 768, False, ((0, 384, "cast"),)),
    (768, 1152, False, ((1, 0, "cast"), (2, 0, "cast"))),
    (1152, 1536, False, ((1, 384, "cast"), (2, 384, "cast"))),
    (1536, 1920, False, ((1, 768, "cast"), (2, 768, "cast"))),
    (1920, 2304, False, ((1, 1152, "cast"), (2, 1152, "cast"))),
    (2304, 2560, True, ((3, 0, "cast"),)),
)
NSA_CHUNKS_P = (
    (0, 384, True, ((0, 0, "cast"),)),
    (384, 768, True, ((0, 384, "cast"),)),
    (768, 1152, False, ((1, 0, "cast"), (2, 0, "cast"))),
    (1152, 1536, True, ((2, 384, "cast"),)),
    (1536, 1920, True, ((2, 768, "cast"),)),
    (1920, 2176, True, ((3, 0, "cast"),)),
    (2176, 2304, False, ((4, 0, "sigmoid"),)),
)
NSA_TCHUNKS_P = (
    (768, 1152, False, ((5, 0, "rows"),)),
    (1152, 1344, True, ((5, 384, "rows"),)),
    (1344, 1536, False, ((5, 576, "rows"), (7, 0, "tiles"))),
    (1536, 1728, True, ((6, 0, "rows"),)),
    (1728, 1920, False, ((6, 192, "rows"), (7, 192, "tiles"))),
)
SB_CHUNKS_P = (
    (0, 384, False, ((0, 0, "cast"),)),
    (384, 768, False, ((0, 384, "cast"),)),
    (768, 1152, False, ((1, 0, "cast"),)),
    (1152, 1536, False, ((1, 384, "cast"),)),
    (1536, 1920, False, ((1, 768, "cast"),)),
    (1920, 2304, False, ((1, 1152, "cast"),)),
    (2304, 2560, True, ((2, 0, "cast"),)),
)
SB_TCHUNKS_P = (
    (768, 1152, False, ((3, 0, "rows"),)),
    (1152, 1536, False, ((3, 384, "rows"),)),
    (1536, 1920, False, ((3, 768, "rows"),)),
    (1920, 2304, False, ((3, 1152, "rows"),)),
)
MEM_CHUNKS_P = (
    (0, 256, True, ((0, 0, "cast"),)),
    (256, 512, False, ((0, 256, "cast"),)),
)
MEM_TCHUNKS_P = (
    (0, 256, True, ((1, 0, "rows"),)),
    (256, 512, False, ((1, 256, "rows"),)),
)
NSA_CHUNKS_S = (
    (0, 384, True, ((0, 0, "cast"),)),
    (384, 768, True, ((0, 384, "cast"),)),
    (768, 1152, False, ((1, 0, "cast"),)),
    (1152, 1536, True, ((1, 384, "cast"),)),
    (1536, 1920, True, ((2, 0, "cast"),)),
    (1920, 2176, True, ((3, 0, "cast"),)),
    (2176, 2304, False, ((4, 0, "sigmoid"),)),
)
SB_CHUNKS_S = (
    (0, 384, False, ((0, 0, "cast"),)),
    (384, 768, False, ((0, 384, "cast"),)),
    (768, 1152, False, ((1, 0, "cast"),)),
    (1152, 1536, False, ((1, 384, "cast"),)),
    (1536, 1920, False, ((1, 768, "cast"),)),
    (1920, 2304, False, ((1, 1152, "cast"),)),
    (2304, 2560, True, ((2, 0, "cast"),)),
)
MEM_CHUNKS = (
    (0, 256, True, ((0, 0, "cast"), (1, 0, "cast"))),
    (256, 512, False, ((0, 256, "cast"), (1, 256, "cast"))),
)


def _nsa_in_weights(w_in, qk_norm, xq_gain):
    o = TOK_WIDTH + 6 * NSA_G * HEAD_DIM
    w = jnp.concatenate([w_in[:, :o + X_WIDTH],
                         jnp.pad(w_in[:, o + X_WIDTH:], ((0, 0), (0, LANES - 3 * TOK_HEADS)))], axis=1)
    kvw = NSA_G * HEAD_DIM
    one = jnp.ones((kvw,), F32)
    zero = jnp.zeros((kvw,), F32)
    gain = jnp.concatenate([jnp.tile(qk_norm[0], TOK_HEADS), one, one, jnp.tile(qk_norm[2], NSA_G), one,
                            jnp.tile(qk_norm[3], NSA_G), one, jnp.tile(xq_gain, X_HEADS), jnp.ones((LANES,), F32)])
    flag = jnp.concatenate([jnp.ones((TOK_WIDTH,), F32), zero, zero, one, zero, one, zero,
                            jnp.ones((X_WIDTH,), F32), jnp.zeros((LANES,), F32)])
    return w.astype(BF16), jnp.stack([gain, flag])


def _sb_in_weights(w_in, xq_gain):
    gain = jnp.concatenate([jnp.ones((3 * TOK_WIDTH,), F32), jnp.tile(xq_gain, X_HEADS)])
    flag = jnp.concatenate([jnp.zeros((3 * TOK_WIDTH,), F32), jnp.ones((X_WIDTH,), F32)])
    return w_in.astype(BF16), jnp.stack([gain, flag])


def _mem_in_weights(w_kv, k_gain):
    gain = jnp.concatenate([jnp.tile(k_gain, X_HEADS), jnp.ones((X_WIDTH,), F32)])
    flag = jnp.concatenate([jnp.ones((X_WIDTH,), F32), jnp.zeros((X_WIDTH,), F32)])
    return w_kv.astype(BF16), jnp.stack([gain, flag])


def _ffn_body(x_ref, tok_ref, mem_ref, wot_ref, wom_ref, g_ref, wa_ref, wu_ref, wo_ref, o_ref, acc_ref, xn_ref):
    j = pl.program_id(1)

    @pl.when(j == 0)
    def _():
        xm = (x_ref[...] + _dot(tok_ref[...].astype(BF16), wot_ref[...])
              + _dot(mem_ref[...].astype(BF16), wom_ref[...]))
        acc_ref[...] = xm
        xn_ref[...] = _rms_rows(xm, g_ref[...]).astype(BF16)

    xn = xn_ref[...]
    a = _dot(xn, wa_ref[...])
    u = _dot(xn, wu_ref[...])
    hsw = (a * _sigmoid(a) * u).astype(BF16)
    acc_ref[...] += _dot(hsw, wo_ref[...])

    @pl.when(j == pl.num_programs(1) - 1)
    def _():
        o_ref[...] = acc_ref[...]


def _out_ffn(x, tok, mem, wo_tok, wo_mem, g, w_in, w_out, tm, th):
    m = x.shape[0]
    nh = FFN_HIDDEN // th
    return pl.pallas_call(
        _ffn_body,
        grid=(m // tm, nh),
        in_specs=[
            pl.BlockSpec((tm, D_MODEL), lambda i, j: (i, 0)),
            pl.BlockSpec((tm, TOK_WIDTH), lambda i, j: (i, 0)),
            pl.BlockSpec((tm, X_WIDTH), lambda i, j: (i, 0)),
            pl.BlockSpec((TOK_WIDTH, D_MODEL), lambda i, j: (0, 0)),
            pl.BlockSpec((X_WIDTH, D_MODEL), lambda i, j: (0, 0)),
            pl.BlockSpec((1, D_MODEL), lambda i, j: (0, 0)),
            pl.BlockSpec((D_MODEL, th), lambda i, j: (0, j)),
            pl.BlockSpec((D_MODEL, th), lambda i, j: (0, j + nh)),
            pl.BlockSpec((th, D_MODEL), lambda i, j: (j, 0)),
        ],
        out_specs=pl.BlockSpec((tm, D_MODEL), lambda i, j: (i, 0)),
        out_shape=jax.ShapeDtypeStruct((m, D_MODEL), F32),
        scratch_shapes=[pltpu.VMEM((tm, D_MODEL), F32), pltpu.VMEM((tm, D_MODEL), BF16)],
        compiler_params=_cp(("parallel", "arbitrary")),
        name="out_ffn",
    )(x, tok, mem, wo_tok, wo_mem, g, w_in, w_in, w_out)


def _mem_heads(q, kv, lane, feature_major=False):
    outs = []
    for c in range(X_WIDTH // LANES):
        q2 = q[:, LANES * c:LANES * (c + 1)]
        if feature_major:
            k2 = kv[LANES * c:LANES * (c + 1), :]
            v2 = kv[X_WIDTH + LANES * c:X_WIDTH + LANES * (c + 1), :]
        else:
            k2 = kv[:, LANES * c:LANES * (c + 1)]
            v2 = kv[:, X_WIDTH + LANES * c:X_WIDTH + LANES * (c + 1)]
        halves = []
        for half in range(2):
            sel = (lane < HALF) if half == 0 else (lane >= HALF)
            qm = jnp.where(sel, q2, jnp.zeros_like(q2))
            s = (_dot(qm, k2) if feature_major else _dot_t(qm, k2)) * SCALE
            e = jnp.exp(s - jnp.max(s, axis=-1, keepdims=True))
            p = (e / jnp.sum(e, axis=-1, keepdims=True)).astype(BF16)
            halves.append(_dot_t(p, v2) if feature_major else _dot(p, v2))
        outs.append(jnp.where(lane < HALF, halves[0], halves[1]))
    return jnp.concatenate(outs, axis=1)


def _memattn_p_body(q_ref, kv_ref, o_ref):
    lane = lax.broadcasted_iota(jnp.int32, (q_ref.shape[0], LANES), 1)
    o_ref[...] = _mem_heads(q_ref[...], kv_ref[...], lane).astype(o_ref.dtype)


def _memattn_prompt(xq, mkv, batch, tq):
    m = xq.shape[0]
    nt = m // batch // tq
    return pl.pallas_call(
        _memattn_p_body,
        grid=(batch, nt),
        in_specs=[pl.BlockSpec((tq, X_WIDTH), lambda b, i: (b * nt + i, 0)),
                  pl.BlockSpec((N_MEM, 2 * X_WIDTH), lambda b, i: (b, 0))],
        out_specs=pl.BlockSpec((tq, X_WIDTH), lambda b, i: (b * nt + i, 0)),
        out_shape=jax.ShapeDtypeStruct((m, X_WIDTH), BF16),
        compiler_params=_cp(("parallel", "parallel")),
        name="memattn_prompt",
    )(xq, mkv)


def _memattn_s_body(q_ref, kv_ref, o_ref, *, nb, t):
    lane = lax.broadcasted_iota(jnp.int32, (t, LANES), 1)
    for bi in range(nb):
        q = q_ref[bi * t:(bi + 1) * t, :].astype(BF16)
        o_ref[bi * t:(bi + 1) * t, :] = _mem_heads(q, kv_ref[bi].astype(BF16), lane, feature_major=True)


def _memattn_sample(xq, mkv, layer, t, nb):
    m = xq.shape[0]
    batch = m // t
    return pl.pallas_call(
        functools.partial(_memattn_s_body, nb=nb, t=t),
        grid=(batch // nb,),
        in_specs=[pl.BlockSpec((nb * t, X_WIDTH), lambda i: (i, 0)),
                  pl.BlockSpec((None, nb, 2 * X_WIDTH, N_MEM), lambda i: (layer, i, 0, 0))],
        out_specs=pl.BlockSpec((nb * t, X_WIDTH), lambda i: (i, 0)),
        out_shape=jax.ShapeDtypeStruct((m, X_WIDTH), F32),
        compiler_params=_cp(("parallel",)),
        name="memattn_sample",
    )(xq, mkv)


def _sb_block(z, strict, carry, u_tri):
    sp = _softplus(z)
    ls = -sp if strict is None else jnp.where(strict, -sp, 0.0)
    after = _split_dot(ls, u_tri) + carry
    a = jnp.exp(z - sp + after)
    if strict is not None:
        a = jnp.where(strict, a, 0.0)
    return a, carry + jnp.sum(ls, axis=1, keepdims=True)


def _upper_tri(n):
    r = lax.broadcasted_iota(jnp.int32, (n, n), 0)
    c = lax.broadcasted_iota(jnp.int32, (n, n), 1)
    return jnp.where(r > c, 1.0, 0.0).astype(BF16)


def _sb_prompt_body(q_ref, k_ref, v_ref, o_ref, qh_s, c_s, a_s, *, tq):
    i = pl.program_id(1)
    npair = TOK_WIDTH // LANES
    u_tri = _upper_tri(tq)
    lane = lax.broadcasted_iota(jnp.int32, (tq, LANES), 1)
    for pair in range(npair):
        q2 = q_ref[:, LANES * pair:LANES * (pair + 1)] * SCALE
        zero = jnp.zeros_like(q2)
        qh_s[2 * pair] = jnp.where(lane < HALF, q2, zero)
        qh_s[2 * pair + 1] = jnp.where(lane >= HALF, q2, zero)
    c_s[...] = jnp.zeros_like(c_s)
    a_s[...] = jnp.zeros_like(a_s)

    def key_block(kb, strict):
        ks = pl.multiple_of(kb * tq, tq)
        heads = range(TOK_HEADS)
        kblk = [k_ref[pl.ds(ks, tq), LANES * p:LANES * (p + 1)] for p in range(npair)]
        vblk = [v_ref[pl.ds(ks, tq), LANES * p:LANES * (p + 1)] for p in range(npair)]
        z = [_dot_t(qh_s[h], kblk[h // 2]) for h in heads]
        sp = [_softplus(z[h]) for h in heads]
        ls = [-sp[h] if strict is None else jnp.where(strict, -sp[h], 0.0) for h in heads]
        after = [_split_dot(ls[h], u_tri) for h in heads]
        old = [c_s[h] for h in heads]
        w = [jnp.exp(z[h] - sp[h] + after[h] + old[h]) for h in heads]
        if strict is not None:
            w = [jnp.where(strict, w[h], 0.0) for h in heads]
        pv = [_dot(w[h].astype(BF16), vblk[h // 2]) for h in heads]
        top = None
        for h in heads:
            carry = old[h] + jnp.sum(ls[h], axis=1, keepdims=True)
            c_s[h] = carry
            top = carry if top is None else jnp.maximum(top, carry)
        for p in range(npair):
            a_s[p] += jnp.where(lane < HALF, pv[2 * p], pv[2 * p + 1])
        return jnp.max(top)

    row = lax.broadcasted_iota(jnp.int32, (tq, tq), 0)
    col = lax.broadcasted_iota(jnp.int32, (tq, tq), 1)
    top0 = key_block(i, col < row)

    def cond(c):
        kb, top = c
        return jnp.logical_and(kb >= 0, top > EXP_ZERO)

    def body(c):
        kb, _ = c
        return kb - 1, key_block(kb, None)

    lax.while_loop(cond, body, (i - 1, top0))
    for pair in range(npair):
        o_ref[:, LANES * pair:LANES * (pair + 1)] = a_s[pair].astype(o_ref.dtype)


def _sb_prompt(q, kvb, batch, tq):
    assert tq == LANES
    m = q.shape[0]
    t = m // batch
    nt = t // tq
    return pl.pallas_call(
        functools.partial(_sb_prompt_body, tq=tq),
        grid=(batch, nt),
        in_specs=[pl.BlockSpec((tq, TOK_WIDTH), lambda b, i: (b * nt + i, 0)),
                  pl.BlockSpec((t, TOK_WIDTH), lambda b, i: (b, 0)),
                  pl.BlockSpec((t, TOK_WIDTH), lambda b, i: (b, 1))],
        out_specs=pl.BlockSpec((tq, TOK_WIDTH), lambda b, i: (b * nt + i, 0)),
        out_shape=jax.ShapeDtypeStruct((m, TOK_WIDTH), BF16),
        scratch_shapes=[pltpu.VMEM((TOK_HEADS, tq, LANES), BF16), pltpu.VMEM((TOK_HEADS, tq, LANES), F32),
                        pltpu.VMEM((TOK_WIDTH // LANES, tq, LANES), F32)],
        compiler_params=_cp(("parallel", "parallel")),
        name="sb_prompt",
    )(q, kvb, kvb)


def _sb_sample_body(pt_ref, q_ref, new_ref, *rest, t, n_pages):
    page_refs = rest[:n_pages]
    o_ref = rest[n_pages]
    qbd_ref, acc_ref, carry_ref, blk_ref = rest[n_pages + 1:]
    del pt_ref
    rows = TOK_HEADS * t
    lane = lax.broadcasted_iota(jnp.int32, (rows, TOK_WIDTH), 1)
    rowi = lax.broadcasted_iota(jnp.int32, (rows, TOK_WIDTH), 0)
    own = (lane // HEAD_DIM) == (rowi // t)
    q = q_ref[0] * SCALE
    qbd_ref[...] = jnp.where(own, jnp.concatenate([q] * TOK_HEADS, axis=0), 0.0).astype(BF16)
    u_tri = _upper_tri(PAGE)

    blk_ref[...] = jnp.zeros_like(blk_ref)
    blk_ref[0:t, :] = new_ref[0]
    kcol = lax.broadcasted_iota(jnp.int32, (rows, PAGE), 1)
    trow = lax.broadcasted_iota(jnp.int32, (rows, PAGE), 0) % t
    strict = kcol < trow
    kv = blk_ref[...]
    z = _dot_t(qbd_ref[...], kv[:, :TOK_WIDTH].astype(BF16))
    w, carry = _sb_block(z, strict, jnp.zeros((rows, 1), F32), u_tri)
    acc_ref[...] = _dot(w.astype(BF16), kv[:, TOK_WIDTH:].astype(BF16))
    carry_ref[...] = carry

    for p in range(n_pages - 1, -1, -1):
        @pl.when(jnp.max(carry_ref[...]) > EXP_ZERO)
        def _(p=p):
            kt = page_refs[p][0:TOK_WIDTH, :].astype(BF16)
            vt = page_refs[p][TOK_WIDTH:2 * TOK_WIDTH, :].astype(BF16)
            w, carry = _sb_block(_dot(qbd_ref[...], kt), None, carry_ref[...], u_tri)
            acc_ref[...] += _dot_t(w.astype(BF16), vt)
            carry_ref[...] = carry

    acc = jnp.where(own, acc_ref[...], 0.0)
    out = acc[0:t]
    for h in range(1, TOK_HEADS):
        out = out + acc[h * t:(h + 1) * t]
    o_ref[0] = out


def _sb_sample(q, new_rows, pool, layer, page_table):
    batch, t, _ = q.shape
    n_pages = page_table.shape[1]
    rows = TOK_HEADS * t
    page_specs = [
        pl.BlockSpec((None, None, 2 * TOK_WIDTH, PAGE), lambda b, pt, p=p: (layer, pt[b, p], 0, 0))
        for p in range(n_pages)
    ]
    grid_spec = pltpu.PrefetchScalarGridSpec(
        num_scalar_prefetch=1,
        grid=(batch,),
        in_specs=[pl.BlockSpec((1, t, TOK_WIDTH), lambda b, pt: (b, 0, 0)),
                  pl.BlockSpec((1, t, 2 * TOK_WIDTH), lambda b, pt: (b, 0, 0))] + page_specs,
        out_specs=pl.BlockSpec((1, t, TOK_WIDTH), lambda b, pt: (b, 0, 0)),
        scratch_shapes=[pltpu.VMEM((rows, TOK_WIDTH), BF16), pltpu.VMEM((rows, TOK_WIDTH), F32),
                        pltpu.VMEM((rows, 1), F32), pltpu.VMEM((PAGE, 2 * TOK_WIDTH), F32)],
    )
    return pl.pallas_call(
        functools.partial(_sb_sample_body, t=t, n_pages=n_pages),
        grid_spec=grid_spec,
        out_shape=jax.ShapeDtypeStruct((batch, t, TOK_WIDTH), F32),
        compiler_params=_cp(("arbitrary",)),
        name="sb_sample",
    )(page_table, q, new_rows, *([pool] * n_pages))


def _masked_softmax(s, mask):
    s = jnp.where(mask, s, -jnp.inf)
    m = jnp.max(s, axis=-1, keepdims=True)
    m = jnp.where(m == -jnp.inf, 0.0, m)
    e = jnp.exp(s - m)
    return e / jnp.maximum(jnp.sum(e, axis=-1, keepdims=True), 1e-30)


def _compress(load_chunks, nch, wc_ref, w1_ref, pe_ref, w2p_ref, gf_ref, s128):
    acc = [jnp.zeros((nch, 4 * CMP_HID), F32) for _ in range(3)]
    for p in range(CMP_STRIDE):
        for c in range(3):
            acc[c] = acc[c] + _dot(load_chunks(p, c).astype(BF16), wc_ref[c, p])
    bias = [_dot(pe_ref[tt], w1_ref[tt])[0:1] for tt in range(2)]
    outs = []
    for c, (ta, tb) in enumerate(((0, 0), (0, 1), (1, 1))):
        hid = []
        for half, tt in ((0, ta), (1, tb)):
            first = acc[c][:, 2 * CMP_HID * half:2 * CMP_HID * half + CMP_HID]
            second = acc[c][:, 2 * CMP_HID * half + CMP_HID:2 * CMP_HID * (half + 1)]
            pre = first + pltpu.roll(second, nch - 1, 0) + bias[tt]
            hid.append((pre * _sigmoid(pre)).astype(BF16))
        outs.append(_dot(hid[0], w2p_ref[ta, 0]) + _dot(hid[1], w2p_ref[tb, 1]))
    ckv = jnp.concatenate(outs, axis=1)
    return _head_scale(ckv, gf_ref[0:1, :], gf_ref[1:2, :], s128)


def _select_topn(score, topn):
    lane = lax.broadcasted_iota(jnp.int32, score.shape, 1)
    sel = jnp.zeros(score.shape, F32)
    cur = score
    for _ in range(topn):
        mx = jnp.max(cur, axis=1, keepdims=True)
        is_max = jnp.logical_and(cur == mx, mx > -jnp.inf)
        idx = jnp.min(jnp.where(is_max, lane, LANES), axis=1, keepdims=True)
        pick = lane == idx
        sel = jnp.where(pick, 1.0, sel)
        cur = jnp.where(pick, -jnp.inf, cur)
    return sel


def _stack_heads(q, pq_ref, g, t, dtype=BF16):
    return jnp.concatenate([_dot(q, pq_ref[NSA_R * g + r])[0:t] for r in range(NSA_R)], axis=0).astype(dtype)


def _place_heads(o, g, t, gate_of, lane):
    vh = (g + 1) % 2
    tiles = []
    for rp in range(2):
        outs = []
        for rr in range(2):
            r = 2 * rp + rr
            o_r = o[r * t:(r + 1) * t]
            if vh != rr:
                o_r = pltpu.roll(o_r, HALF, 1)
            outs.append(o_r * gate_of(NSA_R * g + r))
        tiles.append(jnp.where(lane < HALF, outs[0], outs[1]))
    return tiles


def _nsa_compress_body(r0_ref, r1_ref, r2_ref, wc_ref, w1_ref, pe_ref, w2p_ref, gf_ref, s_ref, o_ref, *, nch):
    tiles = (r0_ref, r1_ref, r2_ref)

    def load(p, c):
        return tiles[c][pl.ds(p, nch, stride=CMP_STRIDE), :]
    o_ref[...] = _compress(load, nch, wc_ref, w1_ref, pe_ref, w2p_ref, gf_ref, s_ref[...])


def _cmp_const_specs():
    return [pl.BlockSpec((3, CMP_STRIDE, LANES, 4 * CMP_HID), lambda *a: (0, 0, 0, 0)),
            pl.BlockSpec((2, CMP_BLOCK * HEAD_DIM, CMP_HID), lambda *a: (0, 0, 0)),
            pl.BlockSpec((2, 8, CMP_BLOCK * HEAD_DIM), lambda *a: (0, 0, 0)),
            pl.BlockSpec((2, 2, CMP_HID, LANES), lambda *a: (0, 0, 0, 0)),
            pl.BlockSpec((2, 3 * LANES), lambda *a: (0, 0)),
            pl.BlockSpec((LANES, LANES), lambda *a: (0, 0))]


def _nsa_compress(rows, cw, batch):
    m = rows.shape[0]
    t = m // batch
    nch = t // CMP_STRIDE
    return pl.pallas_call(
        functools.partial(_nsa_compress_body, nch=nch),
        grid=(batch,),
        in_specs=[pl.BlockSpec((t, LANES), lambda b, c=c: (b, c)) for c in range(3)] + _cmp_const_specs(),
        out_specs=pl.BlockSpec((nch, 3 * LANES), lambda b: (b, 0)),
        out_shape=jax.ShapeDtypeStruct((batch * nch, 3 * LANES), F32),
        compiler_params=_cp(("parallel",)),
        name="nsa_compress",
    )(rows, rows, rows, *cw, _group_ones())


def _nsa_cmp_body(q_ref, ckv_ref, bias_ref, ov_ref, pq_ref, g_ref, sel_ref, part_ref, *, tq, ncp, topn):
    i = pl.program_id(1)
    q0 = i * tq
    q = q_ref[...]
    ckv = ckv_ref[...].astype(BF16)
    rows4 = NSA_R * tq
    tpos = q0 + lax.broadcasted_iota(jnp.int32, (rows4, ncp), 0) % tq
    ncol = lax.broadcasted_iota(jnp.int32, (rows4, ncp), 1)
    mask_c = (CMP_STRIDE * ncol + CMP_BLOCK - 1) <= tpos
    lane = lax.broadcasted_iota(jnp.int32, (tq, LANES), 1)
    lane3 = lax.broadcasted_iota(jnp.int32, (NSA_G * tq, LANES), 1)
    tq_pos = q0 + lax.broadcasted_iota(jnp.int32, (NSA_G * tq, LANES), 0) % tq
    valid = lane3 * SLC_BLOCK <= tq_pos
    curb = tq_pos // SLC_BLOCK
    forced = (lane3 == 0) | (lane3 == curb) | (lane3 == curb - 1)
    gates = g_ref[...]
    groups = range(NSA_G)
    qg = [_stack_heads(q, pq_ref, g, tq) * SCALE for g in groups]
    s = [_dot_t(qg[g], ckv[:, LANES * (g // 2):LANES * (g // 2 + 1)]) + bias_ref[0, g] for g in groups]
    pb = [_masked_softmax(s[g], mask_c).astype(BF16) for g in groups]
    o = [_dot(pb[g], ckv[:, LANES * ((3 + g) // 2):LANES * ((3 + g) // 2 + 1)]) for g in groups]
    imp4 = [_dot(pb[g], ov_ref[...]) for g in groups]
    imp = jnp.concatenate([imp4[g][0:tq] + imp4[g][tq:2 * tq] + imp4[g][2 * tq:3 * tq] + imp4[g][3 * tq:4 * tq]
                           for g in groups], axis=0)
    score = jnp.where(valid, imp + jnp.where(forced, FORCE_BONUS, 0.0), -jnp.inf)
    sel = _select_topn(score, topn)
    for g in groups:
        sel_ref[0, g] = sel[g * tq:(g + 1) * tq].T.astype(sel_ref.dtype)
        tiles = _place_heads(o[g], g, tq, lambda h: gates[:, h:h + 1], lane)
        for rp in range(2):
            k = 2 * g + rp
            part_ref[:, LANES * k:LANES * (k + 1)] = tiles[rp]


def _nsa_cmp(qn, ckv, biasc, ov, pq, gates, batch, tq):
    m = qn.shape[0]
    t = m // batch
    nt = t // tq
    ncp = t // CMP_STRIDE
    topn = min(SLC_TOPN, -(-t // SLC_BLOCK))
    return pl.pallas_call(
        functools.partial(_nsa_cmp_body, tq=tq, ncp=ncp, topn=topn),
        grid=(batch, nt),
        in_specs=[pl.BlockSpec((tq, TOK_WIDTH), lambda b, i: (b * nt + i, 0)),
                  pl.BlockSpec((ncp, 3 * LANES), lambda b, i: (b, 0)),
                  pl.BlockSpec((1, NSA_G, NSA_R * tq, ncp), lambda b, i: (i, 0, 0, 0)),
                  pl.BlockSpec((ncp, LANES), lambda b, i: (0, 0)),
                  pl.BlockSpec((TOK_HEADS, TOK_WIDTH, LANES), lambda b, i: (0, 0, 0)),
                  pl.BlockSpec((tq, LANES), lambda b, i: (b * nt + i, 0))],
        out_specs=[pl.BlockSpec((1, NSA_G, LANES, tq), lambda b, i: (b * nt + i, 0, 0, 0)),
                   pl.BlockSpec((tq, TOK_WIDTH), lambda b, i: (b * nt + i, 0))],
        out_shape=[jax.ShapeDtypeStruct((batch * nt, NSA_G, LANES, tq), BF16),
                   jax.ShapeDtypeStruct((m, TOK_WIDTH), F32)],
        compiler_params=_cp(("parallel", "parallel")),
        name="nsa_cmp_select",
    )(qn, ckv, biasc, ov, pq, gates)


N_NEAR = 8
PAIR = 2


def _nsa_slc_body(q_ref, kvb_ref, vt_ref, sel_ref, part_ref, g_ref, bt_ref, pq_ref, o_ref,
                  qt_s, m_s, acc_s, *, tq):
    i = pl.program_id(1)
    rows4 = NSA_R * tq
    q = q_ref[...]
    for g in range(NSA_G):
        qt_s[g, 0:LANES, :] = (_stack_heads(q, pq_ref, g, tq, F32) * SCALE).T.astype(BF16)
        unselected = ((sel_ref[0, g].astype(F32) - 1.0) * (-NEG)).astype(BF16)
        qt_s[g, LANES:2 * LANES, :] = jnp.concatenate([unselected] * NSA_R, axis=1)
    ones_rows = {n: jnp.ones((2 * 8, n * LANES), BF16) for n in (1, PAIR)}
    kk = lax.broadcasted_iota(jnp.int32, (LANES, rows4), 0)
    tt = lax.broadcasted_iota(jnp.int32, (LANES, rows4), 1) % tq
    causal = jnp.where(kk <= tt, 0.0, NEG)
    oldest = jnp.where(kk > tt, 0.0, NEG)
    khalf = {n: lax.broadcasted_iota(jnp.int32, (n * LANES, LANES), 0) // SLC_BLOCK for n in (1, PAIR)}
    jcol = {n: lax.broadcasted_iota(jnp.int32, (n * LANES, LANES), 1) for n in (1, PAIR)}
    lane = lax.broadcasted_iota(jnp.int32, (tq, LANES), 1)
    gates = g_ref[...]
    tok = [part_ref[:, LANES * k:LANES * (k + 1)] for k in range(2 * NSA_G)]

    def tile_step(kt, branch, extra, ntile=1):
        koff = 3 * LANES * branch
        nk = ntile * LANES
        ks = pl.multiple_of(kt * LANES, LANES)
        groups = range(NSA_G)
        kblk = [kvb_ref[pl.ds(ks, nk), pl.ds(LANES * ((6 + g) // 2) + koff, LANES)] for g in groups]
        vt = [jnp.concatenate(
            [jnp.concatenate([vt_ref[kt + u, pl.ds(HEAD_DIM * (NSA_G * branch + g), HEAD_DIM), :]
                              for u in range(ntile)], axis=1), ones_rows[ntile]], axis=0) for g in groups]
        bias = [jnp.concatenate([bt_ref[jnp.minimum(i - kt - u, N_NEAR), g] for u in range(ntile)], axis=0)
                for g in groups]
        if branch == 0:
            e_mat = jnp.where(jcol[ntile] == 2 * kt + khalf[ntile], 1.0, 0.0).astype(BF16)
            s = [_dot(jnp.concatenate([kblk[g], e_mat], axis=1), qt_s[g]) + bias[g] for g in groups]
        else:
            s = [_dot(kblk[g], qt_s[g, 0:LANES, :]) + bias[g] for g in groups]
        if extra is not None:
            s = [s[g] + extra for g in groups]
        m_old = [m_s[g] for g in groups]
        m_new = [jnp.maximum(m_old[g], jnp.max(s[g], axis=0, keepdims=True)) for g in groups]
        p = [jnp.exp(s[g] - m_new[g]).astype(BF16) for g in groups]
        alpha = [jnp.exp(m_old[g] - m_new[g]) for g in groups]
        pv = [_dot(vt[g], p[g]) for g in groups]
        for g in groups:
            acc_s[g] = alpha[g] * acc_s[g] + pv[g]
            m_s[g] = m_new[g]

    for branch in range(2):
        for g in range(NSA_G):
            m_s[g] = jnp.full((1, rows4), NEG, F32)
            acc_s[g] = jnp.zeros(acc_s.shape[1:], F32)

        if branch == 0:
            lo = 0
        else:
            nback = WINDOW // LANES

            @pl.when(i >= nback)
            def _():
                tile_step(i - nback, 1, oldest)
            lo = jnp.maximum(i - nback + 1, 0)

        def body(pp, carry, branch=branch, lo=lo):
            tile_step(lo + PAIR * pp, branch, None, PAIR)
            return carry
        lax.fori_loop(0, (i - lo) // PAIR, body, 0)

        @pl.when((i - lo) % PAIR == 1)
        def _(branch=branch):
            tile_step(i - 1, branch, None)
        tile_step(i, branch, causal)
        base = TOK_HEADS * (branch + 1)
        for g in range(NSA_G):
            acc = acc_s[g]
            o_t = acc[0:HEAD_DIM] / jnp.maximum(acc[HEAD_DIM:HEAD_DIM + 1], 1e-30)
            for rp in range(2):
                h0 = NSA_R * g + 2 * rp
                pair = jnp.concatenate([o_t[:, 2 * rp * tq:(2 * rp + 1) * tq],
                                        o_t[:, (2 * rp + 1) * tq:(2 * rp + 2) * tq]], axis=0)
                gate = jnp.where(lane < HALF, gates[:, base + h0:base + h0 + 1],
                                 gates[:, base + h0 + 1:base + h0 + 2])
                tok[2 * g + rp] = tok[2 * g + rp] + pair.T * gate
    for k in range(2 * NSA_G):
        o_ref[:, LANES * k:LANES * (k + 1)] = tok[k].astype(o_ref.dtype)


def _nsa_slc_win(qn, kvb, vt, sel, part, gates, bt, pq, batch):
    tq = LANES
    m = qn.shape[0]
    t = m // batch
    nt = t // tq
    rows4 = NSA_R * tq
    return pl.pallas_call(
        functools.partial(_nsa_slc_body, tq=tq),
        grid=(batch, nt),
        in_specs=[pl.BlockSpec((tq, TOK_WIDTH), lambda b, i: (b * nt + i, 0)),
                  pl.BlockSpec((t, 9 * LANES), lambda b, i: (b, 0)),
                  pl.BlockSpec((None, nt, 2 * NSA_G * HEAD_DIM, LANES), lambda b, i: (b, 0, 0, 0)),
                  pl.BlockSpec((1, NSA_G, LANES, tq), lambda b, i: (b * nt + i, 0, 0, 0)),
                  pl.BlockSpec((tq, TOK_WIDTH), lambda b, i: (b * nt + i, 0)),
                  pl.BlockSpec((tq, LANES), lambda b, i: (b * nt + i, 0)),
                  pl.BlockSpec((N_NEAR + 1, NSA_G, LANES, rows4), lambda b, i: (0, 0, 0, 0)),
                  pl.BlockSpec((TOK_HEADS, TOK_WIDTH, LANES), lambda b, i: (0, 0, 0))],
        out_specs=pl.BlockSpec((tq, TOK_WIDTH), lambda b, i: (b * nt + i, 0)),
        out_shape=jax.ShapeDtypeStruct((m, TOK_WIDTH), BF16),
        scratch_shapes=[pltpu.VMEM((NSA_G, 2 * LANES, rows4), BF16), pltpu.VMEM((NSA_G, 1, rows4), F32),
                        pltpu.VMEM((NSA_G, HEAD_DIM + 16, rows4), F32)],
        compiler_params=_cp(("parallel", "parallel")),
        name="nsa_slc_win",
    )(qn, kvb, vt, sel, part, gates, bt, pq)


def _nsa_sample_body(pt_ref, q_ref, new_ref, neww_ref, g_ref, win_ref, *rest, t, n_pages):
    page_refs = rest[:n_pages]
    (wc_ref, w1_ref, pe_ref, w2p_ref, gf_ref, s_ref, pq_ref, ov_ref, e_ref, bc_ref, bs_ref, bw_ref,
     tok_ref, wout_ref, x_buf, n_buf, nw_buf) = rest[n_pages:]
    del pt_ref
    past = n_pages * PAGE
    slen = past + PAGE
    nch = past // CMP_STRIDE
    n_cmp = nch - 1
    nwin = win_ref.shape[1]
    wlen = nwin + PAGE
    rows4 = NSA_R * t

    for c in range(3):
        for p in range(n_pages):
            x_buf[c, p * PAGE:(p + 1) * PAGE, :] = page_refs[p][LANES * c:LANES * (c + 1), :].T
    n_buf[...] = jnp.zeros_like(n_buf)
    n_buf[0:t, :] = new_ref[0]
    nw_buf[...] = jnp.zeros_like(nw_buf)
    nw_buf[0:t, :] = neww_ref[0]
    for c in range(3):
        full = jnp.concatenate([win_ref[LANES * c:LANES * (c + 1), :], nw_buf[:, LANES * c:LANES * (c + 1)].T], axis=1)
        wout_ref[LANES * c:LANES * (c + 1), :] = pltpu.roll(full, wlen - t, 1)[:, 0:nwin]

    def load(p, c):
        return x_buf[c, pl.ds(p, nch, stride=CMP_STRIDE), :]
    ckv = _compress(load, nch, wc_ref, w1_ref, pe_ref, w2p_ref, gf_ref, s_ref[...]).astype(BF16)

    q = jnp.concatenate([q_ref[0], jnp.zeros((8, TOK_WIDTH), F32)], axis=0).astype(BF16)
    new_tiles = [n_buf[:, LANES * k:LANES * (k + 1)].astype(BF16) for k in (3, 4, 5)]
    neww_tiles = [nw_buf[:, LANES * k:LANES * (k + 1)].astype(BF16) for k in (0, 1, 2)]
    win_tiles = [win_ref[LANES * k:LANES * (k + 1), :].astype(BF16) for k in (0, 1, 2)]
    tile_cache = {}

    def page_tile(k, p):
        if (k, p) not in tile_cache:
            tile_cache[(k, p)] = page_refs[p][LANES * k:LANES * (k + 1), :].astype(BF16)
        return tile_cache[(k, p)]
    gates = g_ref[0]
    lane = lax.broadcasted_iota(jnp.int32, (t, LANES), 1)
    lane3 = lax.broadcasted_iota(jnp.int32, (NSA_G * t, LANES), 1)
    tpos = past + lax.broadcasted_iota(jnp.int32, (NSA_G * t, LANES), 0) % t
    valid = lane3 * SLC_BLOCK <= tpos
    curb = tpos // SLC_BLOCK
    forced = (lane3 == 0) | (lane3 == curb) | (lane3 == curb - 1)
    mask_c = lax.broadcasted_iota(jnp.int32, (rows4, nch), 1) < n_cmp
    trow_s = lax.broadcasted_iota(jnp.int32, (rows4, slen), 0) % t
    kcol = lax.broadcasted_iota(jnp.int32, (rows4, slen), 1)
    vis_s = kcol <= past + trow_s
    trow_w = lax.broadcasted_iota(jnp.int32, (rows4, wlen), 0) % t
    wcol = lax.broadcasted_iota(jnp.int32, (rows4, wlen), 1)
    vis_w = jnp.logical_and(wcol > trow_w, wcol <= nwin + trow_w)

    groups = range(NSA_G)
    ki = (0, 0, 1)
    vi = (1, 2, 2)
    qg = [_stack_heads(q, pq_ref, g, t) * SCALE for g in groups]
    sc = [_dot_t(qg[g], ckv[:, LANES * (g // 2):LANES * (g // 2 + 1)]) + bc_ref[g] for g in groups]
    pc = [_masked_softmax(sc[g], mask_c).astype(BF16) for g in groups]
    o_c = [_dot(pc[g], ckv[:, LANES * ((3 + g) // 2):LANES * ((3 + g) // 2 + 1)]) for g in groups]
    imp4 = [_dot(pc[g], ov_ref[...]) for g in groups]
    imp = jnp.concatenate([imp4[g][0:t] + imp4[g][t:2 * t] + imp4[g][2 * t:3 * t] + imp4[g][3 * t:4 * t]
                           for g in groups], axis=0)
    score = jnp.where(valid, imp + jnp.where(forced, FORCE_BONUS, 0.0), -jnp.inf)
    sel = _select_topn(score, SLC_TOPN).astype(BF16)
    ss = [jnp.concatenate([_dot(qg[g], page_tile(3 + ki[g], p)) for p in range(n_pages)]
                          + [_dot_t(qg[g], new_tiles[ki[g]])], axis=1) + bs_ref[g] for g in groups]
    selx = _dot(sel, e_ref[...])
    ps = [_masked_softmax(ss[g], jnp.logical_and(
        vis_s, jnp.concatenate([selx[g * t:(g + 1) * t]] * NSA_R, axis=0) > 0.5)).astype(BF16) for g in groups]
    o_s = [_dot(ps[g][:, past:slen], new_tiles[vi[g]]) for g in groups]
    for p in range(n_pages):
        o_s = [o_s[g] + _dot_t(ps[g][:, p * PAGE:(p + 1) * PAGE], page_tile(3 + vi[g], p)) for g in groups]
    sw = [jnp.concatenate([_dot(qg[g], win_tiles[ki[g]]), _dot_t(qg[g], neww_tiles[ki[g]])], axis=1) + bw_ref[g]
          for g in groups]
    pw = [_masked_softmax(sw[g], vis_w).astype(BF16) for g in groups]
    o_w = [_dot_t(pw[g][:, 0:nwin], win_tiles[vi[g]]) + _dot(pw[g][:, nwin:wlen], neww_tiles[vi[g]]) for g in groups]

    def gate_col(c, g):
        return jnp.concatenate([gates[:, TOK_HEADS * c + NSA_R * g + r:TOK_HEADS * c + NSA_R * g + r + 1]
                                for r in range(NSA_R)], axis=0)
    for g in groups:
        o = gate_col(0, g) * o_c[g] + gate_col(1, g) * o_s[g] + gate_col(2, g) * o_w[g]
        tiles = _place_heads(o, g, t, lambda h: 1.0, lane)
        for rp in range(2):
            k = 2 * g + rp
            tok_ref[0, :, LANES * k:LANES * (k + 1)] = tiles[rp]


def _nsa_sample(q, new_rows, new_w, gates, win_cache, pool, layer, page_table, cw, pq, ov, e_mat, bc, bs, bw):
    batch, t, _ = q.shape
    n_pages = page_table.shape[1]
    past = n_pages * PAGE
    nwin = win_cache.shape[3]
    nch = past // CMP_STRIDE
    rows4 = NSA_R * t
    page_specs = [
        pl.BlockSpec((None, None, TOK_WIDTH, PAGE), lambda b, pt, p=p: (layer, pt[b, p], 0, 0))
        for p in range(n_pages)
    ]
    const3 = lambda b, pt: (0, 0, 0)
    const2 = lambda b, pt: (0, 0)
    grid_spec = pltpu.PrefetchScalarGridSpec(
        num_scalar_prefetch=1,
        grid=(batch,),
        in_specs=[pl.BlockSpec((1, t, TOK_WIDTH), lambda b, pt: (b, 0, 0)),
                  pl.BlockSpec((1, t, TOK_WIDTH), lambda b, pt: (b, 0, 0)),
                  pl.BlockSpec((1, t, 3 * LANES), lambda b, pt: (b, 0, 0)),
                  pl.BlockSpec((1, t, LANES), lambda b, pt: (b, 0, 0)),
                  pl.BlockSpec((None, None, 3 * LANES, nwin), lambda b, pt: (layer, b, 0, 0))]
        + page_specs + _cmp_const_specs()
        + [pl.BlockSpec((TOK_HEADS, TOK_WIDTH, LANES), const3),
           pl.BlockSpec((nch, LANES), const2),
           pl.BlockSpec((LANES, past + PAGE), const2),
           pl.BlockSpec((NSA_G, rows4, nch), const3),
           pl.BlockSpec((NSA_G, rows4, past + PAGE), const3),
           pl.BlockSpec((NSA_G, rows4, nwin + PAGE), const3)],
        out_specs=[pl.BlockSpec((1, t, TOK_WIDTH), lambda b, pt: (b, 0, 0)),
                   pl.BlockSpec((None, 3 * LANES, nwin), lambda b, pt: (b, 0, 0))],
        scratch_shapes=[pltpu.VMEM((3, past, LANES), F32), pltpu.VMEM((PAGE, TOK_WIDTH), F32),
                        pltpu.VMEM((PAGE, 3 * LANES), F32)],
    )
    return pl.pallas_call(
        functools.partial(_nsa_sample_body, t=t, n_pages=n_pages),
        grid_spec=grid_spec,
        out_shape=[jax.ShapeDtypeStruct((batch, t, TOK_WIDTH), F32),
                   jax.ShapeDtypeStruct((batch, 3 * LANES, nwin), F32)],
        compiler_params=_cp(("arbitrary",)),
        name="nsa_sample",
    )(page_table, q, new_rows, new_w, gates, win_cache, *([pool] * n_pages), *cw, _group_ones(),
      pq, ov, e_mat, bc, bs, bw)


def _rel_bucket(dist):
    exact = N_BUCKETS // 2
    d = jnp.maximum(dist, 0)
    far = exact + (jnp.log(jnp.maximum(d, 1).astype(F32) / exact)
                   / math.log(MAX_DIST / exact) * (N_BUCKETS - exact)).astype(jnp.int32)
    return jnp.where(d < exact, d, jnp.minimum(far, N_BUCKETS - 1))


NEG_PAD = 1024


def _dist_table(rel_bias, n):
    tab = rel_bias[_rel_bucket(jnp.arange(n, dtype=jnp.int32))].T.astype(F32)
    return jnp.pad(tab, ((0, 0), (NEG_PAD, 0)))


def _skew(v, nrows, step):
    p = v.shape[-1]
    flat = jnp.tile(v, (1,) * (v.ndim - 1) + (nrows,))[..., :nrows * (p - step)]
    return flat.reshape(v.shape[:-1] + (nrows, p - step))


def _toeplitz(tab, c0, nrows, ncols):
    lo = c0 - ncols + 1 + NEG_PAD
    hi = c0 + nrows + NEG_PAD
    u = jnp.flip(tab[:, lo:hi], axis=1)
    v = jnp.roll(u, -(nrows - 1), axis=1)
    return _skew(v, nrows, 1)[..., :ncols]


def _group_rows(m):
    return m.reshape(NSA_G, NSA_R * m.shape[1], m.shape[2])


def _head_placement():
    pq = np.zeros((TOK_HEADS, TOK_WIDTH, LANES), np.float32)
    d = np.arange(HEAD_DIM)
    for h in range(TOK_HEADS):
        pq[h, HEAD_DIM * h + d, HALF * ((h // NSA_R) % 2) + d] = 1.0
    return jnp.asarray(pq, BF16)


def _overlap(nrows, n_cmp, n_slc):
    cs = np.arange(nrows)[:, None] * CMP_STRIDE
    ss = np.arange(LANES)[None, :] * SLC_BLOCK
    ov = (cs < ss + SLC_BLOCK) & (cs + CMP_BLOCK > ss)
    ov &= (np.arange(nrows)[:, None] < n_cmp) & (np.arange(LANES)[None, :] < n_slc)
    return jnp.asarray(ov.astype(np.float32), BF16)


def _cmp_weights(pe, w1, w2, kn_cmp):
    w1b = w1.astype(BF16)
    w1p = w1b.reshape(2, 2, CMP_STRIDE, HEAD_DIM, CMP_HID)
    cat = jnp.concatenate([w1p[:, 0], w1p[:, 1]], axis=-1)
    z = jnp.zeros_like(cat[0])

    def pair(ta, tb):
        top = jnp.concatenate([cat[ta], z], axis=-1)
        bot = jnp.concatenate([z, cat[tb]], axis=-1)
        return jnp.concatenate([top, bot], axis=1)
    wc = jnp.stack([pair(0, 0), pair(0, 1), pair(1, 1)])
    pe8 = jnp.pad(pe.reshape(2, 1, CMP_BLOCK * HEAD_DIM), ((0, 0), (0, 7), (0, 0))).astype(BF16)
    w2b = w2.astype(BF16)
    zz = jnp.zeros_like(w2b)
    w2p = jnp.stack([jnp.concatenate([w2b, zz], axis=-1), jnp.concatenate([zz, w2b], axis=-1)], axis=1)
    kvw = NSA_G * HEAD_DIM
    gain = jnp.concatenate([jnp.tile(kn_cmp, NSA_G), jnp.ones((kvw,), F32)])
    flag = jnp.concatenate([jnp.ones((kvw,), F32), jnp.zeros((kvw,), F32)])
    return wc, w1b, pe8, w2p, jnp.stack([gain, flag])


def _prompt_tables(rel_bias, t, tq):
    ncp = t // CMP_STRIDE
    nt = t // tq
    n_cmp = (t - CMP_BLOCK) // CMP_STRIDE + 1
    n_slc = -(-t // SLC_BLOCK)
    tab = _dist_table(rel_bias, max(t, MAX_DIST) + LANES)
    per = LANES // CMP_STRIDE
    lo = [NEG_PAD - CMP_STRIDE * b - (CMP_BLOCK - 1) for b in range(per)]
    base = jnp.stack([tab[:, x:x + t + LANES] for x in lo], axis=1)
    bc = _skew(base, ncp // per, LANES)
    bc = bc.reshape(NSA_G, NSA_R, per, ncp // per, nt, tq)
    bc = jnp.transpose(bc, (4, 0, 1, 5, 3, 2)).reshape(nt, NSA_G, NSA_R * tq, ncp)
    near = [_group_rows(_toeplitz(tab, LANES * d, LANES, LANES)) for d in range(N_NEAR)]
    far = jnp.broadcast_to(tab[:, -1].reshape(NSA_G, NSA_R, 1, 1), (NSA_G, NSA_R, LANES, LANES))
    bt = jnp.stack(near + [far.reshape(NSA_G, NSA_R * LANES, LANES)])
    bt = jnp.transpose(bt, (0, 1, 3, 2))
    return bc, bt, _overlap(ncp, n_cmp, n_slc)


def _sample_tables(rel_bias, past, t, nwin):
    nch = past // CMP_STRIDE
    n_cmp = (past + t - CMP_BLOCK) // CMP_STRIDE + 1
    n_slc = -(-(past + t) // SLC_BLOCK)
    tab = _dist_table(rel_bias, past + LANES)
    dist_c = past + np.arange(t)[:, None] - (np.arange(nch)[None, :] * CMP_STRIDE + CMP_BLOCK - 1)
    bc = _group_rows(tab[:, np.maximum(dist_c, -NEG_PAD) + NEG_PAD])
    bs = _group_rows(_toeplitz(tab, past, t, past + PAGE))
    bw = _group_rows(_toeplitz(tab, nwin, t, nwin + PAGE))
    e_mat = (np.arange(LANES)[:, None] == np.arange(past + PAGE)[None, :] // SLC_BLOCK)
    return bc, bs, bw, _overlap(nch, n_cmp, n_slc), jnp.asarray(e_mat.astype(np.float32), BF16)


def _nsa_prompt_mix(qn, kcvc, kvb, vt, gates, cw, tables, pq, batch):
    bc, bt, ov = tables
    ckv = _nsa_compress(kcvc, cw, batch)
    sel, part = _nsa_cmp(qn, ckv, bc, ov, pq, gates, batch, LANES)
    return _nsa_slc_win(qn, kvb, vt, sel, part, gates, bt, pq, batch)


TM = 512
TH = 1408
MEM_NB = 8


def kernel(x_prompt, x_sample, mem_prompt, cache_nsa_kv, cache_nsa_win, cache_sb_kv, cache_mem_kv, page_table,
           rel_bias, norm_mix, norm_ffn, norm_mem, w_in_nsa, w_in_sb, w_mem_kv, w_out, nsa_qk_norm, x_qk_norm,
           cmp_pe, cmp_w1, cmp_w2, w_ffn_in, w_ffn_out):
    batch, seq, _ = x_prompt.shape
    dbatch, dseq, _ = x_sample.shape
    depth = norm_mix.shape[0]
    n_pages = page_table.shape[1]
    past = n_pages * PAGE
    nwin = cache_nsa_win.shape[2]
    kvw = NSA_G * HEAD_DIM

    xp = x_prompt.reshape(batch * seq, D_MODEL)
    xs = x_sample.reshape(dbatch * dseq, D_MODEL)
    mem = mem_prompt.reshape(batch * N_MEM, D_MODEL)
    def feature_major(c):
        ct = jnp.transpose(c, (0, 1, 3, 4, 5, 2))
        return ct.reshape(c.shape[0], c.shape[1], -1, c.shape[2])
    def token_major(a, feat):
        return jnp.transpose(a.reshape((a.shape[0],) + feat + (a.shape[2],)), (0, 4, 1, 2, 3))
    nsa_pool = feature_major(cache_nsa_kv)
    sb_pool = feature_major(cache_sb_kv)
    win_cache = feature_major(cache_nsa_win)
    mem_cache = feature_major(cache_mem_kv)

    pq = _head_placement()
    ptab = _prompt_tables(rel_bias, seq, LANES)
    bc_s, bs_s, bw_s, ov_s, e_s = _sample_tables(rel_bias, past, dseq, nwin)
    wo = w_out.astype(BF16)
    wfi = w_ffn_in.astype(BF16)
    wfo = w_ffn_out.astype(BF16)

    nsa_p, nsa_s, win_p, win_s, sb_p, sb_s, mem_p = [], [], [], [], [], [], []
    for l in range(depth):
        j = l // 2
        gmix = norm_mix[l][None]
        wm, gfm = _mem_in_weights(w_mem_kv[l], x_qk_norm[l, 1])
        mkvb, mkv_t = _inproj(mem, norm_mem[l][None], wm, gfm, MEM_CHUNKS_P,
                              [(2 * X_WIDTH, BF16), ("T", 2 * X_WIDTH, F32)], N_MEM, MEM_TCHUNKS_P, N_MEM)
        mem_p.append(token_major(mkv_t, (2, X_HEADS, HEAD_DIM)))
        if l % 2 == 0:
            w, gf = _nsa_in_weights(w_in_nsa[j], nsa_qk_norm[j], x_qk_norm[l, 0])
            qn, kcvc, kvb, xqn, gates, rows_t, rows_wt, vt = _inproj(
                xp, gmix, w, gf, NSA_CHUNKS_P,
                [(TOK_WIDTH, BF16), (2 * kvw, F32), (6 * kvw, BF16), (X_WIDTH, BF16), (LANES, F32),
                 ("T", 4 * kvw, F32), ("T", 2 * kvw, F32), ("tiles", 2 * kvw, BF16)], TM, NSA_TCHUNKS_P, seq)
            cw = _cmp_weights(cmp_pe[j], cmp_w1[j], cmp_w2[j], nsa_qk_norm[j, 1])
            tok_p = _nsa_prompt_mix(qn, kcvc, kvb, vt, gates, cw, ptab, pq, batch)
            qs, rows_s, rows_ws, xqs, gates_s = _inproj(
                xs, gmix, w, gf, NSA_CHUNKS_S,
                [(TOK_WIDTH, F32), (4 * kvw, F32), (2 * kvw, F32), (X_WIDTH, F32), (LANES, F32)], TM)
            tok_s, wout = _nsa_sample(
                qs.reshape(dbatch, dseq, TOK_WIDTH), rows_s.reshape(dbatch, dseq, 4 * kvw),
                rows_ws.reshape(dbatch, dseq, 2 * kvw), gates_s.reshape(dbatch, dseq, LANES),
                win_cache, nsa_pool, j, page_table, cw, pq, ov_s, e_s, bc_s, bs_s, bw_s)
            nsa_p.append(token_major(rows_t, (4, NSA_G, HEAD_DIM)))
            nsa_s.append(rows_s.reshape(dbatch, dseq, 4, NSA_G, HEAD_DIM))
            win_p.append(token_major(rows_wt[:, :, seq - min(WINDOW, seq):], (2, NSA_G, HEAD_DIM)))
            win_s.append(token_major(wout, (2, NSA_G, HEAD_DIM)))
        else:
            w, gf = _sb_in_weights(w_in_sb[j], x_qk_norm[l, 0])
            q, kvb, xqn, rows_t = _inproj(
                xp, gmix, w, gf, SB_CHUNKS_P,
                [(TOK_WIDTH, BF16), (2 * TOK_WIDTH, BF16), (X_WIDTH, BF16), ("T", 2 * TOK_WIDTH, F32)],
                TM, SB_TCHUNKS_P, seq)
            tok_p = _sb_prompt(q, kvb, batch, LANES)
            qs, rows_s, xqs = _inproj(
                xs, gmix, w, gf, SB_CHUNKS_S, [(TOK_WIDTH, F32), (2 * TOK_WIDTH, F32), (X_WIDTH, F32)], TM)
            tok_s = _sb_sample(qs.reshape(dbatch, dseq, TOK_WIDTH), rows_s.reshape(dbatch, dseq, 2 * TOK_WIDTH),
                               sb_pool, j, page_table)
            sb_p.append(token_major(rows_t, (2, TOK_HEADS, HEAD_DIM)))
            sb_s.append(rows_s.reshape(dbatch, dseq, 2, TOK_HEADS, HEAD_DIM))
        memo_p = _memattn_prompt(xqn, mkvb, batch, TM)
        memo_s = _memattn_sample(xqs, mem_cache, l, dseq, MEM_NB)
        g_ffn = norm_ffn[l][None]
        xp = _out_ffn(xp, tok_p, memo_p, wo[l, :TOK_WIDTH], wo[l, TOK_WIDTH:], g_ffn, wfi[l], wfo[l], TM, TH)
        xs = _out_ffn(xs, tok_s.reshape(dbatch * dseq, TOK_WIDTH), memo_s, wo[l, :TOK_WIDTH], wo[l, TOK_WIDTH:],
                      g_ffn, wfi[l], wfo[l], TM, TH)
    return (xp.reshape(batch, seq, D_MODEL), xs.reshape(dbatch, dseq, D_MODEL), jnp.stack(nsa_p), jnp.stack(nsa_s),
            jnp.stack(win_p), jnp.stack(win_s), jnp.stack(sb_p), jnp.stack(sb_s), jnp.stack(mem_p))
```

```python
import functools
import math

import numpy as np
import jax
import jax.numpy as jnp
from jax import lax
from jax.experimental import pallas as pl
from jax.experimental.pallas import tpu as pltpu

F32 = jnp.float32
BF16 = jnp.bfloat16

D_MODEL = 1024
HEAD_DIM = 64
TOK_HEADS = 12
X_HEADS = 4
TOK_WIDTH = TOK_HEADS * HEAD_DIM
X_WIDTH = X_HEADS * HEAD_DIM
N_MEM = 256
NSA_G = 3
NSA_R = 4
CMP_BLOCK = 32
CMP_STRIDE = 16
CMP_HID = 128
SLC_BLOCK = 64
SLC_TOPN = 8
WINDOW = 512
FORCE_BONUS = 1e4
N_BUCKETS = 32
MAX_DIST = 1024
FFN_HIDDEN = 2816
RMS_EPS = 1e-6
PAGE = 128
SCALE = HEAD_DIM ** -0.5

LANES = 128
HALF = LANES // 2
VMEM_LIMIT = 56 * 1024 * 1024

EXP_ZERO = -104.0
NEG = -1e30


def _cp(sem, vmem=VMEM_LIMIT):
    return pltpu.CompilerParams(dimension_semantics=sem, vmem_limit_bytes=vmem)


def _dot(a, b):
    return jnp.dot(a, b, preferred_element_type=F32)


def _dot_t(a, b):
    return lax.dot_general(a, b, (((1,), (1,)), ((), ())), preferred_element_type=F32)


def _split_dot(x, w):
    hi = x.astype(BF16)
    lo = (x - hi.astype(F32)).astype(BF16)
    return _dot(hi, w) + _dot(lo, w)


def _rms_rows(x, g):
    ms = jnp.mean(x * x, axis=-1, keepdims=True)
    return x * lax.rsqrt(ms + RMS_EPS) * g


def _head_scale(h, gain, flag, s128):
    x2 = h * h
    parts = []
    for k in range(h.shape[1] // LANES):
        parts.append(_split_dot(x2[:, LANES * k:LANES * (k + 1)], s128))
    ssq = parts[0] if len(parts) == 1 else jnp.concatenate(parts, axis=1)
    r = lax.rsqrt(ssq * (1.0 / HEAD_DIM) + RMS_EPS)
    return h * jnp.where(flag > 0.5, r * gain, 1.0)


def _sigmoid(x):
    return 1.0 / (1.0 + jnp.exp(-x))


def _softplus(z):
    return jnp.maximum(z, 0.0) + jnp.log1p(jnp.exp(-jnp.abs(z)))


def _group_ones():
    i = np.arange(LANES)
    return jnp.asarray((i[:, None] // HALF == i[None, :] // HALF).astype(np.float32), BF16)


def _inproj_body(x_ref, g_ref, w_ref, gf_ref, s_ref, *rest, chunks, tchunks, n_carried):
    if tchunks:
        wt_ref, gcol_ref = rest[:2]
        out_refs = rest[2 + n_carried:]
    else:
        out_refs = rest[n_carried:]
    xn = _rms_rows(x_ref[...], g_ref[...]).astype(BF16)
    for c0, c1, norm, outs in chunks:
        h = _dot(xn, w_ref[:, c0:c1])
        if norm:
            h = _head_scale(h, gf_ref[0:1, c0:c1], gf_ref[1:2, c0:c1], s_ref[...])
        for oi, o0, kind in outs:
            o_ref = out_refs[oi]
            if kind == "sigmoid":
                o_ref[:, o0:o0 + (c1 - c0)] = _sigmoid(h)
            else:
                o_ref[:, o0:o0 + (c1 - c0)] = h.astype(o_ref.dtype)
    for r0, r1, norm, outs in tchunks:
        ht = _dot_t(wt_ref[r0:r1, :], xn)
        if norm:
            parts = []
            for k in range((r1 - r0) // HEAD_DIM):
                blk = ht[HEAD_DIM * k:HEAD_DIM * (k + 1)]
                ms = jnp.sum(blk * blk, axis=0, keepdims=True) * (1.0 / HEAD_DIM)
                parts.append(blk * lax.rsqrt(ms + RMS_EPS))
            ht = jnp.concatenate(parts, axis=0) * gcol_ref[r0:r1, :]
        for oi, o0, kind in outs:
            o_ref = out_refs[oi]
            if kind == "tiles":
                for j in range(ht.shape[1] // LANES):
                    o_ref[j, o0:o0 + (r1 - r0), :] = ht[:, LANES * j:LANES * (j + 1)].astype(o_ref.dtype)
            else:
                o_ref[o0:o0 + (r1 - r0), :] = ht.astype(o_ref.dtype)


def _inproj(x, g, w, gf, chunks, out_defs, tm, tchunks=(), rows_per_batch=None):
    m = x.shape[0]
    n = w.shape[1]
    assert m % tm == 0
    tb = rows_per_batch or m
    ntb = tb // tm
    nb = m // tb
    out_shape, out_specs, carried = [], [], {}
    for oi, d in enumerate(out_defs):
        if d[0] == "T" and len(d) == 4:
            nlayers, lidx, prev = d[3]
            out_shape.append(jax.ShapeDtypeStruct((nlayers, nb, d[1], tb), d[2]))
            out_specs.append(pl.BlockSpec((None, None, d[1], tm), lambda i, lidx=lidx: (lidx, i // ntb, 0, i % ntb)))
            if prev is not None:
                carried[oi] = prev
        elif d[0] == "T":
            out_shape.append(jax.ShapeDtypeStruct((nb, d[1], tb), d[2]))
            out_specs.append(pl.BlockSpec((None, d[1], tm), lambda i: (i // ntb, 0, i % ntb)))
        elif d[0] == "tiles":
            out_shape.append(jax.ShapeDtypeStruct((nb, tb // LANES, d[1], LANES), d[2]))
            out_specs.append(pl.BlockSpec((None, tm // LANES, d[1], LANES), lambda i: (i // ntb, i % ntb, 0, 0)))
        else:
            out_shape.append(jax.ShapeDtypeStruct((m, d[0]), d[1]))
            out_specs.append(pl.BlockSpec((tm, d[0]), lambda i: (i, 0)))
    in_specs = [
        pl.BlockSpec((tm, D_MODEL), lambda i: (i, 0)),
        pl.BlockSpec((1, D_MODEL), lambda i: (0, 0)),
        pl.BlockSpec((D_MODEL, n), lambda i: (0, 0)),
        pl.BlockSpec((2, n), lambda i: (0, 0)),
        pl.BlockSpec((LANES, LANES), lambda i: (0, 0)),
    ]
    args = [x, g, w, gf, _group_ones()]
    if tchunks:
        in_specs += [pl.BlockSpec((n, D_MODEL), lambda i: (0, 0)), pl.BlockSpec((n, 1), lambda i: (0, 0))]
        args += [w.T, gf[0].reshape(n, 1)]
    aliases = {}
    for oi, prev in carried.items():
        aliases[len(args)] = oi
        in_specs.append(pl.BlockSpec(memory_space=pl.ANY))
        args.append(prev)
    return pl.pallas_call(
        functools.partial(_inproj_body, chunks=chunks, tchunks=tchunks, n_carried=len(carried)),
        grid=(m // tm,),
        in_specs=in_specs,
        out_specs=out_specs,
        out_shape=out_shape,
        input_output_aliases=aliases,
        compiler_params=_cp(("parallel",)),
        name="inproj",
    )(*args)


NSA_N = 2304
NSA_CHUNKS = (
    (0, 384, True, ((0, 0, "cast"),)),
    (384, 768, True, ((0, 384, "cast"),)),
    (768, 1152, False, ((1, 0, "cast"), (3, 0, "cast"))),
    (1152, 1536, True, ((1, 384, "cast"), (3, 384, "cast"))),
    (1536, 1920, True, ((2, 0, "cast"), (3, 768, "cast"))),
    (1920, 2176, True, ((4, 0, "cast"),)),
    (2176, 2304, False, ((5, 0, "sigmoid"),)),
)
SB_N = 2560
SB_CHUNKS = (
    (0, 384, False, ((0, 0, "cast"),)),
    (384, 768, False, ((0, 384, "cast"),)),
    (768, 1152, False, ((1, 0, "cast"), (2, 0, "cast"))),
    (1152, 1536, False, ((1, 384, "cast"), (2, 384, "cast"))),
    (1536, 1920, False, ((1, 768, "cast"), (2, 768, "cast"))),
    (1920, 2304, False, ((1, 1152, "cast"), (2, 1152, "cast"))),
    (2304, 2560, True, ((3, 0, "cast"),)),
)
NSA_CHUNKS_P = (
    (0, 384, True, ((0, 0, "cast"),)),
    (384, 768, True, ((0, 384, "cast"),)),
    (768, 1152, False, ((1, 0, "cast"), (2, 0, "cast"))),
    (1152, 1536, True, ((2, 384, "cast"),)),
    (1536, 1920, True, ((2, 768, "cast"),)),
    (1920, 2176, True, ((3, 0, "cast"),)),
    (2176, 2304, False, ((4, 0, "sigmoid"),)),
)
NSA_TCHUNKS_P = (
    (768, 1152, False, ((5, 0, "rows"),)),
    (1152, 1344, True, ((5, 384, "rows"),)),
    (1344, 1536, False, ((5, 576, "rows"), (7, 0, "tiles"))),
    (1536, 1728, True, ((6, 0, "rows"),)),
    (1728, 1920, False, ((6, 192, "rows"), (7, 192, "tiles"))),
)
SB_CHUNKS_P = (
    (0, 384, False, ((0, 0, "cast"),)),
    (384, 768, False, ((0, 384, "cast"),)),
    (768, 1152, False, ((1, 0, "cast"),)),
    (1152, 1536, False, ((1, 384, "cast"),)),
    (1536, 1920, False, ((1, 768, "cast"),)),
    (1920, 2304, False, ((1, 1152, "cast"),)),
    (2304, 2560, True, ((2, 0, "cast"),)),
)
SB_TCHUNKS_P = (
    (768, 1152, False, ((3, 0, "rows"),)),
    (1152, 1536, False, ((3, 384, "rows"),)),
    (1536, 1920, False, ((3, 768, "rows"),)),
    (1920, 2304, False, ((3, 1152, "rows"),)),
)
MEM_CHUNKS_P = (
    (0, 256, True, ((0, 0, "cast"),)),
    (256, 512, False, ((0, 256, "cast"),)),
)
MEM_TCHUNKS_P = (
    (0, 256, True, ((1, 0, "rows"),)),
    (256, 512, False, ((1, 256, "rows"),)),
)
NSA_CHUNKS_S = (
    (0, 384, True, ((0, 0, "cast"),)),
    (384, 768, True, ((0, 384, "cast"),)),
    (768, 1152, False, ((1, 0, "cast"),)),
    (1152, 1536, True, ((1, 384, "cast"),)),
    (1536, 1920, True, ((2, 0, "cast"),)),
    (1920, 2176, True, ((3, 0, "cast"),)),
    (2176, 2304, False, ((4, 0, "sigmoid"),)),
)
SB_CHUNKS_S = (
    (0, 384, False, ((0, 0, "cast"),)),
    (384, 768, False, ((0, 384, "cast"),)),
    (768, 1152, False, ((1, 0, "cast"),)),
    (1152, 1536, False, ((1, 384, "cast"),)),
    (1536, 1920, False, ((1, 768, "cast"),)),
    (1920, 2304, False, ((1, 1152, "cast"),)),
    (2304, 2560, True, ((2, 0, "cast"),)),
)
MEM_CHUNKS = (
    (0, 256, True, ((0, 0, "cast"), (1, 0, "cast"))),
    (256, 512, False, ((0, 256, "cast"), (1, 256, "cast"))),
)


def _nsa_in_weights(w_in, qk_norm, xq_gain):
    o = TOK_WIDTH + 6 * NSA_G * HEAD_DIM
    w = jnp.concatenate([w_in[:, :o + X_WIDTH],
                         jnp.pad(w_in[:, o + X_WIDTH:], ((0, 0), (0, LANES - 3 * TOK_HEADS)))], axis=1)
    kvw = NSA_G * HEAD_DIM
    one = jnp.ones((kvw,), F32)
    zero = jnp.zeros((kvw,), F32)
    gain = jnp.concatenate([jnp.tile(qk_norm[0], TOK_HEADS), one, one, jnp.tile(qk_norm[2], NSA_G), one,
                            jnp.tile(qk_norm[3], NSA_G), one, jnp.tile(xq_gain, X_HEADS), jnp.ones((LANES,), F32)])
    flag = jnp.concatenate([jnp.ones((TOK_WIDTH,), F32), zero, zero, one, zero, one, zero,
                            jnp.ones((X_WIDTH,), F32), jnp.zeros((LANES,), F32)])
    return w.astype(BF16), jnp.stack([gain, flag])


def _sb_in_weights(w_in, xq_gain):
    gain = jnp.concatenate([jnp.ones((3 * TOK_WIDTH,), F32), jnp.tile(xq_gain, X_HEADS)])
    flag = jnp.concatenate([jnp.zeros((3 * TOK_WIDTH,), F32), jnp.ones((X_WIDTH,), F32)])
    return w_in.astype(BF16), jnp.stack([gain, flag])


def _mem_in_weights(w_kv, k_gain):
    gain = jnp.concatenate([jnp.tile(k_gain, X_HEADS), jnp.ones((X_WIDTH,), F32)])
    flag = jnp.concatenate([jnp.ones((X_WIDTH,), F32), jnp.zeros((X_WIDTH,), F32)])
    return w_kv.astype(BF16), jnp.stack([gain, flag])


def _ffn_body(x_ref, tok_ref, mem_ref, wot_ref, wom_ref, g_ref, wa_ref, wu_ref, wo_ref, o_ref, acc_ref, xn_ref):
    j = pl.program_id(1)

    @pl.when(j == 0)
    def _():
        xm = (x_ref[...] + _dot(tok_ref[...].astype(BF16), wot_ref[...])
              + _dot(mem_ref[...].astype(BF16), wom_ref[...]))
        acc_ref[...] = xm
        xn_ref[...] = _rms_rows(xm, g_ref[...]).astype(BF16)

    xn = xn_ref[...]
    a = _dot(xn, wa_ref[...])
    u = _dot(xn, wu_ref[...])
    hsw = (a * _sigmoid(a) * u).astype(BF16)
    acc_ref[...] += _dot(hsw, wo_ref[...])

    @pl.when(j == pl.num_programs(1) - 1)
    def _():
        o_ref[...] = acc_ref[...]


def _out_ffn(x, tok, mem, wo_tok, wo_mem, g, w_in, w_out, tm, th):
    m = x.shape[0]
    nh = FFN_HIDDEN // th
    return pl.pallas_call(
        _ffn_body,
        grid=(m // tm, nh),
        in_specs=[
            pl.BlockSpec((tm, D_MODEL), lambda i, j: (i, 0)),
            pl.BlockSpec((tm, TOK_WIDTH), lambda i, j: (i, 0)),
            pl.BlockSpec((tm, X_WIDTH), lambda i, j: (i, 0)),
            pl.BlockSpec((TOK_WIDTH, D_MODEL), lambda i, j: (0, 0)),
            pl.BlockSpec((X_WIDTH, D_MODEL), lambda i, j: (0, 0)),
            pl.BlockSpec((1, D_MODEL), lambda i, j: (0, 0)),
            pl.BlockSpec((D_MODEL, th), lambda i, j: (0, j)),
            pl.BlockSpec((D_MODEL, th), lambda i, j: (0, j + nh)),
            pl.BlockSpec((th, D_MODEL), lambda i, j: (j, 0)),
        ],
        out_specs=pl.BlockSpec((tm, D_MODEL), lambda i, j: (i, 0)),
        out_shape=jax.ShapeDtypeStruct((m, D_MODEL), F32),
        scratch_shapes=[pltpu.VMEM((tm, D_MODEL), F32), pltpu.VMEM((tm, D_MODEL), BF16)],
        compiler_params=_cp(("parallel", "arbitrary")),
        name="out_ffn",
    )(x, tok, mem, wo_tok, wo_mem, g, w_in, w_in, w_out)


def _mem_heads(q, kv, lane, feature_major=False):
    outs = []
    for c in range(X_WIDTH // LANES):
        q2 = q[:, LANES * c:LANES * (c + 1)]
        if feature_major:
            k2 = kv[LANES * c:LANES * (c + 1), :]
            v2 = kv[X_WIDTH + LANES * c:X_WIDTH + LANES * (c + 1), :]
        else:
            k2 = kv[:, LANES * c:LANES * (c + 1)]
            v2 = kv[:, X_WIDTH + LANES * c:X_WIDTH + LANES * (c + 1)]
        halves = []
        for half in range(2):
            sel = (lane < HALF) if half == 0 else (lane >= HALF)
            qm = jnp.where(sel, q2, jnp.zeros_like(q2))
            s = (_dot(qm, k2) if feature_major else _dot_t(qm, k2)) * SCALE
            e = jnp.exp(s - jnp.max(s, axis=-1, keepdims=True))
            p = (e / jnp.sum(e, axis=-1, keepdims=True)).astype(BF16)
            halves.append(_dot_t(p, v2) if feature_major else _dot(p, v2))
        outs.append(jnp.where(lane < HALF, halves[0], halves[1]))
    return jnp.concatenate(outs, axis=1)


def _memattn_p_body(q_ref, kv_ref, o_ref):
    lane = lax.broadcasted_iota(jnp.int32, (q_ref.shape[0], LANES), 1)
    o_ref[...] = _mem_heads(q_ref[...], kv_ref[...], lane).astype(o_ref.dtype)


def _memattn_prompt(xq, mkv, batch, tq):
    m = xq.shape[0]
    nt = m // batch // tq
    return pl.pallas_call(
        _memattn_p_body,
        grid=(batch, nt),
        in_specs=[pl.BlockSpec((tq, X_WIDTH), lambda b, i: (b * nt + i, 0)),
                  pl.BlockSpec((N_MEM, 2 * X_WIDTH), lambda b, i: (b, 0))],
        out_specs=pl.BlockSpec((tq, X_WIDTH), lambda b, i: (b * nt + i, 0)),
        out_shape=jax.ShapeDtypeStruct((m, X_WIDTH), BF16),
        compiler_params=_cp(("parallel", "parallel")),
        name="memattn_prompt",
    )(xq, mkv)


def _memattn_s_body(q_ref, kv_ref, o_ref, *, nb, t):
    lane = lax.broadcasted_iota(jnp.int32, (t, LANES), 1)
    for bi in range(nb):
        q = q_ref[bi * t:(bi + 1) * t, :].astype(BF16)
        o_ref[bi * t:(bi + 1) * t, :] = _mem_heads(q, kv_ref[bi].astype(BF16), lane, feature_major=True)


def _memattn_sample(xq, mkv, layer, t, nb):
    m = xq.shape[0]
    batch = m // t
    return pl.pallas_call(
        functools.partial(_memattn_s_body, nb=nb, t=t),
        grid=(batch // nb,),
        in_specs=[pl.BlockSpec((nb * t, X_WIDTH), lambda i: (i, 0)),
                  pl.BlockSpec((None, nb, 2 * X_WIDTH, N_MEM), lambda i: (layer, i, 0, 0))],
        out_specs=pl.BlockSpec((nb * t, X_WIDTH), lambda i: (i, 0)),
        out_shape=jax.ShapeDtypeStruct((m, X_WIDTH), F32),
        compiler_params=_cp(("parallel",)),
        name="memattn_sample",
    )(xq, mkv)


def _sb_block(z, strict, carry, u_tri):
    sp = _softplus(z)
    ls = -sp if strict is None else jnp.where(strict, -sp, 0.0)
    after = _split_dot(ls, u_tri) + carry
    a = jnp.exp(z - sp + after)
    if strict is not None:
        a = jnp.where(strict, a, 0.0)
    return a, carry + jnp.sum(ls, axis=1, keepdims=True)


def _upper_tri(n):
    r = lax.broadcasted_iota(jnp.int32, (n, n), 0)
    c = lax.broadcasted_iota(jnp.int32, (n, n), 1)
    return jnp.where(r > c, 1.0, 0.0).astype(BF16)


def _sb_prompt_body(q_ref, k_ref, v_ref, o_ref, qh_s, c_s, a_s, *, tq):
    i = pl.program_id(1)
    npair = TOK_WIDTH // LANES
    u_tri = _upper_tri(tq)
    lane = lax.broadcasted_iota(jnp.int32, (tq, LANES), 1)
    for pair in range(npair):
        q2 = q_ref[:, LANES * pair:LANES * (pair + 1)] * SCALE
        zero = jnp.zeros_like(q2)
        qh_s[2 * pair] = jnp.where(lane < HALF, q2, zero)
        qh_s[2 * pair + 1] = jnp.where(lane >= HALF, q2, zero)
    c_s[...] = jnp.zeros_like(c_s)
    a_s[...] = jnp.zeros_like(a_s)

    def key_block(kb, strict):
        ks = pl.multiple_of(kb * tq, tq)
        heads = range(TOK_HEADS)
        kblk = [k_ref[pl.ds(ks, tq), LANES * p:LANES * (p + 1)] for p in range(npair)]
        vblk = [v_ref[pl.ds(ks, tq), LANES * p:LANES * (p + 1)] for p in range(npair)]
        z = [_dot_t(qh_s[h], kblk[h // 2]) for h in heads]
        sp = [_softplus(z[h]) for h in heads]
        ls = [-sp[h] if strict is None else jnp.where(strict, -sp[h], 0.0) for h in heads]
        after = [_split_dot(ls[h], u_tri) for h in heads]
        old = [c_s[h] for h in heads]
        w = [jnp.exp(z[h] - sp[h] + after[h] + old[h]) for h in heads]
        if strict is not None:
            w = [jnp.where(strict, w[h], 0.0) for h in heads]
        pv = [_dot(w[h].astype(BF16), vblk[h // 2]) for h in heads]
        top = None
        for h in heads:
            carry = old[h] + jnp.sum(ls[h], axis=1, keepdims=True)
            c_s[h] = carry
            top = carry if top is None else jnp.maximum(top, carry)
        for p in range(npair):
            a_s[p] += jnp.where(lane < HALF, pv[2 * p], pv[2 * p + 1])
        return jnp.max(top)

    row = lax.broadcasted_iota(jnp.int32, (tq, tq), 0)
    col = lax.broadcasted_iota(jnp.int32, (tq, tq), 1)
    top0 = key_block(i, col < row)

    def cond(c):
        kb, top = c
        return jnp.logical_and(kb >= 0, top > EXP_ZERO)

    def body(c):
        kb, _ = c
        return kb - 1, key_block(kb, None)

    lax.while_loop(cond, body, (i - 1, top0))
    for pair in range(npair):
        o_ref[:, LANES * pair:LANES * (pair + 1)] = a_s[pair].astype(o_ref.dtype)


def _sb_prompt(q, kvb, batch, tq):
    assert tq == LANES
    m = q.shape[0]
    t = m // batch
    nt = t // tq
    return pl.pallas_call(
        functools.partial(_sb_prompt_body, tq=tq),
        grid=(batch, nt),
        in_specs=[pl.BlockSpec((tq, TOK_WIDTH), lambda b, i: (b * nt + i, 0)),
                  pl.BlockSpec((t, TOK_WIDTH), lambda b, i: (b, 0)),
                  pl.BlockSpec((t, TOK_WIDTH), lambda b, i: (b, 1))],
        out_specs=pl.BlockSpec((tq, TOK_WIDTH), lambda b, i: (b * nt + i, 0)),
        out_shape=jax.ShapeDtypeStruct((m, TOK_WIDTH), BF16),
        scratch_shapes=[pltpu.VMEM((TOK_HEADS, tq, LANES), BF16), pltpu.VMEM((TOK_HEADS, tq, LANES), F32),
                        pltpu.VMEM((TOK_WIDTH // LANES, tq, LANES), F32)],
        compiler_params=_cp(("parallel", "parallel")),
        name="sb_prompt",
    )(q, kvb, kvb)


def _sb_sample_body(pt_ref, q_ref, new_ref, pool_ref, o_ref, qbd_ref, acc_ref, carry_ref, blk_ref, buf_ref, sem_ref,
                    *, t, n_pages, layer):
    b = pl.program_id(0)
    nb = pl.num_programs(0)
    mine = 2 * (b % 2)

    def page_copy(seq, page, slot):
        return pltpu.make_async_copy(pool_ref.at[layer, pt_ref[seq, page]], buf_ref.at[slot], sem_ref.at[slot])

    @pl.when(b == 0)
    def _():
        page_copy(0, n_pages - 1, (n_pages - 1) % 2).start()
        page_copy(0, n_pages - 2, (n_pages - 2) % 2).start()

    @pl.when(b + 1 < nb)
    def _():
        page_copy(b + 1, n_pages - 1, 2 - mine + (n_pages - 1) % 2).start()
        page_copy(b + 1, n_pages - 2, 2 - mine + (n_pages - 2) % 2).start()

    rows = TOK_HEADS * t
    lane = lax.broadcasted_iota(jnp.int32, (rows, TOK_WIDTH), 1)
    rowi = lax.broadcasted_iota(jnp.int32, (rows, TOK_WIDTH), 0)
    own = (lane // HEAD_DIM) == (rowi // t)
    q = q_ref[0] * SCALE
    qbd_ref[...] = jnp.where(own, jnp.concatenate([q] * TOK_HEADS, axis=0), 0.0).astype(BF16)
    u_tri = _upper_tri(PAGE)

    blk_ref[...] = jnp.zeros_like(blk_ref)
    blk_ref[0:t, :] = new_ref[0]
    kcol = lax.broadcasted_iota(jnp.int32, (rows, PAGE), 1)
    trow = lax.broadcasted_iota(jnp.int32, (rows, PAGE), 0) % t
    strict = kcol < trow
    kv = blk_ref[...]
    z = _dot_t(qbd_ref[...], kv[:, :TOK_WIDTH].astype(BF16))
    w, carry = _sb_block(z, strict, jnp.zeros((rows, 1), F32), u_tri)
    acc_ref[...] = _dot(w.astype(BF16), kv[:, TOK_WIDTH:].astype(BF16))
    carry_ref[...] = carry

    def cond(c):
        page, more = c
        return jnp.logical_and(page >= 0, more)

    def body(c):
        page, _ = c
        slot = mine + page % 2
        page_copy(b, page, slot).wait()
        kt = buf_ref[slot, 0:TOK_WIDTH, :].astype(BF16)
        vt = buf_ref[slot, TOK_WIDTH:2 * TOK_WIDTH, :].astype(BF16)
        w, carry = _sb_block(_dot(qbd_ref[...], kt), None, carry_ref[...], u_tri)
        acc_ref[...] += _dot_t(w.astype(BF16), vt)
        carry_ref[...] = carry
        more = jnp.max(carry) > EXP_ZERO

        @pl.when(jnp.logical_and(more, page >= 2))
        def _():
            page_copy(b, page - 2, slot).start()
        return page - 1, more

    left, _ = lax.while_loop(cond, body, (n_pages - 1, True))

    @pl.when(left >= 0)
    def _():
        page_copy(b, left, mine + left % 2).wait()

    acc = jnp.where(own, acc_ref[...], 0.0)
    out = acc[0:t]
    for h in range(1, TOK_HEADS):
        out = out + acc[h * t:(h + 1) * t]
    o_ref[0] = out


def _sb_sample(q, new_rows, pool, layer, page_table):
    batch, t, _ = q.shape
    n_pages = page_table.shape[1]
    rows = TOK_HEADS * t
    assert n_pages >= 2
    grid_spec = pltpu.PrefetchScalarGridSpec(
        num_scalar_prefetch=1,
        grid=(batch,),
        in_specs=[pl.BlockSpec((1, t, TOK_WIDTH), lambda b, pt: (b, 0, 0)),
                  pl.BlockSpec((1, t, 2 * TOK_WIDTH), lambda b, pt: (b, 0, 0)),
                  pl.BlockSpec(memory_space=pl.ANY)],
        out_specs=pl.BlockSpec((1, t, TOK_WIDTH), lambda b, pt: (b, 0, 0)),
        scratch_shapes=[pltpu.VMEM((rows, TOK_WIDTH), BF16), pltpu.VMEM((rows, TOK_WIDTH), F32),
                        pltpu.VMEM((rows, 1), F32), pltpu.VMEM((PAGE, 2 * TOK_WIDTH), F32),
                        pltpu.VMEM((4, 2 * TOK_WIDTH, PAGE), F32), pltpu.SemaphoreType.DMA((4,))],
    )
    return pl.pallas_call(
        functools.partial(_sb_sample_body, t=t, n_pages=n_pages, layer=layer),
        grid_spec=grid_spec,
        out_shape=jax.ShapeDtypeStruct((batch, t, TOK_WIDTH), F32),
        compiler_params=_cp(("arbitrary",)),
        name="sb_sample",
    )(page_table, q, new_rows, pool)


def _masked_softmax(s, mask, axis=-1):
    s = jnp.where(mask, s, -jnp.inf)
    m = jnp.max(s, axis=axis, keepdims=True)
    m = jnp.where(m == -jnp.inf, 0.0, m)
    e = jnp.exp(s - m)
    return e / jnp.maximum(jnp.sum(e, axis=axis, keepdims=True), 1e-30)


def _compress(load_chunks, nch, wc_ref, w1_ref, pe_ref, w2p_ref, gf_ref, s128):
    acc = [jnp.zeros((nch, 4 * CMP_HID), F32) for _ in range(3)]
    for p in range(CMP_STRIDE):
        for c in range(3):
            acc[c] = acc[c] + _dot(load_chunks(p, c).astype(BF16), wc_ref[c, p])
    bias = [_dot(pe_ref[tt], w1_ref[tt])[0:1] for tt in range(2)]
    outs = []
    for c, (ta, tb) in enumerate(((0, 0), (0, 1), (1, 1))):
        hid = []
        for half, tt in ((0, ta), (1, tb)):
            first = acc[c][:, 2 * CMP_HID * half:2 * CMP_HID * half + CMP_HID]
            second = acc[c][:, 2 * CMP_HID * half + CMP_HID:2 * CMP_HID * (half + 1)]
            pre = first + pltpu.roll(second, nch - 1, 0) + bias[tt]
            hid.append((pre * _sigmoid(pre)).astype(BF16))
        outs.append(_dot(hid[0], w2p_ref[ta, 0]) + _dot(hid[1], w2p_ref[tb, 1]))
    ckv = jnp.concatenate(outs, axis=1)
    return _head_scale(ckv, gf_ref[0:1, :], gf_ref[1:2, :], s128)


def _select_topn(score, topn, axis=1):
    lane = lax.broadcasted_iota(jnp.int32, score.shape, axis)
    sel = jnp.zeros(score.shape, F32)
    cur = score
    for _ in range(topn):
        mx = jnp.max(cur, axis=axis, keepdims=True)
        is_max = jnp.logical_and(cur == mx, mx > -jnp.inf)
        idx = jnp.min(jnp.where(is_max, lane, score.shape[axis]), axis=axis, keepdims=True)
        pick = lane == idx
        sel = jnp.where(pick, 1.0, sel)
        cur = jnp.where(pick, -jnp.inf, cur)
    return sel


def _stack_heads(q, pq_ref, g, t, dtype=BF16):
    return jnp.concatenate([_dot(q, pq_ref[NSA_R * g + r])[0:t] for r in range(NSA_R)], axis=0).astype(dtype)


def _place_heads(o, g, t, gate_of, lane):
    vh = (g + 1) % 2
    tiles = []
    for rp in range(2):
        outs = []
        for rr in range(2):
            r = 2 * rp + rr
            o_r = o[r * t:(r + 1) * t]
            if vh != rr:
                o_r = pltpu.roll(o_r, HALF, 1)
            outs.append(o_r * gate_of(NSA_R * g + r))
        tiles.append(jnp.where(lane < HALF, outs[0], outs[1]))
    return tiles


def _nsa_compress_body(r0_ref, r1_ref, r2_ref, wc_ref, w1_ref, pe_ref, w2p_ref, gf_ref, s_ref, o_ref, ot_ref, *, nch):
    tiles = (r0_ref, r1_ref, r2_ref)

    def load(p, c):
        return tiles[c][pl.ds(p, nch, stride=CMP_STRIDE), :]
    ckv = _compress(load, nch, wc_ref, w1_ref, pe_ref, w2p_ref, gf_ref, s_ref[...])
    o_ref[...] = ckv
    ot_ref[...] = ckv.T


def _cmp_const_specs():
    return [pl.BlockSpec((3, CMP_STRIDE, LANES, 4 * CMP_HID), lambda *a: (0, 0, 0, 0)),
            pl.BlockSpec((2, CMP_BLOCK * HEAD_DIM, CMP_HID), lambda *a: (0, 0, 0)),
            pl.BlockSpec((2, 8, CMP_BLOCK * HEAD_DIM), lambda *a: (0, 0, 0)),
            pl.BlockSpec((2, 2, CMP_HID, LANES), lambda *a: (0, 0, 0, 0)),
            pl.BlockSpec((2, 3 * LANES), lambda *a: (0, 0)),
            pl.BlockSpec((LANES, LANES), lambda *a: (0, 0))]


def _nsa_compress(rows, cw, batch):
    m = rows.shape[0]
    t = m // batch
    nch = t // CMP_STRIDE
    return pl.pallas_call(
        functools.partial(_nsa_compress_body, nch=nch),
        grid=(batch,),
        in_specs=[pl.BlockSpec((t, LANES), lambda b, c=c: (b, c)) for c in range(3)] + _cmp_const_specs(),
        out_specs=[pl.BlockSpec((nch, 3 * LANES), lambda b: (b, 0)),
                   pl.BlockSpec((None, 3 * LANES, nch), lambda b: (b, 0, 0))],
        out_shape=[jax.ShapeDtypeStruct((batch * nch, 3 * LANES), F32),
                   jax.ShapeDtypeStruct((batch, 3 * LANES, nch), F32)],
        compiler_params=_cp(("parallel",)),
        name="nsa_compress",
    )(rows, rows, rows, *cw, _group_ones())


def _nsa_cmp_body(q_ref, ckv_ref, ckvt_ref, bias_ref, ovt_ref, pq_ref, g_ref, sel_ref, part_ref, *, tq, ncp, topn):
    i = pl.program_id(1)
    q0 = i * tq
    q = q_ref[...]
    ck = ckv_ref[...].astype(BF16)
    cvt = ckvt_ref[...].astype(BF16)
    rows4 = NSA_R * tq
    nrow = lax.broadcasted_iota(jnp.int32, (ncp, rows4), 0)
    tpos = q0 + lax.broadcasted_iota(jnp.int32, (ncp, rows4), 1) % tq
    mask_c = (CMP_STRIDE * nrow + CMP_BLOCK - 1) <= tpos
    lane = lax.broadcasted_iota(jnp.int32, (tq, LANES), 1)
    blk = lax.broadcasted_iota(jnp.int32, (LANES, NSA_G * tq), 0)
    tq_pos = q0 + lax.broadcasted_iota(jnp.int32, (LANES, NSA_G * tq), 1) % tq
    valid = blk * SLC_BLOCK <= tq_pos
    curb = tq_pos // SLC_BLOCK
    forced = (blk == 0) | (blk == curb) | (blk == curb - 1)
    gates = g_ref[...]
    groups = range(NSA_G)
    qt = [(_stack_heads(q, pq_ref, g, tq, F32) * SCALE).T.astype(BF16) for g in groups]
    s = [_dot(ck[:, LANES * (g // 2):LANES * (g // 2 + 1)], qt[g]) + bias_ref[0, g] for g in groups]
    pb = [_masked_softmax(s[g], mask_c, axis=0).astype(BF16) for g in groups]
    kvw = NSA_G * HEAD_DIM
    o_t = [_dot(cvt[kvw + HEAD_DIM * g:kvw + HEAD_DIM * (g + 1), :], pb[g]) for g in groups]
    imp4 = [_dot(ovt_ref[...], pb[g]) for g in groups]
    imp = jnp.concatenate([imp4[g][:, 0:tq] + imp4[g][:, tq:2 * tq] + imp4[g][:, 2 * tq:3 * tq]
                           + imp4[g][:, 3 * tq:4 * tq] for g in groups], axis=1)
    score = jnp.where(valid, imp + jnp.where(forced, FORCE_BONUS, 0.0), -jnp.inf)
    sel = _select_topn(score, topn, axis=0).astype(sel_ref.dtype)
    for g in groups:
        sel_ref[0, g] = sel[:, g * tq:(g + 1) * tq]
        for rp in range(2):
            h0 = NSA_R * g + 2 * rp
            pair = jnp.concatenate([o_t[g][:, 2 * rp * tq:(2 * rp + 1) * tq],
                                    o_t[g][:, (2 * rp + 1) * tq:(2 * rp + 2) * tq]], axis=0)
            gate = jnp.where(lane < HALF, gates[:, h0:h0 + 1], gates[:, h0 + 1:h0 + 2])
            k = 2 * g + rp
            part_ref[:, LANES * k:LANES * (k + 1)] = pair.T * gate


def _nsa_cmp(qn, ckv, ckvt, biasc, ovt, pq, gates, batch, tq):
    m = qn.shape[0]
    t = m // batch
    nt = t // tq
    ncp = t // CMP_STRIDE
    topn = min(SLC_TOPN, -(-t // SLC_BLOCK))
    return pl.pallas_call(
        functools.partial(_nsa_cmp_body, tq=tq, ncp=ncp, topn=topn),
        grid=(batch, nt),
        in_specs=[pl.BlockSpec((tq, TOK_WIDTH), lambda b, i: (b * nt + i, 0)),
                  pl.BlockSpec((ncp, 3 * LANES), lambda b, i: (b, 0)),
                  pl.BlockSpec((None, 3 * LANES, ncp), lambda b, i: (b, 0, 0)),
                  pl.BlockSpec((1, NSA_G, ncp, NSA_R * tq), lambda b, i: (i, 0, 0, 0)),
                  pl.BlockSpec((LANES, ncp), lambda b, i: (0, 0)),
                  pl.BlockSpec((TOK_HEADS, TOK_WIDTH, LANES), lambda b, i: (0, 0, 0)),
                  pl.BlockSpec((tq, LANES), lambda b, i: (b * nt + i, 0))],
        out_specs=[pl.BlockSpec((1, NSA_G, LANES, tq), lambda b, i: (b * nt + i, 0, 0, 0)),
                   pl.BlockSpec((tq, TOK_WIDTH), lambda b, i: (b * nt + i, 0))],
        out_shape=[jax.ShapeDtypeStruct((batch * nt, NSA_G, LANES, tq), BF16),
                   jax.ShapeDtypeStruct((m, TOK_WIDTH), F32)],
        compiler_params=_cp(("parallel", "parallel")),
        name="nsa_cmp_select",
    )(qn, ckv, ckvt, biasc, ovt, pq, gates)


N_NEAR = 8
PAIR = 2


def _nsa_slc_body(q_ref, kvb_ref, vt_ref, sel_ref, part_ref, g_ref, bt_ref, pq_ref, o_ref,
                  qt_s, m_s, acc_s, *, tq):
    i = pl.program_id(1)
    rows4 = NSA_R * tq
    q = q_ref[...]
    for g in range(NSA_G):
        qt_s[g, 0:LANES, :] = (_stack_heads(q, pq_ref, g, tq, F32) * SCALE).T.astype(BF16)
        unselected = ((sel_ref[0, g].astype(F32) - 1.0) * (-NEG)).astype(BF16)
        qt_s[g, LANES:2 * LANES, :] = jnp.concatenate([unselected] * NSA_R, axis=1)
    ones_rows = {n: jnp.ones((2 * 8, n * LANES), BF16) for n in (1, PAIR)}
    kk = lax.broadcasted_iota(jnp.int32, (LANES, rows4), 0)
    tt = lax.broadcasted_iota(jnp.int32, (LANES, rows4), 1) % tq
    causal = jnp.where(kk <= tt, 0.0, NEG)
    oldest = jnp.where(kk > tt, 0.0, NEG)
    khalf = {n: lax.broadcasted_iota(jnp.int32, (n * LANES, LANES), 0) // SLC_BLOCK for n in (1, PAIR)}
    jcol = {n: lax.broadcasted_iota(jnp.int32, (n * LANES, LANES), 1) for n in (1, PAIR)}
    lane = lax.broadcasted_iota(jnp.int32, (tq, LANES), 1)
    gates = g_ref[...]
    tok = [part_ref[:, LANES * k:LANES * (k + 1)] for k in range(2 * NSA_G)]

    def tile_step(kt, branch, extra, ntile=1):
        koff = 3 * LANES * branch
        nk = ntile * LANES
        ks = pl.multiple_of(kt * LANES, LANES)
        groups = range(NSA_G)
        kblk = [kvb_ref[pl.ds(ks, nk), pl.ds(LANES * ((6 + g) // 2) + koff, LANES)] for g in groups]
        vt = [jnp.concatenate(
            [jnp.concatenate([vt_ref[kt + u, pl.ds(HEAD_DIM * (NSA_G * branch + g), HEAD_DIM), :]
                              for u in range(ntile)], axis=1), ones_rows[ntile]], axis=0) for g in groups]
        bias = [jnp.concatenate([bt_ref[jnp.minimum(i - kt - u, N_NEAR), g] for u in range(ntile)], axis=0)
                for g in groups]
        if branch == 0:
            e_mat = jnp.where(jcol[ntile] == 2 * kt + khalf[ntile], 1.0, 0.0).astype(BF16)
            s = [_dot(jnp.concatenate([kblk[g], e_mat], axis=1), qt_s[g]) + bias[g] for g in groups]
        else:
            s = [_dot(kblk[g], qt_s[g, 0:LANES, :]) + bias[g] for g in groups]
        if extra is not None:
            s = [s[g] + extra for g in groups]
        m_old = [m_s[g] for g in groups]
        m_new = [jnp.maximum(m_old[g], jnp.max(s[g], axis=0, keepdims=True)) for g in groups]
        p = [jnp.exp(s[g] - m_new[g]).astype(BF16) for g in groups]
        alpha = [jnp.exp(m_old[g] - m_new[g]) for g in groups]
        pv = [_dot(vt[g], p[g]) for g in groups]
        for g in groups:
            acc_s[g] = alpha[g] * acc_s[g] + pv[g]
            m_s[g] = m_new[g]

    for branch in range(2):
        for g in range(NSA_G):
            m_s[g] = jnp.full((1, rows4), NEG, F32)
            acc_s[g] = jnp.zeros(acc_s.shape[1:], F32)

        if branch == 0:
            lo = 0
        else:
            nback = WINDOW // LANES

            @pl.when(i >= nback)
            def _():
                tile_step(i - nback, 1, oldest)
            lo = jnp.maximum(i - nback + 1, 0)

        def body(pp, carry, branch=branch, lo=lo):
            tile_step(lo + PAIR * pp, branch, None, PAIR)
            return carry
        lax.fori_loop(0, (i - lo) // PAIR, body, 0)

        @pl.when((i - lo) % PAIR == 1)
        def _(branch=branch):
            tile_step(i - 1, branch, None)
        tile_step(i, branch, causal)
        base = TOK_HEADS * (branch + 1)
        for g in range(NSA_G):
            acc = acc_s[g]
            o_t = acc[0:HEAD_DIM] / jnp.maximum(acc[HEAD_DIM:HEAD_DIM + 1], 1e-30)
            for rp in range(2):
                h0 = NSA_R * g + 2 * rp
                pair = jnp.concatenate([o_t[:, 2 * rp * tq:(2 * rp + 1) * tq],
                                        o_t[:, (2 * rp + 1) * tq:(2 * rp + 2) * tq]], axis=0)
                gate = jnp.where(lane < HALF, gates[:, base + h0:base + h0 + 1],
                                 gates[:, base + h0 + 1:base + h0 + 2])
                tok[2 * g + rp] = tok[2 * g + rp] + pair.T * gate
    for k in range(2 * NSA_G):
        o_ref[:, LANES * k:LANES * (k + 1)] = tok[k].astype(o_ref.dtype)


def _nsa_slc_win(qn, kvb, vt, sel, part, gates, bt, pq, batch):
    tq = LANES
    m = qn.shape[0]
    t = m // batch
    nt = t // tq
    rows4 = NSA_R * tq
    return pl.pallas_call(
        functools.partial(_nsa_slc_body, tq=tq),
        grid=(batch, nt),
        in_specs=[pl.BlockSpec((tq, TOK_WIDTH), lambda b, i: (b * nt + i, 0)),
                  pl.BlockSpec((t, 9 * LANES), lambda b, i: (b, 0)),
                  pl.BlockSpec((None, nt, 2 * NSA_G * HEAD_DIM, LANES), lambda b, i: (b, 0, 0, 0)),
                  pl.BlockSpec((1, NSA_G, LANES, tq), lambda b, i: (b * nt + i, 0, 0, 0)),
                  pl.BlockSpec((tq, TOK_WIDTH), lambda b, i: (b * nt + i, 0)),
                  pl.BlockSpec((tq, LANES), lambda b, i: (b * nt + i, 0)),
                  pl.BlockSpec((N_NEAR + 1, NSA_G, LANES, rows4), lambda b, i: (0, 0, 0, 0)),
                  pl.BlockSpec((TOK_HEADS, TOK_WIDTH, LANES), lambda b, i: (0, 0, 0))],
        out_specs=pl.BlockSpec((tq, TOK_WIDTH), lambda b, i: (b * nt + i, 0)),
        out_shape=jax.ShapeDtypeStruct((m, TOK_WIDTH), BF16),
        scratch_shapes=[pltpu.VMEM((NSA_G, 2 * LANES, rows4), BF16), pltpu.VMEM((NSA_G, 1, rows4), F32),
                        pltpu.VMEM((NSA_G, HEAD_DIM + 16, rows4), F32)],
        compiler_params=_cp(("parallel", "parallel")),
        name="nsa_slc_win",
    )(qn, kvb, vt, sel, part, gates, bt, pq)


def _nsa_sample_body(pt_ref, q_ref, new_ref, neww_ref, g_ref, win_ref, *rest, t, n_pages):
    page_refs = rest[:n_pages]
    (wc_ref, w1_ref, pe_ref, w2p_ref, gf_ref, s_ref, pq_ref, ov_ref, e_ref, bc_ref, bs_ref, bw_ref,
     tok_ref, wout_ref, x_buf, n_buf, nw_buf) = rest[n_pages:]
    del pt_ref
    past = n_pages * PAGE
    slen = past + PAGE
    nch = past // CMP_STRIDE
    n_cmp = nch - 1
    nwin = win_ref.shape[1]
    wlen = nwin + PAGE
    rows4 = NSA_R * t

    for c in range(3):
        for p in range(n_pages):
            x_buf[c, p * PAGE:(p + 1) * PAGE, :] = page_refs[p][LANES * c:LANES * (c + 1), :].T
    n_buf[...] = jnp.zeros_like(n_buf)
    n_buf[0:t, :] = new_ref[0]
    nw_buf[...] = jnp.zeros_like(nw_buf)
    nw_buf[0:t, :] = neww_ref[0]
    for c in range(3):
        full = jnp.concatenate([win_ref[LANES * c:LANES * (c + 1), :], nw_buf[:, LANES * c:LANES * (c + 1)].T], axis=1)
        wout_ref[LANES * c:LANES * (c + 1), :] = pltpu.roll(full, wlen - t, 1)[:, 0:nwin]

    def load(p, c):
        return x_buf[c, pl.ds(p, nch, stride=CMP_STRIDE), :]
    ckv = _compress(load, nch, wc_ref, w1_ref, pe_ref, w2p_ref, gf_ref, s_ref[...]).astype(BF16)

    q = jnp.concatenate([q_ref[0], jnp.zeros((8, TOK_WIDTH), F32)], axis=0).astype(BF16)
    new_tiles = [n_buf[:, LANES * k:LANES * (k + 1)].astype(BF16) for k in (3, 4, 5)]
    neww_tiles = [nw_buf[:, LANES * k:LANES * (k + 1)].astype(BF16) for k in (0, 1, 2)]
    win_tiles = [win_ref[LANES * k:LANES * (k + 1), :].astype(BF16) for k in (0, 1, 2)]
    tile_cache = {}

    def page_tile(k, p):
        if (k, p) not in tile_cache:
            tile_cache[(k, p)] = page_refs[p][LANES * k:LANES * (k + 1), :].astype(BF16)
        return tile_cache[(k, p)]
    gates = g_ref[0]
    lane = lax.broadcasted_iota(jnp.int32, (t, LANES), 1)
    lane3 = lax.broadcasted_iota(jnp.int32, (NSA_G * t, LANES), 1)
    tpos = past + lax.broadcasted_iota(jnp.int32, (NSA_G * t, LANES), 0) % t
    valid = lane3 * SLC_BLOCK <= tpos
    curb = tpos // SLC_BLOCK
    forced = (lane3 == 0) | (lane3 == curb) | (lane3 == curb - 1)
    mask_c = lax.broadcasted_iota(jnp.int32, (rows4, nch), 1) < n_cmp
    trow_s = lax.broadcasted_iota(jnp.int32, (rows4, slen), 0) % t
    kcol = lax.broadcasted_iota(jnp.int32, (rows4, slen), 1)
    vis_s = kcol <= past + trow_s
    trow_w = lax.broadcasted_iota(jnp.int32, (rows4, wlen), 0) % t
    wcol = lax.broadcasted_iota(jnp.int32, (rows4, wlen), 1)
    vis_w = jnp.logical_and(wcol > trow_w, wcol <= nwin + trow_w)

    groups = range(NSA_G)
    ki = (0, 0, 1)
    vi = (1, 2, 2)
    qg = [_stack_heads(q, pq_ref, g, t) * SCALE for g in groups]
    sc = [_dot_t(qg[g], ckv[:, LANES * (g // 2):LANES * (g // 2 + 1)]) + bc_ref[g] for g in groups]
    pc = [_masked_softmax(sc[g], mask_c).astype(BF16) for g in groups]
    o_c = [_dot(pc[g], ckv[:, LANES * ((3 + g) // 2):LANES * ((3 + g) // 2 + 1)]) for g in groups]
    imp4 = [_dot(pc[g], ov_ref[...]) for g in groups]
    imp = jnp.concatenate([imp4[g][0:t] + imp4[g][t:2 * t] + imp4[g][2 * t:3 * t] + imp4[g][3 * t:4 * t]
                           for g in groups], axis=0)
    score = jnp.where(valid, imp + jnp.where(forced, FORCE_BONUS, 0.0), -jnp.inf)
    sel = _select_topn(score, SLC_TOPN).astype(BF16)
    ss = [jnp.concatenate([_dot(qg[g], page_tile(3 + ki[g], p)) for p in range(n_pages)]
                          + [_dot_t(qg[g], new_tiles[ki[g]])], axis=1) + bs_ref[g] for g in groups]
    selx = _dot(sel, e_ref[...])
    ps = [_masked_softmax(ss[g], jnp.logical_and(
        vis_s, jnp.concatenate([selx[g * t:(g + 1) * t]] * NSA_R, axis=0) > 0.5)).astype(BF16) for g in groups]
    o_s = [_dot(ps[g][:, past:slen], new_tiles[vi[g]]) for g in groups]
    for p in range(n_pages):
        o_s = [o_s[g] + _dot_t(ps[g][:, p * PAGE:(p + 1) * PAGE], page_tile(3 + vi[g], p)) for g in groups]
    sw = [jnp.concatenate([_dot(qg[g], win_tiles[ki[g]]), _dot_t(qg[g], neww_tiles[ki[g]])], axis=1) + bw_ref[g]
          for g in groups]
    pw = [_masked_softmax(sw[g], vis_w).astype(BF16) for g in groups]
    o_w = [_dot_t(pw[g][:, 0:nwin], win_tiles[vi[g]]) + _dot(pw[g][:, nwin:wlen], neww_tiles[vi[g]]) for g in groups]

    def gate_col(c, g):
        return jnp.concatenate([gates[:, TOK_HEADS * c + NSA_R * g + r:TOK_HEADS * c + NSA_R * g + r + 1]
                                for r in range(NSA_R)], axis=0)
    for g in groups:
        o = gate_col(0, g) * o_c[g] + gate_col(1, g) * o_s[g] + gate_col(2, g) * o_w[g]
        tiles = _place_heads(o, g, t, lambda h: 1.0, lane)
        for rp in range(2):
            k = 2 * g + rp
            tok_ref[0, :, LANES * k:LANES * (k + 1)] = tiles[rp]


def _nsa_sample(q, new_rows, new_w, gates, win_cache, pool, layer, page_table, cw, pq, ov, e_mat, bc, bs, bw):
    batch, t, _ = q.shape
    n_pages = page_table.shape[1]
    past = n_pages * PAGE
    nwin = win_cache.shape[3]
    nch = past // CMP_STRIDE
    rows4 = NSA_R * t
    page_specs = [
        pl.BlockSpec((None, None, TOK_WIDTH, PAGE), lambda b, pt, p=p: (layer, pt[b, p], 0, 0))
        for p in range(n_pages)
    ]
    const3 = lambda b, pt: (0, 0, 0)
    const2 = lambda b, pt: (0, 0)
    grid_spec = pltpu.PrefetchScalarGridSpec(
        num_scalar_prefetch=1,
        grid=(batch,),
        in_specs=[pl.BlockSpec((1, t, TOK_WIDTH), lambda b, pt: (b, 0, 0)),
                  pl.BlockSpec((1, t, TOK_WIDTH), lambda b, pt: (b, 0, 0)),
                  pl.BlockSpec((1, t, 3 * LANES), lambda b, pt: (b, 0, 0)),
                  pl.BlockSpec((1, t, LANES), lambda b, pt: (b, 0, 0)),
                  pl.BlockSpec((None, None, 3 * LANES, nwin), lambda b, pt: (layer, b, 0, 0))]
        + page_specs + _cmp_const_specs()
        + [pl.BlockSpec((TOK_HEADS, TOK_WIDTH, LANES), const3),
           pl.BlockSpec((nch, LANES), const2),
           pl.BlockSpec((LANES, past + PAGE), const2),
           pl.BlockSpec((NSA_G, rows4, nch), const3),
           pl.BlockSpec((NSA_G, rows4, past + PAGE), const3),
           pl.BlockSpec((NSA_G, rows4, nwin + PAGE), const3)],
        out_specs=[pl.BlockSpec((1, t, TOK_WIDTH), lambda b, pt: (b, 0, 0)),
                   pl.BlockSpec((None, 3 * LANES, nwin), lambda b, pt: (b, 0, 0))],
        scratch_shapes=[pltpu.VMEM((3, past, LANES), F32), pltpu.VMEM((PAGE, TOK_WIDTH), F32),
                        pltpu.VMEM((PAGE, 3 * LANES), F32)],
    )
    return pl.pallas_call(
        functools.partial(_nsa_sample_body, t=t, n_pages=n_pages),
        grid_spec=grid_spec,
        out_shape=[jax.ShapeDtypeStruct((batch, t, TOK_WIDTH), F32),
                   jax.ShapeDtypeStruct((batch, 3 * LANES, nwin), F32)],
        compiler_params=_cp(("arbitrary",)),
        name="nsa_sample",
    )(page_table, q, new_rows, new_w, gates, win_cache, *([pool] * n_pages), *cw, _group_ones(),
      pq, ov, e_mat, bc, bs, bw)


def _rel_bucket(dist):
    exact = N_BUCKETS // 2
    d = jnp.maximum(dist, 0)
    far = exact + (jnp.log(jnp.maximum(d, 1).astype(F32) / exact)
                   / math.log(MAX_DIST / exact) * (N_BUCKETS - exact)).astype(jnp.int32)
    return jnp.where(d < exact, d, jnp.minimum(far, N_BUCKETS - 1))


NEG_PAD = 1024


def _dist_table(rel_bias, n):
    tab = rel_bias[_rel_bucket(jnp.arange(n, dtype=jnp.int32))].T.astype(F32)
    return jnp.pad(tab, ((0, 0), (NEG_PAD, 0)))


def _skew(v, nrows, step):
    p = v.shape[-1]
    flat = jnp.tile(v, (1,) * (v.ndim - 1) + (nrows,))[..., :nrows * (p - step)]
    return flat.reshape(v.shape[:-1] + (nrows, p - step))


def _toeplitz(tab, c0, nrows, ncols):
    lo = c0 - ncols + 1 + NEG_PAD
    hi = c0 + nrows + NEG_PAD
    u = jnp.flip(tab[:, lo:hi], axis=1)
    v = jnp.roll(u, -(nrows - 1), axis=1)
    return _skew(v, nrows, 1)[..., :ncols]


def _group_rows(m):
    return m.reshape(NSA_G, NSA_R * m.shape[1], m.shape[2])


def _head_placement():
    pq = np.zeros((TOK_HEADS, TOK_WIDTH, LANES), np.float32)
    d = np.arange(HEAD_DIM)
    for h in range(TOK_HEADS):
        pq[h, HEAD_DIM * h + d, HALF * ((h // NSA_R) % 2) + d] = 1.0
    return jnp.asarray(pq, BF16)


def _overlap(nrows, n_cmp, n_slc):
    cs = np.arange(nrows)[:, None] * CMP_STRIDE
    ss = np.arange(LANES)[None, :] * SLC_BLOCK
    ov = (cs < ss + SLC_BLOCK) & (cs + CMP_BLOCK > ss)
    ov &= (np.arange(nrows)[:, None] < n_cmp) & (np.arange(LANES)[None, :] < n_slc)
    return jnp.asarray(ov.astype(np.float32), BF16)


def _cmp_weights(pe, w1, w2, kn_cmp):
    w1b = w1.astype(BF16)
    w1p = w1b.reshape(2, 2, CMP_STRIDE, HEAD_DIM, CMP_HID)
    cat = jnp.concatenate([w1p[:, 0], w1p[:, 1]], axis=-1)
    z = jnp.zeros_like(cat[0])

    def pair(ta, tb):
        top = jnp.concatenate([cat[ta], z], axis=-1)
        bot = jnp.concatenate([z, cat[tb]], axis=-1)
        return jnp.concatenate([top, bot], axis=1)
    wc = jnp.stack([pair(0, 0), pair(0, 1), pair(1, 1)])
    pe8 = jnp.pad(pe.reshape(2, 1, CMP_BLOCK * HEAD_DIM), ((0, 0), (0, 7), (0, 0))).astype(BF16)
    w2b = w2.astype(BF16)
    zz = jnp.zeros_like(w2b)
    w2p = jnp.stack([jnp.concatenate([w2b, zz], axis=-1), jnp.concatenate([zz, w2b], axis=-1)], axis=1)
    kvw = NSA_G * HEAD_DIM
    gain = jnp.concatenate([jnp.tile(kn_cmp, NSA_G), jnp.ones((kvw,), F32)])
    flag = jnp.concatenate([jnp.ones((kvw,), F32), jnp.zeros((kvw,), F32)])
    return wc, w1b, pe8, w2p, jnp.stack([gain, flag])


def _prompt_tables(rel_bias, t, tq):
    ncp = t // CMP_STRIDE
    nt = t // tq
    n_cmp = (t - CMP_BLOCK) // CMP_STRIDE + 1
    n_slc = -(-t // SLC_BLOCK)
    tab = _dist_table(rel_bias, max(t, MAX_DIST) + LANES)
    per = LANES // CMP_STRIDE
    lo = [NEG_PAD - CMP_STRIDE * b - (CMP_BLOCK - 1) for b in range(per)]
    base = jnp.stack([tab[:, x:x + t + LANES] for x in lo], axis=1)
    bc = _skew(base, ncp // per, LANES)
    bc = bc.reshape(NSA_G, NSA_R, per, ncp // per, nt, tq)
    bc = jnp.transpose(bc, (4, 0, 3, 2, 1, 5)).reshape(nt, NSA_G, ncp, NSA_R * tq)
    near = [_group_rows(_toeplitz(tab, LANES * d, LANES, LANES)) for d in range(N_NEAR)]
    far = jnp.broadcast_to(tab[:, -1].reshape(NSA_G, NSA_R, 1, 1), (NSA_G, NSA_R, LANES, LANES))
    bt = jnp.stack(near + [far.reshape(NSA_G, NSA_R * LANES, LANES)])
    bt = jnp.transpose(bt, (0, 1, 3, 2))
    return bc, bt, _overlap(ncp, n_cmp, n_slc).T


def _sample_tables(rel_bias, past, t, nwin):
    nch = past // CMP_STRIDE
    n_cmp = (past + t - CMP_BLOCK) // CMP_STRIDE + 1
    n_slc = -(-(past + t) // SLC_BLOCK)
    tab = _dist_table(rel_bias, past + LANES)
    dist_c = past + np.arange(t)[:, None] - (np.arange(nch)[None, :] * CMP_STRIDE + CMP_BLOCK - 1)
    bc = _group_rows(tab[:, np.maximum(dist_c, -NEG_PAD) + NEG_PAD])
    bs = _group_rows(_toeplitz(tab, past, t, past + PAGE))
    bw = _group_rows(_toeplitz(tab, nwin, t, nwin + PAGE))
    e_mat = (np.arange(LANES)[:, None] == np.arange(past + PAGE)[None, :] // SLC_BLOCK)
    return bc, bs, bw, _overlap(nch, n_cmp, n_slc), jnp.asarray(e_mat.astype(np.float32), BF16)


def _nsa_prompt_mix(qn, kcvc, kvb, vt, gates, cw, tables, pq, batch):
    bc, bt, ovt = tables
    ckv, ckvt = _nsa_compress(kcvc, cw, batch)
    sel, part = _nsa_cmp(qn, ckv, ckvt, bc, ovt, pq, gates, batch, LANES)
    return _nsa_slc_win(qn, kvb, vt, sel, part, gates, bt, pq, batch)


TM = 512
TH = 1408
MEM_NB = 8


def kernel(x_prompt, x_sample, mem_prompt, cache_nsa_kv, cache_nsa_win, cache_sb_kv, cache_mem_kv, page_table,
           rel_bias, norm_mix, norm_ffn, norm_mem, w_in_nsa, w_in_sb, w_mem_kv, w_out, nsa_qk_norm, x_qk_norm,
           cmp_pe, cmp_w1, cmp_w2, w_ffn_in, w_ffn_out):
    batch, seq, _ = x_prompt.shape
    dbatch, dseq, _ = x_sample.shape
    depth = norm_mix.shape[0]
    n_pages = page_table.shape[1]
    past = n_pages * PAGE
    nwin = cache_nsa_win.shape[2]
    kvw = NSA_G * HEAD_DIM

    xp = x_prompt.reshape(batch * seq, D_MODEL)
    xs = x_sample.reshape(dbatch * dseq, D_MODEL)
    mem = mem_prompt.reshape(batch * N_MEM, D_MODEL)
    def feature_major(c):
        ct = jnp.transpose(c, (0, 1, 3, 4, 5, 2))
        return ct.reshape(c.shape[0], c.shape[1], -1, c.shape[2])
    def token_major(a, feat):
        return jnp.transpose(a.reshape((a.shape[0],) + feat + (a.shape[2],)), (0, 4, 1, 2, 3))
    nsa_pool = feature_major(cache_nsa_kv)
    sb_pool = feature_major(cache_sb_kv)
    win_cache = feature_major(cache_nsa_win)
    mem_cache = feature_major(cache_mem_kv)

    pq = _head_placement()
    ptab = _prompt_tables(rel_bias, seq, LANES)
    bc_s, bs_s, bw_s, ov_s, e_s = _sample_tables(rel_bias, past, dseq, nwin)
    wo = w_out.astype(BF16)
    wfi = w_ffn_in.astype(BF16)
    wfo = w_ffn_out.astype(BF16)

    nsa_s, win_p, win_s, sb_s, mem_p = [], [], [], [], []
    n_nsa, n_sb = (depth + 1) // 2, depth // 2
    nsa_rows = jnp.zeros((n_nsa, batch, 4 * kvw, seq), F32)
    sb_rows = jnp.zeros((n_sb, batch, 2 * TOK_WIDTH, seq), F32)
    for l in range(depth):
        j = l // 2
        gmix = norm_mix[l][None]
        wm, gfm = _mem_in_weights(w_mem_kv[l], x_qk_norm[l, 1])
        mkvb, mkv_t = _inproj(mem, norm_mem[l][None], wm, gfm, MEM_CHUNKS_P,
                              [(2 * X_WIDTH, BF16), ("T", 2 * X_WIDTH, F32)], N_MEM, MEM_TCHUNKS_P, N_MEM)
        mem_p.append(token_major(mkv_t, (2, X_HEADS, HEAD_DIM)))
        if l % 2 == 0:
            w, gf = _nsa_in_weights(w_in_nsa[j], nsa_qk_norm[j], x_qk_norm[l, 0])
            qn, kcvc, kvb, xqn, gates, nsa_rows, rows_wt, vt = _inproj(
                xp, gmix, w, gf, NSA_CHUNKS_P,
                [(TOK_WIDTH, BF16), (2 * kvw, F32), (6 * kvw, BF16), (X_WIDTH, BF16), (LANES, F32),
                 ("T", 4 * kvw, F32, (n_nsa, j, nsa_rows)), ("T", 2 * kvw, F32), ("tiles", 2 * kvw, BF16)],
                TM, NSA_TCHUNKS_P, seq)
            cw = _cmp_weights(cmp_pe[j], cmp_w1[j], cmp_w2[j], nsa_qk_norm[j, 1])
            tok_p = _nsa_prompt_mix(qn, kcvc, kvb, vt, gates, cw, ptab, pq, batch)
            qs, rows_s, rows_ws, xqs, gates_s = _inproj(
                xs, gmix, w, gf, NSA_CHUNKS_S,
                [(TOK_WIDTH, F32), (4 * kvw, F32), (2 * kvw, F32), (X_WIDTH, F32), (LANES, F32)], TM)
            tok_s, wout = _nsa_sample(
                qs.reshape(dbatch, dseq, TOK_WIDTH), rows_s.reshape(dbatch, dseq, 4 * kvw),
                rows_ws.reshape(dbatch, dseq, 2 * kvw), gates_s.reshape(dbatch, dseq, LANES),
                win_cache, nsa_pool, j, page_table, cw, pq, ov_s, e_s, bc_s, bs_s, bw_s)
            nsa_s.append(rows_s.reshape(dbatch, dseq, 4, NSA_G, HEAD_DIM))
            win_p.append(token_major(rows_wt[:, :, seq - min(WINDOW, seq):], (2, NSA_G, HEAD_DIM)))
            win_s.append(token_major(wout, (2, NSA_G, HEAD_DIM)))
        else:
            w, gf = _sb_in_weights(w_in_sb[j], x_qk_norm[l, 0])
            q, kvb, xqn, sb_rows = _inproj(
                xp, gmix, w, gf, SB_CHUNKS_P,
                [(TOK_WIDTH, BF16), (2 * TOK_WIDTH, BF16), (X_WIDTH, BF16),
                 ("T", 2 * TOK_WIDTH, F32, (n_sb, j, sb_rows))], TM, SB_TCHUNKS_P, seq)
            tok_p = _sb_prompt(q, kvb, batch, LANES)
            qs, rows_s, xqs = _inproj(
                xs, gmix, w, gf, SB_CHUNKS_S, [(TOK_WIDTH, F32), (2 * TOK_WIDTH, F32), (X_WIDTH, F32)], TM)
            tok_s = _sb_sample(qs.reshape(dbatch, dseq, TOK_WIDTH), rows_s.reshape(dbatch, dseq, 2 * TOK_WIDTH),
                               sb_pool, j, page_table)
            sb_s.append(rows_s.reshape(dbatch, dseq, 2, TOK_HEADS, HEAD_DIM))
        memo_p = _memattn_prompt(xqn, mkvb, batch, TM)
        memo_s = _memattn_sample(xqs, mem_cache, l, dseq, MEM_NB)
        g_ffn = norm_ffn[l][None]
        xp = _out_ffn(xp, tok_p, memo_p, wo[l, :TOK_WIDTH], wo[l, TOK_WIDTH:], g_ffn, wfi[l], wfo[l], TM, TH)
        xs = _out_ffn(xs, tok_s.reshape(dbatch * dseq, TOK_WIDTH), memo_s, wo[l, :TOK_WIDTH], wo[l, TOK_WIDTH:],
                      g_ffn, wfi[l], wfo[l], TM, TH)
    def layers_token_major(a, feat):
        return jnp.transpose(a.reshape(a.shape[:2] + feat + (a.shape[3],)), (0, 1, 5, 2, 3, 4))
    return (xp.reshape(batch, seq, D_MODEL), xs.reshape(dbatch, dseq, D_MODEL),
            layers_token_major(nsa_rows, (4, NSA_G, HEAD_DIM)), jnp.stack(nsa_s),
            jnp.stack(win_p), jnp.stack(win_s), layers_token_major(sb_rows, (2, TOK_HEADS, HEAD_DIM)),
            jnp.stack(sb_s), jnp.stack(mem_p))
```

```python
import functools
import math

import numpy as np
import jax
import jax.numpy as jnp
from jax import lax
from jax.experimental import pallas as pl
from jax.experimental.pallas import tpu as pltpu

F32 = jnp.float32
BF16 = jnp.bfloat16

D_MODEL = 1024
HEAD_DIM = 64
TOK_HEADS = 12
X_HEADS = 4
TOK_WIDTH = TOK_HEADS * HEAD_DIM
X_WIDTH = X_HEADS * HEAD_DIM
N_MEM = 256
NSA_G = 3
NSA_R = 4
CMP_BLOCK = 32
CMP_STRIDE = 16
CMP_HID = 128
SLC_BLOCK = 64
SLC_TOPN = 8
WINDOW = 512
FORCE_BONUS = 1e4
N_BUCKETS = 32
MAX_DIST = 1024
FFN_HIDDEN = 2816
RMS_EPS = 1e-6
PAGE = 128
SCALE = HEAD_DIM ** -0.5

LANES = 128
HALF = LANES // 2
VMEM_LIMIT = 56 * 1024 * 1024

EXP_ZERO = -104.0
NEG = -1e30


def _cp(sem, vmem=VMEM_LIMIT):
    return pltpu.CompilerParams(dimension_semantics=sem, vmem_limit_bytes=vmem)


def _dot(a, b):
    return jnp.dot(a, b, preferred_element_type=F32)


def _dot_t(a, b):
    return lax.dot_general(a, b, (((1,), (1,)), ((), ())), preferred_element_type=F32)


def _split_dot(x, w):
    hi = x.astype(BF16)
    lo = (x - hi.astype(F32)).astype(BF16)
    return _dot(hi, w) + _dot(lo, w)


def _rms_rows(x, g):
    ms = jnp.mean(x * x, axis=-1, keepdims=True)
    return x * lax.rsqrt(ms + RMS_EPS) * g


def _head_scale(h, gain, flag, s128):
    x2 = h * h
    parts = []
    for k in range(h.shape[1] // LANES):
        parts.append(_split_dot(x2[:, LANES * k:LANES * (k + 1)], s128))
    ssq = parts[0] if len(parts) == 1 else jnp.concatenate(parts, axis=1)
    r = lax.rsqrt(ssq * (1.0 / HEAD_DIM) + RMS_EPS)
    return h * jnp.where(flag > 0.5, r * gain, 1.0)


def _sigmoid(x):
    return 1.0 / (1.0 + jnp.exp(-x))


def _softplus(z):
    return jnp.maximum(z, 0.0) + jnp.log1p(jnp.exp(-jnp.abs(z)))


def _group_ones():
    i = np.arange(LANES)
    return jnp.asarray((i[:, None] // HALF == i[None, :] // HALF).astype(np.float32), BF16)


def _inproj_body(x_ref, g_ref, w_ref, gf_ref, s_ref, *rest, chunks, tchunks, n_carried):
    if tchunks:
        wt_ref, gcol_ref = rest[:2]
        out_refs = rest[2 + n_carried:]
    else:
        out_refs = rest[n_carried:]
    xn = _rms_rows(x_ref[...], g_ref[...]).astype(BF16)
    for c0, c1, norm, outs in chunks:
        h = _dot(xn, w_ref[:, c0:c1])
        if norm:
            h = _head_scale(h, gf_ref[0:1, c0:c1], gf_ref[1:2, c0:c1], s_ref[...])
        for oi, o0, kind in outs:
            o_ref = out_refs[oi]
            if kind == "sigmoid":
                o_ref[:, o0:o0 + (c1 - c0)] = _sigmoid(h)
            else:
                o_ref[:, o0:o0 + (c1 - c0)] = h.astype(o_ref.dtype)
    for r0, r1, norm, outs in tchunks:
        ht = _dot_t(wt_ref[r0:r1, :], xn)
        if norm:
            parts = []
            for k in range((r1 - r0) // HEAD_DIM):
                blk = ht[HEAD_DIM * k:HEAD_DIM * (k + 1)]
                ms = jnp.sum(blk * blk, axis=0, keepdims=True) * (1.0 / HEAD_DIM)
                parts.append(blk * lax.rsqrt(ms + RMS_EPS))
            ht = jnp.concatenate(parts, axis=0) * gcol_ref[r0:r1, :]
        for oi, o0, kind in outs:
            o_ref = out_refs[oi]
            if kind == "tiles":
                for j in range(ht.shape[1] // LANES):
                    o_ref[j, o0:o0 + (r1 - r0), :] = ht[:, LANES * j:LANES * (j + 1)].astype(o_ref.dtype)
            else:
                o_ref[o0:o0 + (r1 - r0), :] = ht.astype(o_ref.dtype)


def _inproj(x, g, w, gf, chunks, out_defs, tm, tchunks=(), rows_per_batch=None):
    m = x.shape[0]
    n = w.shape[1]
    assert m % tm == 0
    tb = rows_per_batch or m
    ntb = tb // tm
    nb = m // tb
    out_shape, out_specs, carried = [], [], {}
    for oi, d in enumerate(out_defs):
        if d[0] == "T" and len(d) == 4:
            nlayers, lidx, prev = d[3]
            out_shape.append(jax.ShapeDtypeStruct((nlayers, nb, d[1], tb), d[2]))
            out_specs.append(pl.BlockSpec((None, None, d[1], tm), lambda i, lidx=lidx: (lidx, i // ntb, 0, i % ntb)))
            if prev is not None:
                carried[oi] = prev
        elif d[0] == "T":
            out_shape.append(jax.ShapeDtypeStruct((nb, d[1], tb), d[2]))
            out_specs.append(pl.BlockSpec((None, d[1], tm), lambda i: (i // ntb, 0, i % ntb)))
        elif d[0] == "tiles":
            out_shape.append(jax.ShapeDtypeStruct((nb, tb // LANES, d[1], LANES), d[2]))
            out_specs.append(pl.BlockSpec((None, tm // LANES, d[1], LANES), lambda i: (i // ntb, i % ntb, 0, 0)))
        else:
            out_shape.append(jax.ShapeDtypeStruct((m, d[0]), d[1]))
            out_specs.append(pl.BlockSpec((tm, d[0]), lambda i: (i, 0)))
    in_specs = [
        pl.BlockSpec((tm, D_MODEL), lambda i: (i, 0)),
        pl.BlockSpec((1, D_MODEL), lambda i: (0, 0)),
        pl.BlockSpec((D_MODEL, n), lambda i: (0, 0)),
        pl.BlockSpec((2, n), lambda i: (0, 0)),
        pl.BlockSpec((LANES, LANES), lambda i: (0, 0)),
    ]
    args = [x, g, w, gf, _group_ones()]
    if tchunks:
        in_specs += [pl.BlockSpec((n, D_MODEL), lambda i: (0, 0)), pl.BlockSpec((n, 1), lambda i: (0, 0))]
        args += [w.T, gf[0].reshape(n, 1)]
    aliases = {}
    for oi, prev in carried.items():
        aliases[len(args)] = oi
        in_specs.append(pl.BlockSpec(memory_space=pl.ANY))
        args.append(prev)
    return pl.pallas_call(
        functools.partial(_inproj_body, chunks=chunks, tchunks=tchunks, n_carried=len(carried)),
        grid=(m // tm,),
        in_specs=in_specs,
        out_specs=out_specs,
        out_shape=out_shape,
        input_output_aliases=aliases,
        compiler_params=_cp(("parallel",)),
        name="inproj",
    )(*args)


NSA_N = 2304
NSA_CHUNKS = (
    (0, 384, True, ((0, 0, "cast"),)),
    (384, 768, True, ((0, 384, "cast"),)),
    (768, 1152, False, ((1, 0, "cast"), (3, 0, "cast"))),
    (1152, 1536, True, ((1, 384, "cast"), (3, 384, "cast"))),
    (1536, 1920, True, ((2, 0, "cast"), (3, 768, "cast"))),
    (1920, 2176, True, ((4, 0, "cast"),)),
    (2176, 2304, False, ((5, 0, "sigmoid"),)),
)
SB_N = 2560
SB_CHUNKS = (
    (0, 384, False, ((0, 0, "cast"),)),
    (384, 768, False, ((0, 384, "cast"),)),
    (768, 1152, False, ((1, 0, "cast"), (2, 0, "cast"))),
    (1152, 1536, False, ((1, 384, "cast"), (2, 384, "cast"))),
    (1536, 1920, False, ((1, 768, "cast"), (2, 768, "cast"))),
    (1920, 2304, False, ((1, 1152, "cast"), (2, 1152, "cast"))),
    (2304, 2560, True, ((3, 0, "cast"),)),
)
NSA_CHUNKS_P = (
    (0, 384, True, ((0, 0, "cast"),)),
    (384, 768, True, ((0, 384, "cast"),)),
    (768, 1152, False, ((1, 0, "cast"), (2, 0, "cast"))),
    (1152, 1536, True, ((2, 384, "cast"),)),
    (1536, 1920, True, ((2, 768, "cast"),)),
    (1920, 2176, True, ((3, 0, "cast"),)),
    (2176, 2304, False, ((4, 0, "sigmoid"),)),
)
NSA_TCHUNKS_P = (
    (768, 1152, False, ((5, 0, "rows"),)),
    (1152, 1344, True, ((5, 384, "rows"),)),
    (1344, 1536, False, ((5, 576, "rows"), (7, 0, "tiles"))),
    (1536, 1728, True, ((6, 0, "rows"),)),
    (1728, 1920, False, ((6, 192, "rows"), (7, 192, "tiles"))),
)
SB_CHUNKS_P = (
    (0, 384, False, ((0, 0, "cast"),)),
    (384, 768, False, ((0, 384, "cast"),)),
    (768, 1152, False, ((1, 0, "cast"),)),
    (1152, 1536, False, ((1, 384, "cast"),)),
    (1536, 1920, False, ((1, 768, "cast"),)),
    (1920, 2304, False, ((1, 1152, "cast"),)),
    (2304, 2560, True, ((2, 0, "cast"),)),
)
SB_TCHUNKS_P = (
    (768, 1152, False, ((3, 0, "rows"),)),
    (1152, 1536, False, ((3, 384, "rows"),)),
    (1536, 1920, False, ((3, 768, "rows"),)),
    (1920, 2304, False, ((3, 1152, "rows"),)),
)
MEM_CHUNKS_P = (
    (0, 256, True, ((0, 0, "cast"),)),
    (256, 512, False, ((0, 256, "cast"),)),
)
MEM_TCHUNKS_P = (
    (0, 256, True, ((1, 0, "rows"),)),
    (256, 512, False, ((1, 256, "rows"),)),
)
NSA_CHUNKS_S = (
    (0, 384, True, ((0, 0, "cast"),)),
    (384, 768, True, ((0, 384, "cast"),)),
    (768, 1152, False, ((1, 0, "cast"),)),
    (1152, 1536, True, ((1, 384, "cast"),)),
    (1536, 1920, True, ((2, 0, "cast"),)),
    (1920, 2176, True, ((3, 0, "cast"),)),
    (2176, 2304, False, ((4, 0, "sigmoid"),)),
)
SB_CHUNKS_S = (
    (0, 384, False, ((0, 0, "cast"),)),
    (384, 768, False, ((0, 384, "cast"),)),
    (768, 1152, False, ((1, 0, "cast"),)),
    (1152, 1536, False, ((1, 384, "cast"),)),
    (1536, 1920, False, ((1, 768, "cast"),)),
    (1920, 2304, False, ((1, 1152, "cast"),)),
    (2304, 2560, True, ((2, 0, "cast"),)),
)
MEM_CHUNKS = (
    (0, 256, True, ((0, 0, "cast"), (1, 0, "cast"))),
    (256, 512, False, ((0, 256, "cast"), (1, 256, "cast"))),
)


def _nsa_in_weights(w_in, qk_norm, xq_gain):
    o = TOK_WIDTH + 6 * NSA_G * HEAD_DIM
    w = jnp.concatenate([w_in[:, :o + X_WIDTH],
                         jnp.pad(w_in[:, o + X_WIDTH:], ((0, 0), (0, LANES - 3 * TOK_HEADS)))], axis=1)
    kvw = NSA_G * HEAD_DIM
    one = jnp.ones((kvw,), F32)
    zero = jnp.zeros((kvw,), F32)
    gain = jnp.concatenate([jnp.tile(qk_norm[0], TOK_HEADS), one, one, jnp.tile(qk_norm[2], NSA_G), one,
                            jnp.tile(qk_norm[3], NSA_G), one, jnp.tile(xq_gain, X_HEADS), jnp.ones((LANES,), F32)])
    flag = jnp.concatenate([jnp.ones((TOK_WIDTH,), F32), zero, zero, one, zero, one, zero,
                            jnp.ones((X_WIDTH,), F32), jnp.zeros((LANES,), F32)])
    return w.astype(BF16), jnp.stack([gain, flag])


def _sb_in_weights(w_in, xq_gain):
    gain = jnp.concatenate([jnp.ones((3 * TOK_WIDTH,), F32), jnp.tile(xq_gain, X_HEADS)])
    flag = jnp.concatenate([jnp.zeros((3 * TOK_WIDTH,), F32), jnp.ones((X_WIDTH,), F32)])
    return w_in.astype(BF16), jnp.stack([gain, flag])


def _mem_in_weights(w_kv, k_gain):
    gain = jnp.concatenate([jnp.tile(k_gain, X_HEADS), jnp.ones((X_WIDTH,), F32)])
    flag = jnp.concatenate([jnp.ones((X_WIDTH,), F32), jnp.zeros((X_WIDTH,), F32)])
    return w_kv.astype(BF16), jnp.stack([gain, flag])


def _ffn_body(x_ref, tok_ref, mem_ref, wot_ref, wom_ref, g_ref, wa_ref, wu_ref, wo_ref, o_ref, acc_ref, xn_ref):
    j = pl.program_id(1)

    @pl.when(j == 0)
    def _():
        xm = (x_ref[...] + _dot(tok_ref[...].astype(BF16), wot_ref[...])
              + _dot(mem_ref[...].astype(BF16), wom_ref[...]))
        acc_ref[...] = xm
        xn_ref[...] = _rms_rows(xm, g_ref[...]).astype(BF16)

    xn = xn_ref[...]
    a = _dot(xn, wa_ref[...])
    u = _dot(xn, wu_ref[...])
    hsw = (a * _sigmoid(a) * u).astype(BF16)
    acc_ref[...] += _dot(hsw, wo_ref[...])

    @pl.when(j == pl.num_programs(1) - 1)
    def _():
        o_ref[...] = acc_ref[...]


def _out_ffn(x, tok, mem, wo_tok, wo_mem, g, w_in, w_out, tm, th):
    m = x.shape[0]
    nh = FFN_HIDDEN // th
    return pl.pallas_call(
        _ffn_body,
        grid=(m // tm, nh),
        in_specs=[
            pl.BlockSpec((tm, D_MODEL), lambda i, j: (i, 0)),
            pl.BlockSpec((tm, TOK_WIDTH), lambda i, j: (i, 0)),
            pl.BlockSpec((tm, X_WIDTH), lambda i, j: (i, 0)),
            pl.BlockSpec((TOK_WIDTH, D_MODEL), lambda i, j: (0, 0)),
            pl.BlockSpec((X_WIDTH, D_MODEL), lambda i, j: (0, 0)),
            pl.BlockSpec((1, D_MODEL), lambda i, j: (0, 0)),
            pl.BlockSpec((D_MODEL, th), lambda i, j: (0, j)),
            pl.BlockSpec((D_MODEL, th), lambda i, j: (0, j + nh)),
            pl.BlockSpec((th, D_MODEL), lambda i, j: (j, 0)),
        ],
        out_specs=pl.BlockSpec((tm, D_MODEL), lambda i, j: (i, 0)),
        out_shape=jax.ShapeDtypeStruct((m, D_MODEL), F32),
        scratch_shapes=[pltpu.VMEM((tm, D_MODEL), F32), pltpu.VMEM((tm, D_MODEL), BF16)],
        compiler_params=_cp(("parallel", "arbitrary")),
        name="out_ffn",
    )(x, tok, mem, wo_tok, wo_mem, g, w_in, w_in, w_out)


def _mem_heads(q, kv, lane, feature_major=False):
    outs = []
    for c in range(X_WIDTH // LANES):
        q2 = q[:, LANES * c:LANES * (c + 1)]
        if feature_major:
            k2 = kv[LANES * c:LANES * (c + 1), :]
            v2 = kv[X_WIDTH + LANES * c:X_WIDTH + LANES * (c + 1), :]
        else:
            k2 = kv[:, LANES * c:LANES * (c + 1)]
            v2 = kv[:, X_WIDTH + LANES * c:X_WIDTH + LANES * (c + 1)]
        halves = []
        for half in range(2):
            sel = (lane < HALF) if half == 0 else (lane >= HALF)
            qm = jnp.where(sel, q2, jnp.zeros_like(q2))
            s = (_dot(qm, k2) if feature_major else _dot_t(qm, k2)) * SCALE
            e = jnp.exp(s - jnp.max(s, axis=-1, keepdims=True))
            p = (e / jnp.sum(e, axis=-1, keepdims=True)).astype(BF16)
            halves.append(_dot_t(p, v2) if feature_major else _dot(p, v2))
        outs.append(jnp.where(lane < HALF, halves[0], halves[1]))
    return jnp.concatenate(outs, axis=1)


def _memattn_p_body(q_ref, kv_ref, o_ref):
    lane = lax.broadcasted_iota(jnp.int32, (q_ref.shape[0], LANES), 1)
    o_ref[...] = _mem_heads(q_ref[...], kv_ref[...], lane).astype(o_ref.dtype)


def _memattn_prompt(xq, mkv, batch, tq):
    m = xq.shape[0]
    nt = m // batch // tq
    return pl.pallas_call(
        _memattn_p_body,
        grid=(batch, nt),
        in_specs=[pl.BlockSpec((tq, X_WIDTH), lambda b, i: (b * nt + i, 0)),
                  pl.BlockSpec((N_MEM, 2 * X_WIDTH), lambda b, i: (b, 0))],
        out_specs=pl.BlockSpec((tq, X_WIDTH), lambda b, i: (b * nt + i, 0)),
        out_shape=jax.ShapeDtypeStruct((m, X_WIDTH), BF16),
        compiler_params=_cp(("parallel", "parallel")),
        name="memattn_prompt",
    )(xq, mkv)


def _memattn_s_body(q_ref, kv_ref, o_ref, *, nb, t):
    lane = lax.broadcasted_iota(jnp.int32, (t, LANES), 1)
    npair = X_WIDTH // LANES
    units = [(bi, c, half) for bi in range(nb) for c in range(npair) for half in range(2)]
    q2 = {(bi, c): q_ref[bi * t:(bi + 1) * t, LANES * c:LANES * (c + 1)].astype(BF16)
          for bi in range(nb) for c in range(npair)}
    k2 = {(bi, c): kv_ref[bi, LANES * c:LANES * (c + 1), :].astype(BF16) for bi in range(nb) for c in range(npair)}
    v2 = {(bi, c): kv_ref[bi, X_WIDTH + LANES * c:X_WIDTH + LANES * (c + 1), :].astype(BF16)
          for bi in range(nb) for c in range(npair)}
    zero = jnp.zeros((t, LANES), BF16)
    qm = [jnp.where((lane < HALF) if half == 0 else (lane >= HALF), q2[bi, c], zero) for bi, c, half in units]
    s = [_dot(qm[u], k2[bi, c]) * SCALE for u, (bi, c, half) in enumerate(units)]
    e = [jnp.exp(x - jnp.max(x, axis=-1, keepdims=True)) for x in s]
    p = [(x / jnp.sum(x, axis=-1, keepdims=True)).astype(BF16) for x in e]
    o = [_dot_t(p[u], v2[bi, c]) for u, (bi, c, half) in enumerate(units)]
    for bi in range(nb):
        for c in range(npair):
            u = (bi * npair + c) * 2
            o_ref[bi * t:(bi + 1) * t, LANES * c:LANES * (c + 1)] = jnp.where(lane < HALF, o[u], o[u + 1])


def _memattn_sample(xq, mkv, layer, t, nb):
    m = xq.shape[0]
    batch = m // t
    return pl.pallas_call(
        functools.partial(_memattn_s_body, nb=nb, t=t),
        grid=(batch // nb,),
        in_specs=[pl.BlockSpec((nb * t, X_WIDTH), lambda i: (i, 0)),
                  pl.BlockSpec((None, nb, 2 * X_WIDTH, N_MEM), lambda i: (layer, i, 0, 0))],
        out_specs=pl.BlockSpec((nb * t, X_WIDTH), lambda i: (i, 0)),
        out_shape=jax.ShapeDtypeStruct((m, X_WIDTH), F32),
        compiler_params=_cp(("parallel",)),
        name="memattn_sample",
    )(xq, mkv)


def _sb_block(z, strict, carry, u_tri):
    sp = _softplus(z)
    ls = -sp if strict is None else jnp.where(strict, -sp, 0.0)
    after = _split_dot(ls, u_tri) + carry
    a = jnp.exp(z - sp + after)
    if strict is not None:
        a = jnp.where(strict, a, 0.0)
    return a, carry + jnp.sum(ls, axis=1, keepdims=True)


def _upper_tri(n):
    r = lax.broadcasted_iota(jnp.int32, (n, n), 0)
    c = lax.broadcasted_iota(jnp.int32, (n, n), 1)
    return jnp.where(r > c, 1.0, 0.0).astype(BF16)


def _sb_prompt_body(q_ref, k_ref, v_ref, o_ref, qh_s, c_s, a_s, *, tq):
    i = pl.program_id(1)
    npair = TOK_WIDTH // LANES
    u_tri = _upper_tri(tq)
    lane = lax.broadcasted_iota(jnp.int32, (tq, LANES), 1)
    for pair in range(npair):
        q2 = q_ref[:, LANES * pair:LANES * (pair + 1)] * SCALE
        zero = jnp.zeros_like(q2)
        qh_s[2 * pair] = jnp.where(lane < HALF, q2, zero)
        qh_s[2 * pair + 1] = jnp.where(lane >= HALF, q2, zero)
    c_s[...] = jnp.zeros_like(c_s)
    a_s[...] = jnp.zeros_like(a_s)

    def key_block(kb, strict):
        ks = pl.multiple_of(kb * tq, tq)
        heads = range(TOK_HEADS)
        kblk = [k_ref[pl.ds(ks, tq), LANES * p:LANES * (p + 1)] for p in range(npair)]
        vblk = [v_ref[pl.ds(ks, tq), LANES * p:LANES * (p + 1)] for p in range(npair)]
        z = [_dot_t(qh_s[h], kblk[h // 2]) for h in heads]
        sp = [_softplus(z[h]) for h in heads]
        ls = [-sp[h] if strict is None else jnp.where(strict, -sp[h], 0.0) for h in heads]
        after = [_split_dot(ls[h], u_tri) for h in heads]
        old = [c_s[h] for h in heads]
        w = [jnp.exp(z[h] - sp[h] + after[h] + old[h]) for h in heads]
        if strict is not None:
            w = [jnp.where(strict, w[h], 0.0) for h in heads]
        pv = [_dot(w[h].astype(BF16), vblk[h // 2]) for h in heads]
        top = None
        for h in heads:
            carry = old[h] + jnp.sum(ls[h], axis=1, keepdims=True)
            c_s[h] = carry
            top = carry if top is None else jnp.maximum(top, carry)
        for p in range(npair):
            a_s[p] += jnp.where(lane < HALF, pv[2 * p], pv[2 * p + 1])
        return jnp.max(top)

    row = lax.broadcasted_iota(jnp.int32, (tq, tq), 0)
    col = lax.broadcasted_iota(jnp.int32, (tq, tq), 1)
    top0 = key_block(i, col < row)

    def cond(c):
        kb, top = c
        return jnp.logical_and(kb >= 0, top > EXP_ZERO)

    def body(c):
        kb, _ = c
        return kb - 1, key_block(kb, None)

    lax.while_loop(cond, body, (i - 1, top0))
    for pair in range(npair):
        o_ref[:, LANES * pair:LANES * (pair + 1)] = a_s[pair].astype(o_ref.dtype)


def _sb_prompt(q, kvb, batch, tq):
    assert tq == LANES
    m = q.shape[0]
    t = m // batch
    nt = t // tq
    return pl.pallas_call(
        functools.partial(_sb_prompt_body, tq=tq),
        grid=(batch, nt),
        in_specs=[pl.BlockSpec((tq, TOK_WIDTH), lambda b, i: (b * nt + i, 0)),
                  pl.BlockSpec((t, TOK_WIDTH), lambda b, i: (b, 0)),
                  pl.BlockSpec((t, TOK_WIDTH), lambda b, i: (b, 1))],
        out_specs=pl.BlockSpec((tq, TOK_WIDTH), lambda b, i: (b * nt + i, 0)),
        out_shape=jax.ShapeDtypeStruct((m, TOK_WIDTH), BF16),
        scratch_shapes=[pltpu.VMEM((TOK_HEADS, tq, LANES), BF16), pltpu.VMEM((TOK_HEADS, tq, LANES), F32),
                        pltpu.VMEM((TOK_WIDTH // LANES, tq, LANES), F32)],
        compiler_params=_cp(("parallel", "parallel")),
        name="sb_prompt",
    )(q, kvb, kvb)


def _sb_sample_body(pt_ref, q_ref, new_ref, pool_ref, o_ref, qbd_ref, acc_ref, carry_ref, blk_ref, buf_ref, sem_ref,
                    *, t, n_pages, layer):
    b = pl.program_id(0)
    nb = pl.num_programs(0)
    mine = 2 * (b % 2)

    def page_copy(seq, page, slot):
        return pltpu.make_async_copy(pool_ref.at[layer, pt_ref[seq, page]], buf_ref.at[slot], sem_ref.at[slot])

    @pl.when(b == 0)
    def _():
        page_copy(0, n_pages - 1, (n_pages - 1) % 2).start()
        page_copy(0, n_pages - 2, (n_pages - 2) % 2).start()

    @pl.when(b + 1 < nb)
    def _():
        page_copy(b + 1, n_pages - 1, 2 - mine + (n_pages - 1) % 2).start()
        page_copy(b + 1, n_pages - 2, 2 - mine + (n_pages - 2) % 2).start()

    rows = TOK_HEADS * t
    lane = lax.broadcasted_iota(jnp.int32, (rows, TOK_WIDTH), 1)
    rowi = lax.broadcasted_iota(jnp.int32, (rows, TOK_WIDTH), 0)
    own = (lane // HEAD_DIM) == (rowi // t)
    q = q_ref[0] * SCALE
    qbd_ref[...] = jnp.where(own, jnp.concatenate([q] * TOK_HEADS, axis=0), 0.0).astype(BF16)
    u_tri = _upper_tri(PAGE)

    blk_ref[...] = jnp.zeros_like(blk_ref)
    blk_ref[0:t, :] = new_ref[0]
    kcol = lax.broadcasted_iota(jnp.int32, (rows, PAGE), 1)
    trow = lax.broadcasted_iota(jnp.int32, (rows, PAGE), 0) % t
    strict = kcol < trow
    kv = blk_ref[...]
    z = _dot_t(qbd_ref[...], kv[:, :TOK_WIDTH].astype(BF16))
    w, carry = _sb_block(z, strict, jnp.zeros((rows, 1), F32), u_tri)
    acc_ref[...] = _dot(w.astype(BF16), kv[:, TOK_WIDTH:].astype(BF16))
    carry_ref[...] = carry

    def cond(c):
        page, more = c
        return jnp.logical_and(page >= 0, more)

    def body(c):
        page, _ = c
        slot = mine + page % 2
        page_copy(b, page, slot).wait()
        kt = buf_ref[slot, 0:TOK_WIDTH, :].astype(BF16)
        vt = buf_ref[slot, TOK_WIDTH:2 * TOK_WIDTH, :].astype(BF16)
        w, carry = _sb_block(_dot(qbd_ref[...], kt), None, carry_ref[...], u_tri)
        acc_ref[...] += _dot_t(w.astype(BF16), vt)
        carry_ref[...] = carry
        more = jnp.max(carry) > EXP_ZERO

        @pl.when(jnp.logical_and(more, page >= 2))
        def _():
            page_copy(b, page - 2, slot).start()
        return page - 1, more

    left, _ = lax.while_loop(cond, body, (n_pages - 1, True))

    @pl.when(left >= 0)
    def _():
        page_copy(b, left, mine + left % 2).wait()

    acc = jnp.where(own, acc_ref[...], 0.0)
    out = acc[0:t]
    for h in range(1, TOK_HEADS):
        out = out + acc[h * t:(h + 1) * t]
    o_ref[0] = out


def _sb_sample(q, new_rows, pool, layer, page_table):
    batch, t, _ = q.shape
    n_pages = page_table.shape[1]
    rows = TOK_HEADS * t
    assert n_pages >= 2
    grid_spec = pltpu.PrefetchScalarGridSpec(
        num_scalar_prefetch=1,
        grid=(batch,),
        in_specs=[pl.BlockSpec((1, t, TOK_WIDTH), lambda b, pt: (b, 0, 0)),
                  pl.BlockSpec((1, t, 2 * TOK_WIDTH), lambda b, pt: (b, 0, 0)),
                  pl.BlockSpec(memory_space=pl.ANY)],
        out_specs=pl.BlockSpec((1, t, TOK_WIDTH), lambda b, pt: (b, 0, 0)),
        scratch_shapes=[pltpu.VMEM((rows, TOK_WIDTH), BF16), pltpu.VMEM((rows, TOK_WIDTH), F32),
                        pltpu.VMEM((rows, 1), F32), pltpu.VMEM((PAGE, 2 * TOK_WIDTH), F32),
                        pltpu.VMEM((4, 2 * TOK_WIDTH, PAGE), F32), pltpu.SemaphoreType.DMA((4,))],
    )
    return pl.pallas_call(
        functools.partial(_sb_sample_body, t=t, n_pages=n_pages, layer=layer),
        grid_spec=grid_spec,
        out_shape=jax.ShapeDtypeStruct((batch, t, TOK_WIDTH), F32),
        compiler_params=_cp(("arbitrary",)),
        name="sb_sample",
    )(page_table, q, new_rows, pool)


def _masked_softmax(s, mask, axis=-1):
    s = jnp.where(mask, s, -jnp.inf)
    m = jnp.max(s, axis=axis, keepdims=True)
    m = jnp.where(m == -jnp.inf, 0.0, m)
    e = jnp.exp(s - m)
    return e / jnp.maximum(jnp.sum(e, axis=axis, keepdims=True), 1e-30)


def _compress(load_chunks, nch, wc_ref, w1_ref, pe_ref, w2p_ref, gf_ref, s128):
    acc = [jnp.zeros((nch, 4 * CMP_HID), F32) for _ in range(3)]
    for p in range(CMP_STRIDE):
        for c in range(3):
            acc[c] = acc[c] + _dot(load_chunks(p, c).astype(BF16), wc_ref[c, p])
    bias = [_dot(pe_ref[tt], w1_ref[tt])[0:1] for tt in range(2)]
    outs = []
    for c, (ta, tb) in enumerate(((0, 0), (0, 1), (1, 1))):
        hid = []
        for half, tt in ((0, ta), (1, tb)):
            first = acc[c][:, 2 * CMP_HID * half:2 * CMP_HID * half + CMP_HID]
            second = acc[c][:, 2 * CMP_HID * half + CMP_HID:2 * CMP_HID * (half + 1)]
            pre = first + pltpu.roll(second, nch - 1, 0) + bias[tt]
            hid.append((pre * _sigmoid(pre)).astype(BF16))
        outs.append(_dot(hid[0], w2p_ref[ta, 0]) + _dot(hid[1], w2p_ref[tb, 1]))
    ckv = jnp.concatenate(outs, axis=1)
    return _head_scale(ckv, gf_ref[0:1, :], gf_ref[1:2, :], s128)


def _select_topn(score, topn, axis=1):
    lane = lax.broadcasted_iota(jnp.int32, score.shape, axis)
    sel = jnp.zeros(score.shape, F32)
    cur = score
    for _ in range(topn):
        mx = jnp.max(cur, axis=axis, keepdims=True)
        is_max = jnp.logical_and(cur == mx, mx > -jnp.inf)
        idx = jnp.min(jnp.where(is_max, lane, score.shape[axis]), axis=axis, keepdims=True)
        pick = lane == idx
        sel = jnp.where(pick, 1.0, sel)
        cur = jnp.where(pick, -jnp.inf, cur)
    return sel


def _stack_heads(q, pq_ref, g, t, dtype=BF16):
    return jnp.concatenate([_dot(q, pq_ref[NSA_R * g + r])[0:t] for r in range(NSA_R)], axis=0).astype(dtype)


def _place_heads(o, g, t, gate_of, lane):
    vh = (g + 1) % 2
    tiles = []
    for rp in range(2):
        outs = []
        for rr in range(2):
            r = 2 * rp + rr
            o_r = o[r * t:(r + 1) * t]
            if vh != rr:
                o_r = pltpu.roll(o_r, HALF, 1)
            outs.append(o_r * gate_of(NSA_R * g + r))
        tiles.append(jnp.where(lane < HALF, outs[0], outs[1]))
    return tiles


def _nsa_compress_body(r0_ref, r1_ref, r2_ref, wc_ref, w1_ref, pe_ref, w2p_ref, gf_ref, s_ref, o_ref, ot_ref, *, nch):
    tiles = (r0_ref, r1_ref, r2_ref)

    def load(p, c):
        return tiles[c][pl.ds(p, nch, stride=CMP_STRIDE), :]
    ckv = _compress(load, nch, wc_ref, w1_ref, pe_ref, w2p_ref, gf_ref, s_ref[...])
    o_ref[...] = ckv
    ot_ref[...] = ckv.T


def _cmp_const_specs():
    return [pl.BlockSpec((3, CMP_STRIDE, LANES, 4 * CMP_HID), lambda *a: (0, 0, 0, 0)),
            pl.BlockSpec((2, CMP_BLOCK * HEAD_DIM, CMP_HID), lambda *a: (0, 0, 0)),
            pl.BlockSpec((2, 8, CMP_BLOCK * HEAD_DIM), lambda *a: (0, 0, 0)),
            pl.BlockSpec((2, 2, CMP_HID, LANES), lambda *a: (0, 0, 0, 0)),
            pl.BlockSpec((2, 3 * LANES), lambda *a: (0, 0)),
            pl.BlockSpec((LANES, LANES), lambda *a: (0, 0))]


def _nsa_compress(rows, cw, batch):
    m = rows.shape[0]
    t = m // batch
    nch = t // CMP_STRIDE
    return pl.pallas_call(
        functools.partial(_nsa_compress_body, nch=nch),
        grid=(batch,),
        in_specs=[pl.BlockSpec((t, LANES), lambda b, c=c: (b, c)) for c in range(3)] + _cmp_const_specs(),
        out_specs=[pl.BlockSpec((nch, 3 * LANES), lambda b: (b, 0)),
                   pl.BlockSpec((None, 3 * LANES, nch), lambda b: (b, 0, 0))],
        out_shape=[jax.ShapeDtypeStruct((batch * nch, 3 * LANES), F32),
                   jax.ShapeDtypeStruct((batch, 3 * LANES, nch), F32)],
        compiler_params=_cp(("parallel",)),
        name="nsa_compress",
    )(rows, rows, rows, *cw, _group_ones())


def _nsa_cmp_body(q_ref, ckv_ref, ckvt_ref, bias_ref, ovt_ref, pq_ref, g_ref, sel_ref, part_ref, *, tq, ncp, topn):
    i = pl.program_id(1)
    q0 = i * tq
    q = q_ref[...]
    ck = ckv_ref[...].astype(BF16)
    cvt = ckvt_ref[...].astype(BF16)
    rows4 = NSA_R * tq
    nrow = lax.broadcasted_iota(jnp.int32, (ncp, rows4), 0)
    tpos = q0 + lax.broadcasted_iota(jnp.int32, (ncp, rows4), 1) % tq
    mask_c = (CMP_STRIDE * nrow + CMP_BLOCK - 1) <= tpos
    lane = lax.broadcasted_iota(jnp.int32, (tq, LANES), 1)
    blk = lax.broadcasted_iota(jnp.int32, (LANES, NSA_G * tq), 0)
    tq_pos = q0 + lax.broadcasted_iota(jnp.int32, (LANES, NSA_G * tq), 1) % tq
    valid = blk * SLC_BLOCK <= tq_pos
    curb = tq_pos // SLC_BLOCK
    forced = (blk == 0) | (blk == curb) | (blk == curb - 1)
    gates = g_ref[...]
    groups = range(NSA_G)
    qt = [(_stack_heads(q, pq_ref, g, tq, F32) * SCALE).T.astype(BF16) for g in groups]
    s = [_dot(ck[:, LANES * (g // 2):LANES * (g // 2 + 1)], qt[g]) + bias_ref[0, g] for g in groups]
    pb = [_masked_softmax(s[g], mask_c, axis=0).astype(BF16) for g in groups]
    kvw = NSA_G * HEAD_DIM
    o_t = [_dot(cvt[kvw + HEAD_DIM * g:kvw + HEAD_DIM * (g + 1), :], pb[g]) for g in groups]
    imp4 = [_dot(ovt_ref[...], pb[g]) for g in groups]
    imp = jnp.concatenate([imp4[g][:, 0:tq] + imp4[g][:, tq:2 * tq] + imp4[g][:, 2 * tq:3 * tq]
                           + imp4[g][:, 3 * tq:4 * tq] for g in groups], axis=1)
    score = jnp.where(valid, imp + jnp.where(forced, FORCE_BONUS, 0.0), -jnp.inf)
    sel = _select_topn(score, topn, axis=0).astype(sel_ref.dtype)
    for g in groups:
        sel_ref[0, g] = sel[:, g * tq:(g + 1) * tq]
        for rp in range(2):
            h0 = NSA_R * g + 2 * rp
            pair = jnp.concatenate([o_t[g][:, 2 * rp * tq:(2 * rp + 1) * tq],
                                    o_t[g][:, (2 * rp + 1) * tq:(2 * rp + 2) * tq]], axis=0)
            gate = jnp.where(lane < HALF, gates[:, h0:h0 + 1], gates[:, h0 + 1:h0 + 2])
            k = 2 * g + rp
            part_ref[:, LANES * k:LANES * (k + 1)] = pair.T * gate


def _nsa_cmp(qn, ckv, ckvt, biasc, ovt, pq, gates, batch, tq):
    m = qn.shape[0]
    t = m // batch
    nt = t // tq
    ncp = t // CMP_STRIDE
    topn = min(SLC_TOPN, -(-t // SLC_BLOCK))
    return pl.pallas_call(
        functools.partial(_nsa_cmp_body, tq=tq, ncp=ncp, topn=topn),
        grid=(batch, nt),
        in_specs=[pl.BlockSpec((tq, TOK_WIDTH), lambda b, i: (b * nt + i, 0)),
                  pl.BlockSpec((ncp, 3 * LANES), lambda b, i: (b, 0)),
                  pl.BlockSpec((None, 3 * LANES, ncp), lambda b, i: (b, 0, 0)),
                  pl.BlockSpec((1, NSA_G, ncp, NSA_R * tq), lambda b, i: (i, 0, 0, 0)),
                  pl.BlockSpec((LANES, ncp), lambda b, i: (0, 0)),
                  pl.BlockSpec((TOK_HEADS, TOK_WIDTH, LANES), lambda b, i: (0, 0, 0)),
                  pl.BlockSpec((tq, LANES), lambda b, i: (b * nt + i, 0))],
        out_specs=[pl.BlockSpec((1, NSA_G, LANES, tq), lambda b, i: (b * nt + i, 0, 0, 0)),
                   pl.BlockSpec((tq, TOK_WIDTH), lambda b, i: (b * nt + i, 0))],
        out_shape=[jax.ShapeDtypeStruct((batch * nt, NSA_G, LANES, tq), BF16),
                   jax.ShapeDtypeStruct((m, TOK_WIDTH), F32)],
        compiler_params=_cp(("parallel", "parallel")),
        name="nsa_cmp_select",
    )(qn, ckv, ckvt, biasc, ovt, pq, gates)


N_NEAR = 8
PAIR = 2


def _nsa_slc_body(q_ref, kvb_ref, vt_ref, sel_ref, part_ref, g_ref, bt_ref, pq_ref, o_ref,
                  qt_s, m_s, acc_s, *, tq):
    i = pl.program_id(1)
    rows4 = NSA_R * tq
    q = q_ref[...]
    for g in range(NSA_G):
        qt_s[g, 0:LANES, :] = (_stack_heads(q, pq_ref, g, tq, F32) * SCALE).T.astype(BF16)
        unselected = ((sel_ref[0, g].astype(F32) - 1.0) * (-NEG)).astype(BF16)
        qt_s[g, LANES:2 * LANES, :] = jnp.concatenate([unselected] * NSA_R, axis=1)
    ones_rows = {n: jnp.ones((2 * 8, n * LANES), BF16) for n in (1, PAIR)}
    kk = lax.broadcasted_iota(jnp.int32, (LANES, rows4), 0)
    tt = lax.broadcasted_iota(jnp.int32, (LANES, rows4), 1) % tq
    causal = jnp.where(kk <= tt, 0.0, NEG)
    oldest = jnp.where(kk > tt, 0.0, NEG)
    khalf = {n: lax.broadcasted_iota(jnp.int32, (n * LANES, LANES), 0) // SLC_BLOCK for n in (1, PAIR)}
    jcol = {n: lax.broadcasted_iota(jnp.int32, (n * LANES, LANES), 1) for n in (1, PAIR)}
    lane = lax.broadcasted_iota(jnp.int32, (tq, LANES), 1)
    gates = g_ref[...]
    tok = [part_ref[:, LANES * k:LANES * (k + 1)] for k in range(2 * NSA_G)]

    def tile_step(kt, branch, extra, ntile=1):
        koff = 3 * LANES * branch
        nk = ntile * LANES
        ks = pl.multiple_of(kt * LANES, LANES)
        groups = range(NSA_G)
        kblk = [kvb_ref[pl.ds(ks, nk), pl.ds(LANES * ((6 + g) // 2) + koff, LANES)] for g in groups]
        vt = [jnp.concatenate(
            [jnp.concatenate([vt_ref[kt + u, pl.ds(HEAD_DIM * (NSA_G * branch + g), HEAD_DIM), :]
                              for u in range(ntile)], axis=1), ones_rows[ntile]], axis=0) for g in groups]
        bias = [jnp.concatenate([bt_ref[jnp.minimum(i - kt - u, N_NEAR), g] for u in range(ntile)], axis=0)
                for g in groups]
        if branch == 0:
            e_mat = jnp.where(jcol[ntile] == 2 * kt + khalf[ntile], 1.0, 0.0).astype(BF16)
            s = [_dot(jnp.concatenate([kblk[g], e_mat], axis=1), qt_s[g]) + bias[g] for g in groups]
        else:
            s = [_dot(kblk[g], qt_s[g, 0:LANES, :]) + bias[g] for g in groups]
        if extra is not None:
            s = [s[g] + extra for g in groups]
        m_old = [m_s[g] for g in groups]
        m_new = [jnp.maximum(m_old[g], jnp.max(s[g], axis=0, keepdims=True)) for g in groups]
        p = [jnp.exp(s[g] - m_new[g]).astype(BF16) for g in groups]
        alpha = [jnp.exp(m_old[g] - m_new[g]) for g in groups]
        pv = [_dot(vt[g], p[g]) for g in groups]
        for g in groups:
            acc_s[g] = alpha[g] * acc_s[g] + pv[g]
            m_s[g] = m_new[g]

    for branch in range(2):
        for g in range(NSA_G):
            m_s[g] = jnp.full((1, rows4), NEG, F32)
            acc_s[g] = jnp.zeros(acc_s.shape[1:], F32)

        if branch == 0:
            lo = 0
        else:
            nback = WINDOW // LANES

            @pl.when(i >= nback)
            def _():
                tile_step(i - nback, 1, oldest)
            lo = jnp.maximum(i - nback + 1, 0)

        def body(pp, carry, branch=branch, lo=lo):
            tile_step(lo + PAIR * pp, branch, None, PAIR)
            return carry
        lax.fori_loop(0, (i - lo) // PAIR, body, 0)

        @pl.when((i - lo) % PAIR == 1)
        def _(branch=branch):
            tile_step(i - 1, branch, None)
        tile_step(i, branch, causal)
        base = TOK_HEADS * (branch + 1)
        for g in range(NSA_G):
            acc = acc_s[g]
            o_t = acc[0:HEAD_DIM] / jnp.maximum(acc[HEAD_DIM:HEAD_DIM + 1], 1e-30)
            for rp in range(2):
                h0 = NSA_R * g + 2 * rp
                pair = jnp.concatenate([o_t[:, 2 * rp * tq:(2 * rp + 1) * tq],
                                        o_t[:, (2 * rp + 1) * tq:(2 * rp + 2) * tq]], axis=0)
                gate = jnp.where(lane < HALF, gates[:, base + h0:base + h0 + 1],
                                 gates[:, base + h0 + 1:base + h0 + 2])
                tok[2 * g + rp] = tok[2 * g + rp] + pair.T * gate
    for k in range(2 * NSA_G):
        o_ref[:, LANES * k:LANES * (k + 1)] = tok[k].astype(o_ref.dtype)


def _nsa_slc_win(qn, kvb, vt, sel, part, gates, bt, pq, batch):
    tq = LANES
    m = qn.shape[0]
    t = m // batch
    nt = t // tq
    rows4 = NSA_R * tq
    return pl.pallas_call(
        functools.partial(_nsa_slc_body, tq=tq),
        grid=(batch, nt),
        in_specs=[pl.BlockSpec((tq, TOK_WIDTH), lambda b, i: (b * nt + i, 0)),
                  pl.BlockSpec((t, 9 * LANES), lambda b, i: (b, 0)),
                  pl.BlockSpec((None, nt, 2 * NSA_G * HEAD_DIM, LANES), lambda b, i: (b, 0, 0, 0)),
                  pl.BlockSpec((1, NSA_G, LANES, tq), lambda b, i: (b * nt + i, 0, 0, 0)),
                  pl.BlockSpec((tq, TOK_WIDTH), lambda b, i: (b * nt + i, 0)),
                  pl.BlockSpec((tq, LANES), lambda b, i: (b * nt + i, 0)),
                  pl.BlockSpec((N_NEAR + 1, NSA_G, LANES, rows4), lambda b, i: (0, 0, 0, 0)),
                  pl.BlockSpec((TOK_HEADS, TOK_WIDTH, LANES), lambda b, i: (0, 0, 0))],
        out_specs=pl.BlockSpec((tq, TOK_WIDTH), lambda b, i: (b * nt + i, 0)),
        out_shape=jax.ShapeDtypeStruct((m, TOK_WIDTH), BF16),
        scratch_shapes=[pltpu.VMEM((NSA_G, 2 * LANES, rows4), BF16), pltpu.VMEM((NSA_G, 1, rows4), F32),
                        pltpu.VMEM((NSA_G, HEAD_DIM + 16, rows4), F32)],
        compiler_params=_cp(("parallel", "parallel")),
        name="nsa_slc_win",
    )(qn, kvb, vt, sel, part, gates, bt, pq)


def _nsa_sample_body(pt_ref, q_ref, new_ref, neww_ref, g_ref, win_ref, *rest, t, n_pages):
    page_refs = rest[:n_pages]
    (wc_ref, w1_ref, pe_ref, w2p_ref, gf_ref, s_ref, pq_ref, ov_ref, e_ref, bc_ref, bs_ref, bw_ref,
     carried_ref, tok_ref, wout_ref, x_buf, n_buf, nw_buf) = rest[n_pages:]
    del pt_ref, carried_ref
    past = n_pages * PAGE
    slen = past + PAGE
    nch = past // CMP_STRIDE
    n_cmp = nch - 1
    nwin = win_ref.shape[1]
    wlen = nwin + PAGE
    rows4 = NSA_R * t

    for c in range(3):
        for p in range(n_pages):
            x_buf[c, p * PAGE:(p + 1) * PAGE, :] = page_refs[p][LANES * c:LANES * (c + 1), :].T
    n_buf[...] = jnp.zeros_like(n_buf)
    n_buf[0:t, :] = new_ref[0]
    nw_buf[...] = jnp.zeros_like(nw_buf)
    nw_buf[0:t, :] = neww_ref[0]
    for c in range(3):
        full = jnp.concatenate([win_ref[LANES * c:LANES * (c + 1), :], nw_buf[:, LANES * c:LANES * (c + 1)].T], axis=1)
        wout_ref[LANES * c:LANES * (c + 1), :] = pltpu.roll(full, wlen - t, 1)[:, 0:nwin]

    def load(p, c):
        return x_buf[c, pl.ds(p, nch, stride=CMP_STRIDE), :]
    ckv = _compress(load, nch, wc_ref, w1_ref, pe_ref, w2p_ref, gf_ref, s_ref[...]).astype(BF16)

    q = jnp.concatenate([q_ref[0], jnp.zeros((8, TOK_WIDTH), F32)], axis=0).astype(BF16)
    new_tiles = [n_buf[:, LANES * k:LANES * (k + 1)].astype(BF16) for k in (3, 4, 5)]
    neww_tiles = [nw_buf[:, LANES * k:LANES * (k + 1)].astype(BF16) for k in (0, 1, 2)]
    win_tiles = [win_ref[LANES * k:LANES * (k + 1), :].astype(BF16) for k in (0, 1, 2)]
    tile_cache = {}

    def page_tile(k, p):
        if (k, p) not in tile_cache:
            tile_cache[(k, p)] = page_refs[p][LANES * k:LANES * (k + 1), :].astype(BF16)
        return tile_cache[(k, p)]
    gates = g_ref[0]
    lane = lax.broadcasted_iota(jnp.int32, (t, LANES), 1)
    lane3 = lax.broadcasted_iota(jnp.int32, (NSA_G * t, LANES), 1)
    tpos = past + lax.broadcasted_iota(jnp.int32, (NSA_G * t, LANES), 0) % t
    valid = lane3 * SLC_BLOCK <= tpos
    curb = tpos // SLC_BLOCK
    forced = (lane3 == 0) | (lane3 == curb) | (lane3 == curb - 1)
    mask_c = lax.broadcasted_iota(jnp.int32, (rows4, nch), 1) < n_cmp
    trow_s = lax.broadcasted_iota(jnp.int32, (rows4, slen), 0) % t
    kcol = lax.broadcasted_iota(jnp.int32, (rows4, slen), 1)
    vis_s = kcol <= past + trow_s
    trow_w = lax.broadcasted_iota(jnp.int32, (rows4, wlen), 0) % t
    wcol = lax.broadcasted_iota(jnp.int32, (rows4, wlen), 1)
    vis_w = jnp.logical_and(wcol > trow_w, wcol <= nwin + trow_w)

    groups = range(NSA_G)
    ki = (0, 0, 1)
    vi = (1, 2, 2)
    qg = [_stack_heads(q, pq_ref, g, t) * SCALE for g in groups]
    sc = [_dot_t(qg[g], ckv[:, LANES * (g // 2):LANES * (g // 2 + 1)]) + bc_ref[g] for g in groups]
    pc = [_masked_softmax(sc[g], mask_c).astype(BF16) for g in groups]
    o_c = [_dot(pc[g], ckv[:, LANES * ((3 + g) // 2):LANES * ((3 + g) // 2 + 1)]) for g in groups]
    imp4 = [_dot(pc[g], ov_ref[...]) for g in groups]
    imp = jnp.concatenate([imp4[g][0:t] + imp4[g][t:2 * t] + imp4[g][2 * t:3 * t] + imp4[g][3 * t:4 * t]
                           for g in groups], axis=0)
    score = jnp.where(valid, imp + jnp.where(forced, FORCE_BONUS, 0.0), -jnp.inf)
    sel = _select_topn(score, SLC_TOPN).astype(BF16)
    ss = [jnp.concatenate([_dot(qg[g], page_tile(3 + ki[g], p)) for p in range(n_pages)]
                          + [_dot_t(qg[g], new_tiles[ki[g]])], axis=1) + bs_ref[g] for g in groups]
    selx = _dot(sel, e_ref[...])
    ps = [_masked_softmax(ss[g], jnp.logical_and(
        vis_s, jnp.concatenate([selx[g * t:(g + 1) * t]] * NSA_R, axis=0) > 0.5)).astype(BF16) for g in groups]
    o_s = [_dot(ps[g][:, past:slen], new_tiles[vi[g]]) for g in groups]
    for p in range(n_pages):
        o_s = [o_s[g] + _dot_t(ps[g][:, p * PAGE:(p + 1) * PAGE], page_tile(3 + vi[g], p)) for g in groups]
    sw = [jnp.concatenate([_dot(qg[g], win_tiles[ki[g]]), _dot_t(qg[g], neww_tiles[ki[g]])], axis=1) + bw_ref[g]
          for g in groups]
    pw = [_masked_softmax(sw[g], vis_w).astype(BF16) for g in groups]
    o_w = [_dot_t(pw[g][:, 0:nwin], win_tiles[vi[g]]) + _dot(pw[g][:, nwin:wlen], neww_tiles[vi[g]]) for g in groups]

    def gate_col(c, g):
        return jnp.concatenate([gates[:, TOK_HEADS * c + NSA_R * g + r:TOK_HEADS * c + NSA_R * g + r + 1]
                                for r in range(NSA_R)], axis=0)
    for g in groups:
        o = gate_col(0, g) * o_c[g] + gate_col(1, g) * o_s[g] + gate_col(2, g) * o_w[g]
        tiles = _place_heads(o, g, t, lambda h: 1.0, lane)
        for rp in range(2):
            k = 2 * g + rp
            tok_ref[0, :, LANES * k:LANES * (k + 1)] = tiles[rp]


def _nsa_sample(q, new_rows, new_w, gates, win_cache, pool, layer, page_table, cw, pq, ov, e_mat, bc, bs, bw, win_out):
    batch, t, _ = q.shape
    n_pages = page_table.shape[1]
    past = n_pages * PAGE
    nwin = win_cache.shape[3]
    nch = past // CMP_STRIDE
    rows4 = NSA_R * t
    page_specs = [
        pl.BlockSpec((None, None, TOK_WIDTH, PAGE), lambda b, pt, p=p: (layer, pt[b, p], 0, 0))
        for p in range(n_pages)
    ]
    const3 = lambda b, pt: (0, 0, 0)
    const2 = lambda b, pt: (0, 0)
    grid_spec = pltpu.PrefetchScalarGridSpec(
        num_scalar_prefetch=1,
        grid=(batch,),
        in_specs=[pl.BlockSpec((1, t, TOK_WIDTH), lambda b, pt: (b, 0, 0)),
                  pl.BlockSpec((1, t, TOK_WIDTH), lambda b, pt: (b, 0, 0)),
                  pl.BlockSpec((1, t, 3 * LANES), lambda b, pt: (b, 0, 0)),
                  pl.BlockSpec((1, t, LANES), lambda b, pt: (b, 0, 0)),
                  pl.BlockSpec((None, None, 3 * LANES, nwin), lambda b, pt: (layer, b, 0, 0))]
        + page_specs + _cmp_const_specs()
        + [pl.BlockSpec((TOK_HEADS, TOK_WIDTH, LANES), const3),
           pl.BlockSpec((nch, LANES), const2),
           pl.BlockSpec((LANES, past + PAGE), const2),
           pl.BlockSpec((NSA_G, rows4, nch), const3),
           pl.BlockSpec((NSA_G, rows4, past + PAGE), const3),
           pl.BlockSpec((NSA_G, rows4, nwin + PAGE), const3),
           pl.BlockSpec(memory_space=pl.ANY)],
        out_specs=[pl.BlockSpec((1, t, TOK_WIDTH), lambda b, pt: (b, 0, 0)),
                   pl.BlockSpec((None, None, 3 * LANES, nwin), lambda b, pt: (layer, b, 0, 0))],
        scratch_shapes=[pltpu.VMEM((3, past, LANES), F32), pltpu.VMEM((PAGE, TOK_WIDTH), F32),
                        pltpu.VMEM((PAGE, 3 * LANES), F32)],
    )
    args = (page_table, q, new_rows, new_w, gates, win_cache, *([pool] * n_pages), *cw, _group_ones(),
            pq, ov, e_mat, bc, bs, bw, win_out)
    return pl.pallas_call(
        functools.partial(_nsa_sample_body, t=t, n_pages=n_pages),
        grid_spec=grid_spec,
        out_shape=[jax.ShapeDtypeStruct((batch, t, TOK_WIDTH), F32),
                   jax.ShapeDtypeStruct(win_out.shape, F32)],
        input_output_aliases={len(args) - 1: 1},
        compiler_params=_cp(("arbitrary",)),
        name="nsa_sample",
    )(*args)


def _rel_bucket(dist):
    exact = N_BUCKETS // 2
    d = jnp.maximum(dist, 0)
    far = exact + (jnp.log(jnp.maximum(d, 1).astype(F32) / exact)
                   / math.log(MAX_DIST / exact) * (N_BUCKETS - exact)).astype(jnp.int32)
    return jnp.where(d < exact, d, jnp.minimum(far, N_BUCKETS - 1))


NEG_PAD = 1024


def _dist_table(rel_bias, n):
    tab = rel_bias[_rel_bucket(jnp.arange(n, dtype=jnp.int32))].T.astype(F32)
    return jnp.pad(tab, ((0, 0), (NEG_PAD, 0)))


def _skew(v, nrows, step):
    p = v.shape[-1]
    flat = jnp.tile(v, (1,) * (v.ndim - 1) + (nrows,))[..., :nrows * (p - step)]
    return flat.reshape(v.shape[:-1] + (nrows, p - step))


def _toeplitz(tab, c0, nrows, ncols):
    lo = c0 - ncols + 1 + NEG_PAD
    hi = c0 + nrows + NEG_PAD
    u = jnp.flip(tab[:, lo:hi], axis=1)
    v = jnp.roll(u, -(nrows - 1), axis=1)
    return _skew(v, nrows, 1)[..., :ncols]


def _group_rows(m):
    return m.reshape(NSA_G, NSA_R * m.shape[1], m.shape[2])


def _head_placement():
    pq = np.zeros((TOK_HEADS, TOK_WIDTH, LANES), np.float32)
    d = np.arange(HEAD_DIM)
    for h in range(TOK_HEADS):
        pq[h, HEAD_DIM * h + d, HALF * ((h // NSA_R) % 2) + d] = 1.0
    return jnp.asarray(pq, BF16)


def _overlap(nrows, n_cmp, n_slc):
    cs = np.arange(nrows)[:, None] * CMP_STRIDE
    ss = np.arange(LANES)[None, :] * SLC_BLOCK
    ov = (cs < ss + SLC_BLOCK) & (cs + CMP_BLOCK > ss)
    ov &= (np.arange(nrows)[:, None] < n_cmp) & (np.arange(LANES)[None, :] < n_slc)
    return jnp.asarray(ov.astype(np.float32), BF16)


def _cmp_weights(pe, w1, w2, kn_cmp):
    w1b = w1.astype(BF16)
    w1p = w1b.reshape(2, 2, CMP_STRIDE, HEAD_DIM, CMP_HID)
    cat = jnp.concatenate([w1p[:, 0], w1p[:, 1]], axis=-1)
    z = jnp.zeros_like(cat[0])

    def pair(ta, tb):
        top = jnp.concatenate([cat[ta], z], axis=-1)
        bot = jnp.concatenate([z, cat[tb]], axis=-1)
        return jnp.concatenate([top, bot], axis=1)
    wc = jnp.stack([pair(0, 0), pair(0, 1), pair(1, 1)])
    pe8 = jnp.pad(pe.reshape(2, 1, CMP_BLOCK * HEAD_DIM), ((0, 0), (0, 7), (0, 0))).astype(BF16)
    w2b = w2.astype(BF16)
    zz = jnp.zeros_like(w2b)
    w2p = jnp.stack([jnp.concatenate([w2b, zz], axis=-1), jnp.concatenate([zz, w2b], axis=-1)], axis=1)
    kvw = NSA_G * HEAD_DIM
    gain = jnp.concatenate([jnp.tile(kn_cmp, NSA_G), jnp.ones((kvw,), F32)])
    flag = jnp.concatenate([jnp.ones((kvw,), F32), jnp.zeros((kvw,), F32)])
    return wc, w1b, pe8, w2p, jnp.stack([gain, flag])


def _prompt_tables(rel_bias, t, tq):
    ncp = t // CMP_STRIDE
    nt = t // tq
    n_cmp = (t - CMP_BLOCK) // CMP_STRIDE + 1
    n_slc = -(-t // SLC_BLOCK)
    tab = _dist_table(rel_bias, max(t, MAX_DIST) + LANES)
    per = LANES // CMP_STRIDE
    lo = [NEG_PAD - CMP_STRIDE * b - (CMP_BLOCK - 1) for b in range(per)]
    base = jnp.stack([tab[:, x:x + t + LANES] for x in lo], axis=1)
    bc = _skew(base, ncp // per, LANES)
    bc = bc.reshape(NSA_G, NSA_R, per, ncp // per, nt, tq)
    bc = jnp.transpose(bc, (4, 0, 3, 2, 1, 5)).reshape(nt, NSA_G, ncp, NSA_R * tq)
    near = [_group_rows(_toeplitz(tab, LANES * d, LANES, LANES)) for d in range(N_NEAR)]
    far = jnp.broadcast_to(tab[:, -1].reshape(NSA_G, NSA_R, 1, 1), (NSA_G, NSA_R, LANES, LANES))
    bt = jnp.stack(near + [far.reshape(NSA_G, NSA_R * LANES, LANES)])
    bt = jnp.transpose(bt, (0, 1, 3, 2))
    return bc, bt, _overlap(ncp, n_cmp, n_slc).T


def _sample_tables(rel_bias, past, t, nwin):
    nch = past // CMP_STRIDE
    n_cmp = (past + t - CMP_BLOCK) // CMP_STRIDE + 1
    n_slc = -(-(past + t) // SLC_BLOCK)
    tab = _dist_table(rel_bias, past + LANES)
    dist_c = past + np.arange(t)[:, None] - (np.arange(nch)[None, :] * CMP_STRIDE + CMP_BLOCK - 1)
    bc = _group_rows(tab[:, np.maximum(dist_c, -NEG_PAD) + NEG_PAD])
    bs = _group_rows(_toeplitz(tab, past, t, past + PAGE))
    bw = _group_rows(_toeplitz(tab, nwin, t, nwin + PAGE))
    e_mat = (np.arange(LANES)[:, None] == np.arange(past + PAGE)[None, :] // SLC_BLOCK)
    return bc, bs, bw, _overlap(nch, n_cmp, n_slc), jnp.asarray(e_mat.astype(np.float32), BF16)


def _nsa_prompt_mix(qn, kcvc, kvb, vt, gates, cw, tables, pq, batch):
    bc, bt, ovt = tables
    ckv, ckvt = _nsa_compress(kcvc, cw, batch)
    sel, part = _nsa_cmp(qn, ckv, ckvt, bc, ovt, pq, gates, batch, LANES)
    return _nsa_slc_win(qn, kvb, vt, sel, part, gates, bt, pq, batch)


TM = 512
TH = 1408
MEM_NB = 8


def kernel(x_prompt, x_sample, mem_prompt, cache_nsa_kv, cache_nsa_win, cache_sb_kv, cache_mem_kv, page_table,
           rel_bias, norm_mix, norm_ffn, norm_mem, w_in_nsa, w_in_sb, w_mem_kv, w_out, nsa_qk_norm, x_qk_norm,
           cmp_pe, cmp_w1, cmp_w2, w_ffn_in, w_ffn_out):
    batch, seq, _ = x_prompt.shape
    dbatch, dseq, _ = x_sample.shape
    depth = norm_mix.shape[0]
    n_pages = page_table.shape[1]
    past = n_pages * PAGE
    nwin = cache_nsa_win.shape[2]
    kvw = NSA_G * HEAD_DIM

    xp = x_prompt.reshape(batch * seq, D_MODEL)
    xs = x_sample.reshape(dbatch * dseq, D_MODEL)
    mem = mem_prompt.reshape(batch * N_MEM, D_MODEL)
    def feature_major(c):
        ct = jnp.transpose(c, (0, 1, 3, 4, 5, 2))
        return ct.reshape(c.shape[0], c.shape[1], -1, c.shape[2])
    def token_major(a, feat):
        return jnp.transpose(a.reshape((a.shape[0],) + feat + (a.shape[2],)), (0, 4, 1, 2, 3))
    nsa_pool = feature_major(cache_nsa_kv)
    sb_pool = feature_major(cache_sb_kv)
    win_cache = feature_major(cache_nsa_win)
    mem_cache = feature_major(cache_mem_kv)

    pq = _head_placement()
    ptab = _prompt_tables(rel_bias, seq, LANES)
    bc_s, bs_s, bw_s, ov_s, e_s = _sample_tables(rel_bias, past, dseq, nwin)
    wo = w_out.astype(BF16)
    wfi = w_ffn_in.astype(BF16)
    wfo = w_ffn_out.astype(BF16)

    nsa_s, win_p, sb_s, mem_p = [], [], [], []
    n_nsa, n_sb = (depth + 1) // 2, depth // 2
    nsa_rows = jnp.zeros((n_nsa, batch, 4 * kvw, seq), F32)
    sb_rows = jnp.zeros((n_sb, batch, 2 * TOK_WIDTH, seq), F32)
    win_rows = jnp.zeros((n_nsa, dbatch, 2 * kvw, nwin), F32)
    for l in range(depth):
        j = l // 2
        gmix = norm_mix[l][None]
        wm, gfm = _mem_in_weights(w_mem_kv[l], x_qk_norm[l, 1])
        mkvb, mkv_t = _inproj(mem, norm_mem[l][None], wm, gfm, MEM_CHUNKS_P,
                              [(2 * X_WIDTH, BF16), ("T", 2 * X_WIDTH, F32)], N_MEM, MEM_TCHUNKS_P, N_MEM)
        mem_p.append(token_major(mkv_t, (2, X_HEADS, HEAD_DIM)))
        if l % 2 == 0:
            w, gf = _nsa_in_weights(w_in_nsa[j], nsa_qk_norm[j], x_qk_norm[l, 0])
            qn, kcvc, kvb, xqn, gates, nsa_rows, rows_wt, vt = _inproj(
                xp, gmix, w, gf, NSA_CHUNKS_P,
                [(TOK_WIDTH, BF16), (2 * kvw, F32), (6 * kvw, BF16), (X_WIDTH, BF16), (LANES, F32),
                 ("T", 4 * kvw, F32, (n_nsa, j, nsa_rows)), ("T", 2 * kvw, F32), ("tiles", 2 * kvw, BF16)],
                TM, NSA_TCHUNKS_P, seq)
            cw = _cmp_weights(cmp_pe[j], cmp_w1[j], cmp_w2[j], nsa_qk_norm[j, 1])
            tok_p = _nsa_prompt_mix(qn, kcvc, kvb, vt, gates, cw, ptab, pq, batch)
            qs, rows_s, rows_ws, xqs, gates_s = _inproj(
                xs, gmix, w, gf, NSA_CHUNKS_S,
                [(TOK_WIDTH, F32), (4 * kvw, F32), (2 * kvw, F32), (X_WIDTH, F32), (LANES, F32)], TM)
            tok_s, win_rows = _nsa_sample(
                qs.reshape(dbatch, dseq, TOK_WIDTH), rows_s.reshape(dbatch, dseq, 4 * kvw),
                rows_ws.reshape(dbatch, dseq, 2 * kvw), gates_s.reshape(dbatch, dseq, LANES),
                win_cache, nsa_pool, j, page_table, cw, pq, ov_s, e_s, bc_s, bs_s, bw_s, win_rows)
            nsa_s.append(rows_s.reshape(dbatch, dseq, 4, NSA_G, HEAD_DIM))
            win_p.append(token_major(rows_wt[:, :, seq - min(WINDOW, seq):], (2, NSA_G, HEAD_DIM)))
        else:
            w, gf = _sb_in_weights(w_in_sb[j], x_qk_norm[l, 0])
            q, kvb, xqn, sb_rows = _inproj(
                xp, gmix, w, gf, SB_CHUNKS_P,
                [(TOK_WIDTH, BF16), (2 * TOK_WIDTH, BF16), (X_WIDTH, BF16),
                 ("T", 2 * TOK_WIDTH, F32, (n_sb, j, sb_rows))], TM, SB_TCHUNKS_P, seq)
            tok_p = _sb_prompt(q, kvb, batch, LANES)
            qs, rows_s, xqs = _inproj(
                xs, gmix, w, gf, SB_CHUNKS_S, [(TOK_WIDTH, F32), (2 * TOK_WIDTH, F32), (X_WIDTH, F32)], TM)
            tok_s = _sb_sample(qs.reshape(dbatch, dseq, TOK_WIDTH), rows_s.reshape(dbatch, dseq, 2 * TOK_WIDTH),
                               sb_pool, j, page_table)
            sb_s.append(rows_s.reshape(dbatch, dseq, 2, TOK_HEADS, HEAD_DIM))
        memo_p = _memattn_prompt(xqn, mkvb, batch, TM)
        memo_s = _memattn_sample(xqs, mem_cache, l, dseq, MEM_NB)
        g_ffn = norm_ffn[l][None]
        xp = _out_ffn(xp, tok_p, memo_p, wo[l, :TOK_WIDTH], wo[l, TOK_WIDTH:], g_ffn, wfi[l], wfo[l], TM, TH)
        xs = _out_ffn(xs, tok_s.reshape(dbatch * dseq, TOK_WIDTH), memo_s, wo[l, :TOK_WIDTH], wo[l, TOK_WIDTH:],
                      g_ffn, wfi[l], wfo[l], TM, TH)
    def layers_token_major(a, feat):
        return jnp.transpose(a.reshape(a.shape[:2] + feat + (a.shape[3],)), (0, 1, 5, 2, 3, 4))
    return (xp.reshape(batch, seq, D_MODEL), xs.reshape(dbatch, dseq, D_MODEL),
            layers_token_major(nsa_rows, (4, NSA_G, HEAD_DIM)), jnp.stack(nsa_s),
            jnp.stack(win_p), layers_token_major(win_rows, (2, NSA_G, HEAD_DIM)),
            layers_token_major(sb_rows, (2, TOK_HEADS, HEAD_DIM)),
            jnp.stack(sb_s), jnp.stack(mem_p))
```
